```python
import math
import jax, jax.numpy as jnp
from jax import lax
import numpy as np

D_MODEL = 1024
BATCH = 16
SEQ = 2048
DEPTH = 1

DIFF_HEADS = 4
DIFF_HD = 64
DIFF_VD = 2 * DIFF_HD
DIFF_WIDTH = DIFF_HEADS * DIFF_VD
NSA_HEADS = 8
NSA_KV_HEADS = 2
NSA_GROUP = NSA_HEADS // NSA_KV_HEADS
NSA_HD = 64
NSA_WIDTH = NSA_HEADS * NSA_HD
MIX_WIDTH = DIFF_WIDTH + NSA_WIDTH
CMP_LEN = 32
CMP_STRIDE = 16
CMP_HIDDEN = 256
SEL_LEN = 64
SEL_TOPN = 8
WINDOW = 512
Q_BLOCK = 128
FORCE_BONUS = 1000.0
NEG_INF = -1e30
N_BUCKETS = 32
MAX_DISTANCE = 128
N_BIAS_HEADS = DIFF_HEADS + NSA_HEADS
PEER_HEADS = 8
PEER_NKEYS = 128
PEER_EXPERTS = PEER_NKEYS ** 2
PEER_DKEY = 256
PEER_TOPK = 16
PEER_CHUNK = 128
EPS = 1e-6

KV_COLS = NSA_KV_HEADS * NSA_HD
SPLIT_SIZES = (DIFF_HEADS * 2 * DIFF_HD,
               DIFF_HEADS * 2 * DIFF_HD,
               DIFF_WIDTH,
               NSA_WIDTH,
               KV_COLS, KV_COLS,
               KV_COLS, KV_COLS,
               KV_COLS, KV_COLS,
               3 * NSA_HEADS)
IN_COLS = sum(SPLIT_SIZES)
SPLIT_IDX = tuple(int(v) for v in np.cumsum(SPLIT_SIZES)[:-1])

kernel_name = 'hybrid_diffattn_nsa_peer_layer'


def rms_norm(x, g):
    xf = x.astype(jnp.float32)
    y = xf * lax.rsqrt(jnp.mean(xf * xf, axis=-1, keepdims=True) + EPS)
    return (y * g.astype(jnp.float32)).astype(x.dtype)


def t5_bucket(rel):
    n = jnp.maximum(rel, 0)
    max_exact = N_BUCKETS // 2
    nf = jnp.maximum(n, 1).astype(jnp.float32)
    large = max_exact + (jnp.log(nf / max_exact) / math.log(MAX_DISTANCE / max_exact)
                         * (N_BUCKETS - max_exact)).astype(jnp.int32)
    large = jnp.minimum(large, N_BUCKETS - 1)
    return jnp.where(n < max_exact, n, large)


def diff_attention(q, k, v, bias_tab, lq1, lk1, lq2, lk2, subln_g, layer_idx):
    B, S = q.shape[:2]
    nqb = S // Q_BLOCK
    f32 = jnp.float32
    scale = DIFF_HD ** -0.5
    lambda_init = 0.8 - 0.6 * math.exp(-0.3 * layer_idx)
    lam = (jnp.exp(jnp.sum(lq1.astype(f32) * lk1.astype(f32)))
           - jnp.exp(jnp.sum(lq2.astype(f32) * lk2.astype(f32))) + lambda_init)
    qb = q.reshape(B, nqb, Q_BLOCK, DIFF_HEADS, 2, DIFF_HD).swapaxes(0, 1)
    kpos = jnp.arange(S)

    def block(args):
        i, qi = args
        qpos = i * Q_BLOCK + jnp.arange(Q_BLOCK)
        rel = qpos[:, None] - kpos[None, :]
        bias = bias_tab[t5_bucket(rel)].transpose(2, 0, 1)
        s = (jnp.einsum('bqhcd,bkhcd->bhcqk', qi, k).astype(f32) * scale
             + bias[None, :, None].astype(f32))
        s = jnp.where((rel >= 0)[None, None, None], s, NEG_INF)
        p = jax.nn.softmax(s, axis=-1)
        a = p[:, :, 0] - lam * p[:, :, 1]
        return jnp.einsum('bhqk,bkhe->bqhe', a.astype(v.dtype), v)

    o = lax.map(block, (jnp.arange(nqb), qb))
    o = o.swapaxes(0, 1).reshape(B, S, DIFF_HEADS, DIFF_VD)
    o = rms_norm(o, subln_g) * (1.0 - lambda_init)
    return o.reshape(B, S, DIFF_WIDTH)


def compress_blocks(kv, pos_emb, w1, w2):
    B, S, Hkv, d = kv.shape
    n_cmp = (S - CMP_LEN) // CMP_STRIDE + 1
    idx = np.arange(n_cmp)[:, None] * CMP_STRIDE + np.arange(CMP_LEN)[None, :]
    blocks = kv[:, idx] + pos_emb[None, None, :, None, :]
    flat = jnp.moveaxis(blocks, 3, 2).reshape(B, n_cmp, Hkv, CMP_LEN * d)
    return jax.nn.gelu(flat @ w1) @ w2


def cmp_to_sel_matrix(n_cmp, n_sel):
    r = SEL_LEN // CMP_STRIDE
    c = CMP_LEN // CMP_STRIDE
    off = np.arange(n_cmp)[:, None] - r * np.arange(n_sel)[None, :] + (c - 1)
    counts = np.zeros((n_cmp, n_sel), np.float32)
    for n in range(c):
        counts += ((off - n >= 0) & (off - n < r)).astype(np.float32)
    return jnp.asarray(counts)


def nsa_attention(q, kc, vc, ks, vs, kw, vw, gate_logits, bias_tab,
                  pos_k, pos_v, w1_k, w2_k, w1_v, w2_v):
    B, S = q.shape[:2]
    Hkv, G, d = NSA_KV_HEADS, NSA_GROUP, NSA_HD
    f32 = jnp.float32
    scale = d ** -0.5
    qg = q.reshape(B, S, Hkv, G, d)
    pos = jnp.arange(S)

    n_cmp = (S - CMP_LEN) // CMP_STRIDE + 1
    k_cmp = compress_blocks(kc, pos_k, w1_k, w2_k)
    v_cmp = compress_blocks(vc, pos_v, w1_v, w2_v)
    cmp_end = jnp.arange(n_cmp) * CMP_STRIDE + CMP_LEN - 1
    cmp_ok = (cmp_end[None, :] <= pos[:, None])[None, :, None, None]
    s_c = jnp.einsum('bshgd,bchd->bshgc', qg, k_cmp).astype(f32) * scale
    s_c = jnp.where(cmp_ok, s_c, NEG_INF)
    p_c = jnp.where(cmp_ok, jax.nn.softmax(s_c, axis=-1), 0.0)
    o_cmp = jnp.einsum('bshgc,bchd->bshgd', p_c.astype(vc.dtype), v_cmp)

    n_sel = S // SEL_LEN
    top_n = min(SEL_TOPN, n_sel)
    imp = jnp.einsum('bshc,cj->bshj', p_c.sum(axis=3), cmp_to_sel_matrix(n_cmp, n_sel))
    blk = jnp.arange(n_sel)[None, :]
    cur = (pos // SEL_LEN)[:, None]
    sel_ok = (blk <= cur)[None, :, None]
    forced = ((blk == 0) | (blk == cur) | (blk == cur - 1)).astype(f32)[None, :, None]
    score = jnp.where(sel_ok, imp + FORCE_BONUS * forced, NEG_INF)
    top_s, sel_idx = lax.top_k(score, top_n)
    sel_valid = top_s > 0.5 * NEG_INF

    nqb = S // Q_BLOCK
    ks_blk = ks.reshape(B, n_sel, SEL_LEN, Hkv, d)
    vs_blk = vs.reshape(B, n_sel, SEL_LEN, Hkv, d)
    pad = ((0, 0), (WINDOW, 0), (0, 0), (0, 0))
    kw_pad = jnp.pad(kw, pad)
    vw_pad = jnp.pad(vw, pad)
    bias_g = bias_tab.reshape(N_BUCKETS, Hkv, G)
    b_i = jnp.arange(B)[:, None, None, None]
    h_i = jnp.arange(Hkv)[None, None, :, None]
    h_i5 = jnp.arange(Hkv)[None, None, :, None, None]

    def to_blocks(a):
        return a.reshape(B, nqb, Q_BLOCK, *a.shape[2:]).swapaxes(0, 1)

    def block(args):
        i, qb, idx, ok = args
        qpos = i * Q_BLOCK + jnp.arange(Q_BLOCK)
        k_sel = ks_blk[b_i, idx, :, h_i, :]
        v_sel = vs_blk[b_i, idx, :, h_i, :]
        kpos = idx[..., None] * SEL_LEN + jnp.arange(SEL_LEN)
        rel = qpos[None, :, None, None, None] - kpos
        m = (ok[..., None] & (rel >= 0))[:, :, :, None]
        bias = jnp.moveaxis(bias_g[t5_bucket(rel), h_i5], -1, 3)
        s = (jnp.einsum('bqhgd,bqhnld->bqhgnl', qb, k_sel).astype(f32) * scale
             + bias.astype(f32))
        s = jnp.where(m, s, NEG_INF).reshape(B, Q_BLOCK, Hkv, G, top_n * SEL_LEN)
        p = jax.nn.softmax(s, axis=-1)
        o_sel = jnp.einsum('bqhgk,bqhkd->bqhgd', p.astype(vs.dtype),
                           v_sel.reshape(B, Q_BLOCK, Hkv, top_n * SEL_LEN, d))
        k_win = lax.dynamic_slice_in_dim(kw_pad, i * Q_BLOCK, WINDOW + Q_BLOCK, axis=1)
        v_win = lax.dynamic_slice_in_dim(vw_pad, i * Q_BLOCK, WINDOW + Q_BLOCK, axis=1)
        wpos = i * Q_BLOCK - WINDOW + jnp.arange(WINDOW + Q_BLOCK)
        relw = qpos[:, None] - wpos[None, :]
        mw = (relw >= 0) & (relw < WINDOW) & (wpos >= 0)[None, :]
        biasw = bias_g[t5_bucket(relw)].transpose(0, 2, 3, 1)
        sw = (jnp.einsum('bqhgd,bkhd->bqhgk', qb, k_win).astype(f32) * scale
              + biasw[None].astype(f32))
        sw = jnp.where(mw[None, :, None, None, :], sw, NEG_INF)
        pw = jax.nn.softmax(sw, axis=-1)
        o_win = jnp.einsum('bqhgk,bkhd->bqhgd', pw.astype(vw.dtype), v_win)
        return o_sel, o_win

    o_sel, o_win = lax.map(block, (jnp.arange(nqb), to_blocks(qg),
                                   to_blocks(sel_idx), to_blocks(sel_valid)))
    o_sel = o_sel.swapaxes(0, 1).reshape(B, S, Hkv, G, d)
    o_win = o_win.swapaxes(0, 1).reshape(B, S, Hkv, G, d)

    g = jax.nn.sigmoid(gate_logits.astype(f32)).reshape(B, S, Hkv, G, 3)
    o = (g[..., 0:1] * o_cmp.astype(f32) + g[..., 1:2] * o_sel.astype(f32)
         + g[..., 2:3] * o_win.astype(f32))
    return o.astype(q.dtype).reshape(B, S, NSA_WIDTH)


def peer_ffn(x, w_q, sub_keys, u, v):
    B, S, D = x.shape
    xt = x.reshape(B * S // PEER_CHUNK, PEER_CHUNK, D)
    f32 = jnp.float32

    def chunk(xc):
        q = (xc @ w_q).reshape(PEER_CHUNK, PEER_HEADS, 2, PEER_DKEY // 2)
        s = jnp.einsum('chpk,hpnk->chpn', q, sub_keys).astype(f32)
        sv, si = lax.top_k(s, PEER_TOPK)
        cand = sv[:, :, 0, :, None] + sv[:, :, 1, None, :]
        cand_idx = si[:, :, 0, :, None] * PEER_NKEYS + si[:, :, 1, None, :]
        cv, ci = lax.top_k(cand.reshape(PEER_CHUNK, PEER_HEADS, PEER_TOPK * PEER_TOPK), PEER_TOPK)
        e = jnp.take_along_axis(cand_idx.reshape(PEER_CHUNK, PEER_HEADS, -1), ci, axis=-1)
        gate = jax.nn.softmax(cv, axis=-1)
        h = jax.nn.gelu(jnp.einsum('cd,chkd->chk', xc, u[e]).astype(f32))
        return jnp.einsum('chk,chkd->cd', (gate * h).astype(xc.dtype), v[e])

    return lax.map(chunk, xt).reshape(B, S, D)


def setup_inputs(seed: int = 0) -> dict:
    key = jax.random.key(seed)
    ks = jax.random.split(key, 24)
    nrm = lambda k, shape, s: jax.random.normal(k, shape, jnp.float32) * s
    L, D = DEPTH, D_MODEL
    return {
        'x': nrm(ks[0], (BATCH, SEQ, D), 1.0),
        'w_in': nrm(ks[1], (L, D, IN_COLS), D ** -0.5),
        'w_out': nrm(ks[2], (L, MIX_WIDTH, D), MIX_WIDTH ** -0.5),
        'norm_mix_g': 1.0 + nrm(ks[3], (L, D), 0.02),
        'norm_ffn_g': 1.0 + nrm(ks[4], (L, D), 0.02),
        'final_norm_g': 1.0 + nrm(ks[5], (D,), 0.02),
        'rel_bias': nrm(ks[6], (N_BUCKETS, N_BIAS_HEADS), 0.5),
        'diff_lq1': nrm(ks[7], (L, DIFF_HD), 0.1),
        'diff_lk1': nrm(ks[8], (L, DIFF_HD), 0.1),
        'diff_lq2': nrm(ks[9], (L, DIFF_HD), 0.1),
        'diff_lk2': nrm(ks[10], (L, DIFF_HD), 0.1),
        'diff_subln_g': 1.0 + nrm(ks[11], (L, DIFF_VD), 0.02),
        'cmp_pos_k': nrm(ks[12], (L, CMP_LEN, NSA_HD), 0.02),
        'cmp_pos_v': nrm(ks[13], (L, CMP_LEN, NSA_HD), 0.02),
        'cmp_w1_k': nrm(ks[14], (L, CMP_LEN * NSA_HD, CMP_HIDDEN), (CMP_LEN * NSA_HD) ** -0.5),
        'cmp_w2_k': nrm(ks[15], (L, CMP_HIDDEN, NSA_HD), CMP_HIDDEN ** -0.5),
        'cmp_w1_v': nrm(ks[16], (L, CMP_LEN * NSA_HD, CMP_HIDDEN), (CMP_LEN * NSA_HD) ** -0.5),
        'cmp_w2_v': nrm(ks[17], (L, CMP_HIDDEN, NSA_HD), CMP_HIDDEN ** -0.5),
        'peer_w_q': nrm(ks[18], (L, D, PEER_HEADS * PEER_DKEY), D ** -0.5),
        'peer_sub_keys': nrm(ks[19], (L, PEER_HEADS, 2, PEER_NKEYS, PEER_DKEY // 2), (PEER_DKEY // 2) ** -0.5),
        'peer_u': nrm(ks[20], (L, PEER_EXPERTS, D), D ** -0.5),
        'peer_v': nrm(ks[21], (L, PEER_EXPERTS, D), 0.1),
    }


def reference(x, w_in, w_out, norm_mix_g, norm_ffn_g, final_norm_g, rel_bias,
              diff_lq1, diff_lk1, diff_lq2, diff_lk2, diff_subln_g,
              cmp_pos_k, cmp_pos_v, cmp_w1_k, cmp_w2_k, cmp_w1_v, cmp_w2_v,
              peer_w_q, peer_sub_keys, peer_u, peer_v):
    B, S, _ = x.shape
    for l in range(DEPTH):
        h = rms_norm(x, norm_mix_g[l])
        proj = h @ w_in[l]
        dq, dk, dv, nq, kc, vc, ksel, vsel, kwin, vwin, ng = jnp.split(proj, SPLIT_IDX, axis=-1)
        kv_shape = (B, S, NSA_KV_HEADS, NSA_HD)
        o_diff = diff_attention(
            dq.reshape(B, S, DIFF_HEADS, 2, DIFF_HD), dk.reshape(B, S, DIFF_HEADS, 2, DIFF_HD),
            dv.reshape(B, S, DIFF_HEADS, DIFF_VD), rel_bias[:, :DIFF_HEADS],
            diff_lq1[l], diff_lk1[l], diff_lq2[l], diff_lk2[l], diff_subln_g[l], l)
        o_nsa = nsa_attention(
            nq.reshape(B, S, NSA_HEADS, NSA_HD), kc.reshape(kv_shape), vc.reshape(kv_shape),
            ksel.reshape(kv_shape), vsel.reshape(kv_shape), kwin.reshape(kv_shape),
            vwin.reshape(kv_shape), ng, rel_bias[:, DIFF_HEADS:],
            cmp_pos_k[l], cmp_pos_v[l], cmp_w1_k[l], cmp_w2_k[l], cmp_w1_v[l], cmp_w2_v[l])
        x = x + jnp.concatenate([o_diff, o_nsa], axis=-1) @ w_out[l]
        x = x + peer_ffn(rms_norm(x, norm_ffn_g[l]), peer_w_q[l], peer_sub_keys[l],
                         peer_u[l], peer_v[l])
    return rms_norm(x, final_norm_g)
```

```python
import functools
import math

import numpy as np
import jax
import jax.numpy as jnp
from jax import lax
from jax.experimental import pallas as pl
from jax.experimental.pallas import tpu as pltpu

D_MODEL = 1024
DIFF_HEADS = 4
DIFF_HD = 64
DIFF_VD = 2 * DIFF_HD
DIFF_WIDTH = DIFF_HEADS * DIFF_VD
NSA_HEADS = 8
NSA_KV_HEADS = 2
NSA_GROUP = NSA_HEADS // NSA_KV_HEADS
NSA_HD = 64
NSA_WIDTH = NSA_HEADS * NSA_HD
CMP_LEN = 32
CMP_STRIDE = 16
CMP_HIDDEN = 256
SEL_LEN = 64
SEL_TOPN = 8
WINDOW = 512
FORCE_BONUS = 1000.0
NEG_INF = -1e30
N_BUCKETS = 32
MAX_DISTANCE = 128
PEER_HEADS = 8
PEER_NKEYS = 128
PEER_EXPERTS = PEER_NKEYS ** 2
PEER_DKEY = 256
PEER_TOPK = 16
EPS = 1e-6
LAMBDA_INIT = 0.8 - 0.6 * math.exp(-0.3 * 0)

KV_COLS = NSA_KV_HEADS * NSA_HD
SPLIT_SIZES = (DIFF_HEADS * 2 * DIFF_HD, DIFF_HEADS * 2 * DIFF_HD, DIFF_WIDTH, NSA_WIDTH,
               KV_COLS, KV_COLS, KV_COLS, KV_COLS, KV_COLS, KV_COLS, 3 * NSA_HEADS)
SPLIT_OFF = tuple(int(v) for v in np.concatenate([[0], np.cumsum(SPLIT_SIZES)]))

LANES = 128
VMEM_LIMIT = 56 * 1024 * 1024
HALF_EXPERTS = PEER_EXPERTS // 2
PAIR_ROWS = 16

_f32 = jnp.float32
_bf16 = jnp.bfloat16


def _cparams(sem):
    return pltpu.CompilerParams(dimension_semantics=sem, vmem_limit_bytes=VMEM_LIMIT)


def _dot(a, b):
    return jnp.dot(a, b, preferred_element_type=_f32)


def _dot_nt(a, b):
    return lax.dot_general(a, b, (((1,), (1,)), ((), ())), preferred_element_type=_f32)


def _split_hi_lo(x):
    hi = x.astype(_bf16)
    lo = (x - hi.astype(_f32)).astype(_bf16)
    return hi, lo


def _gelu_tanh(x):
    c = math.sqrt(2.0 / math.pi)
    return 0.5 * x * (1.0 + jnp.tanh(c * (x + 0.044715 * (x * x * x))))


IN_TM = 512
_IN_GROUPS = (("dq", 512, _bf16), ("dk", 512, _bf16), ("dv", 512, _bf16), ("nq", 512, _bf16),
              ("kcvc", 256, _f32), ("ks2", 256, _bf16), ("vs2", 256, _bf16),
              ("kw2", 256, _bf16), ("vw2", 256, _bf16), ("ng", 128, _f32))


def _in_proj_kernel(x_ref, g_ref, w_ref, *out_refs):
    x = x_ref[...]
    y = x * lax.rsqrt(jnp.mean(x * x, axis=-1, keepdims=True) + EPS)
    h = (y * g_ref[...]).astype(_bf16)
    off = 0
    for (_, width, dt), o_ref in zip(_IN_GROUPS, out_refs):
        o_ref[...] = _dot(h, w_ref[:, off:off + width]).astype(dt)
        off += width


def _arrange_w_in(w_in):
    o = SPLIT_OFF
    sl = lambda i: w_in[:, o[i]:o[i + 1]]
    dup = lambda w: jnp.concatenate([w[:, :64], w[:, :64], w[:, 64:], w[:, 64:]], axis=1)
    ng = jnp.pad(sl(10), ((0, 0), (0, LANES - 3 * NSA_HEADS)))
    cols = [sl(0) * (DIFF_HD ** -0.5), sl(1), sl(2), sl(3) * (NSA_HD ** -0.5),
            sl(4), sl(5), dup(sl(6)), dup(sl(7)), dup(sl(8)), dup(sl(9)), ng]
    return jnp.concatenate(cols, axis=1).astype(_bf16)


def _in_proj(x2d, g, w_arr):
    t = x2d.shape[0]
    ncol = w_arr.shape[1]
    out_shape = [jax.ShapeDtypeStruct((t, wd), dt) for _, wd, dt in _IN_GROUPS]
    out_specs = [pl.BlockSpec((IN_TM, wd), lambda i: (i, 0)) for _, wd, _ in _IN_GROUPS]
    return pl.pallas_call(
        _in_proj_kernel,
        grid=(t // IN_TM,),
        in_specs=[pl.BlockSpec((IN_TM, D_MODEL), lambda i: (i, 0)),
                  pl.BlockSpec((1, D_MODEL), lambda i: (0, 0)),
                  pl.BlockSpec((D_MODEL, ncol), lambda i: (0, 0))],
        out_specs=out_specs,
        out_shape=out_shape,
        compiler_params=_cparams(("parallel",)),
        name="in_proj",
    )(x2d, g.reshape(1, D_MODEL), w_arr)


def _bucket_table(n):
    rel = np.arange(n)
    max_exact = N_BUCKETS // 2
    nf = np.maximum(rel, 1).astype(np.float32)
    large = max_exact + (np.log(nf / np.float32(max_exact)) / np.float32(math.log(MAX_DISTANCE / max_exact))
                         * np.float32(N_BUCKETS - max_exact)).astype(np.int32)
    large = np.minimum(large, N_BUCKETS - 1)
    return np.where(rel < max_exact, rel, large).astype(np.int32)


def _bias_tiles(rel_bias_heads, tq, tk, n_tiles, window=None):
    span = n_tiles * tq + tk
    bucket = _bucket_table(span)
    rel = (np.arange(n_tiles)[:, None, None] * tq + np.arange(tq)[None, :, None]
           - np.arange(tk)[None, None, :])
    ok = rel >= 0
    if window is not None:
        ok &= rel < window
    idx = bucket[np.clip(rel, 0, span - 1)]
    tab = rel_bias_heads.astype(_f32) - rel_bias_heads[N_BUCKETS - 1][None, :].astype(_f32)
    tiles = jnp.transpose(tab[idx], (3, 0, 1, 2))
    return jnp.where(jnp.asarray(ok)[None], tiles, NEG_INF)


def _softmax_first(s, v):
    m = jnp.max(s, axis=-1, keepdims=True)
    p = jnp.exp(s - m)
    l = jnp.sum(p, axis=-1, keepdims=True)
    acc = _dot(p.astype(_bf16), v)
    return m, l, acc


def _softmax_step(carry, s, v):
    m, l, acc = carry
    m_new = jnp.maximum(m, jnp.max(s, axis=-1, keepdims=True))
    alpha = jnp.exp(m - m_new)
    p = jnp.exp(s - m_new)
    l = alpha * l + jnp.sum(p, axis=-1, keepdims=True)
    acc = alpha * acc + _dot(p.astype(_bf16), v)
    return m_new, l, acc


def _lane_half_mask(shape):
    return lax.broadcasted_iota(jnp.int32, shape, len(shape) - 1) < (LANES // 2)


DIFF_TQ = 256


def _diff_attn_kernel(q_ref, k_ref, v_ref, bt_ref, lam_ref, g_ref, o_ref):
    tq = DIFF_TQ
    qb = pl.program_id(2)
    q = q_ref[...]
    lo = _lane_half_mask(q.shape)
    zero = jnp.zeros_like(q)
    q2 = jnp.concatenate([jnp.where(lo, q, zero), jnp.where(lo, zero, q)], axis=0)

    def kv(kb):
        start = pl.multiple_of(kb * tq, tq)
        return k_ref[pl.ds(start, tq), :], v_ref[pl.ds(start, tq), :]

    k, v = kv(qb)
    b0 = bt_ref[0]
    carry = _softmax_first(_dot_nt(q2, k) + jnp.concatenate([b0, b0], axis=0), v)
    k, v = kv(jnp.maximum(qb - 1, 0))
    b1 = bt_ref[1] + jnp.where(qb >= 1, 0.0, NEG_INF)
    carry = _softmax_step(carry, _dot_nt(q2, k) + jnp.concatenate([b1, b1], axis=0), v)

    def far(i, c):
        k, v = kv(qb - 2 - i)
        return _softmax_step(c, _dot_nt(q2, k), v)

    m, l, acc = lax.fori_loop(0, jnp.maximum(qb - 1, 0), far, carry)
    o2 = acc / l
    lv = lam_ref[...]
    lam = (jnp.exp(jnp.sum(lv[0:1] * lv[1:2], axis=-1, keepdims=True))
           - jnp.exp(jnp.sum(lv[2:3] * lv[3:4], axis=-1, keepdims=True)) + LAMBDA_INIT)
    o = o2[:tq] - lam * o2[tq:]
    y = o * lax.rsqrt(jnp.mean(o * o, axis=-1, keepdims=True) + EPS)
    o_ref[...] = (y * g_ref[...] * (1.0 - LAMBDA_INIT)).astype(o_ref.dtype)


def _diff_attn(dq, dk, dv, bt, lam_rows, subln_g):
    b, s, _ = dq.shape
    tq = DIFF_TQ
    return pl.pallas_call(
        _diff_attn_kernel,
        grid=(b, DIFF_HEADS, s // tq),
        in_specs=[pl.BlockSpec((None, tq, LANES), lambda bi, h, qb: (bi, qb, h)),
                  pl.BlockSpec((None, s, LANES), lambda bi, h, qb: (bi, 0, h)),
                  pl.BlockSpec((None, s, LANES), lambda bi, h, qb: (bi, 0, h)),
                  pl.BlockSpec((None, 2, tq, tq), lambda bi, h, qb: (h, 0, 0, 0)),
                  pl.BlockSpec((8, LANES), lambda bi, h, qb: (0, 0)),
                  pl.BlockSpec((1, LANES), lambda bi, h, qb: (0, 0))],
        out_specs=pl.BlockSpec((None, tq, LANES), lambda bi, h, qb: (bi, qb, h)),
        out_shape=jax.ShapeDtypeStruct((b, s, DIFF_WIDTH), _bf16),
        compiler_params=_cparams(("parallel", "parallel", "arbitrary")),
        name="diff_attn",
    )(dq, dk, dv, bt, lam_rows, subln_g.reshape(1, LANES))


def _compress_kernel(a_ref, pos_ref, w1_ref, w2_ref, o_ref):
    a = a_ref[...]
    half = (CMP_LEN // 2) * NSA_HD
    za = _dot((a + pos_ref[0]).astype(_bf16), w1_ref[:half, :])
    zb = _dot((a + pos_ref[1]).astype(_bf16), w1_ref[half:, :])
    n = a.shape[0]
    hid = za + pltpu.roll(zb, n - 1, 0)
    o_ref[...] = _dot(_gelu_tanh(hid).astype(_bf16), w2_ref[...]).astype(o_ref.dtype)


def _compress(a, pos, w1, w2dup):
    b, _, _, n, width = a.shape
    return pl.pallas_call(
        _compress_kernel,
        grid=(b, 2, NSA_KV_HEADS),
        in_specs=[pl.BlockSpec((None, None, None, n, width), lambda bi, kv, g: (bi, kv, g, 0, 0)),
                  pl.BlockSpec((None, 2, 1, width), lambda bi, kv, g: (kv, 0, 0, 0)),
                  pl.BlockSpec((None, 2 * width, CMP_HIDDEN), lambda bi, kv, g: (kv, 0, 0)),
                  pl.BlockSpec((None, CMP_HIDDEN, LANES), lambda bi, kv, g: (kv, 0, 0))],
        out_specs=pl.BlockSpec((None, None, None, n, LANES), lambda bi, kv, g: (bi, kv, g, 0, 0)),
        out_shape=jax.ShapeDtypeStruct((b, 2, NSA_KV_HEADS, n, LANES), _bf16),
        compiler_params=_cparams(("parallel", "parallel", "parallel")),
        name="compress",
    )(a, pos, w1, w2dup)


NSA_TQ = 128


def _stack_group_queries(q):
    rows = []
    for j in range(NSA_GROUP):
        blk = q[:, (j // 2) * LANES:(j // 2 + 1) * LANES]
        lo = _lane_half_mask(blk.shape)
        keep = lo if j % 2 == 0 else jnp.logical_not(lo)
        rows.append(jnp.where(keep, blk, jnp.zeros_like(blk)))
    return jnp.concatenate(rows, axis=0)


def _unstack_group_outputs(o4, tq):
    lo = _lane_half_mask((tq, LANES))
    pairs = [jnp.where(lo, o4[(2 * i) * tq:(2 * i + 1) * tq], o4[(2 * i + 1) * tq:(2 * i + 2) * tq])
             for i in range(NSA_GROUP // 2)]
    return jnp.concatenate(pairs, axis=1)


def _cmp_attn_kernel(q_ref, kc_ref, vc_ref, cs_ref, o_ref, sel_ref, *, n_sel):
    tq = NSA_TQ
    qb = pl.program_id(2)
    q4 = _stack_group_queries(q_ref[...])
    s = _dot_nt(q4, kc_ref[...])
    row = lax.broadcasted_iota(jnp.int32, s.shape, 0)
    pos4 = qb * tq + (row & (tq - 1))
    c = lax.broadcasted_iota(jnp.int32, s.shape, 1)
    ok = (c * CMP_STRIDE + (CMP_LEN - 1)) <= pos4
    s = jnp.where(ok, s, NEG_INF)
    m = jnp.max(s, axis=-1, keepdims=True)
    e = jnp.where(ok, jnp.exp(s - m), 0.0)
    l = jnp.sum(e, axis=-1, keepdims=True)
    p = e / jnp.where(l > 0.0, l, 1.0)
    o_ref[...] = _unstack_group_outputs(_dot(p.astype(_bf16), vc_ref[...]), tq).astype(o_ref.dtype)

    psum = p[0:tq] + p[tq:2 * tq] + p[2 * tq:3 * tq] + p[3 * tq:4 * tq]
    hi, lo = _split_hi_lo(psum)
    imp = _dot(hi, cs_ref[...]) + _dot(lo, cs_ref[...])
    blk = lax.broadcasted_iota(jnp.int32, imp.shape, 1)
    pos = qb * tq + lax.broadcasted_iota(jnp.int32, imp.shape, 0)
    cur = pos // SEL_LEN
    forced = (blk == 0) | (blk == cur) | (blk == cur - 1)
    score = jnp.where(blk <= cur, imp + jnp.where(forced, FORCE_BONUS, 0.0), NEG_INF)
    rank = jnp.zeros(score.shape, jnp.int32)
    for i in range(n_sel):
        si = jnp.max(jnp.where(blk == i, score, -jnp.inf), axis=-1, keepdims=True)
        beats = (si > score) | ((si == score) & (i < blk))
        rank = rank + beats.astype(jnp.int32)
    sel = (rank < SEL_TOPN) & (score > 0.5 * NEG_INF)
    sel_ref[...] = sel.astype(sel_ref.dtype)


def _cmp_attn(nq, kcmp, vcmp, cmp_sel, n_sel):
    b, s, _ = nq.shape
    tq = NSA_TQ
    n = kcmp.shape[-2]
    gw = NSA_GROUP * NSA_HD
    return pl.pallas_call(
        functools.partial(_cmp_attn_kernel, n_sel=n_sel),
        grid=(b, NSA_KV_HEADS, s // tq),
        in_specs=[pl.BlockSpec((None, tq, gw), lambda bi, g, qb: (bi, qb, g)),
                  pl.BlockSpec((None, None, n, LANES), lambda bi, g, qb: (bi, g, 0, 0)),
                  pl.BlockSpec((None, None, n, LANES), lambda bi, g, qb: (bi, g, 0, 0)),
                  pl.BlockSpec((n, LANES), lambda bi, g, qb: (0, 0))],
        out_specs=[pl.BlockSpec((None, tq, gw), lambda bi, g, qb: (bi, qb, g)),
                   pl.BlockSpec((None, None, tq, LANES), lambda bi, g, qb: (bi, g, qb, 0))],
        out_shape=[jax.ShapeDtypeStruct((b, s, NSA_WIDTH), _bf16),
                   jax.ShapeDtypeStruct((b, NSA_KV_HEADS, s, LANES), _bf16)],
        compiler_params=_cparams(("parallel", "parallel", "arbitrary")),
        name="cmp_attn",
    )(nq, kcmp, vcmp, cmp_sel)


def _sel_attn_kernel(q_ref, k_ref, v_ref, sel_ref, ex_ref, bt_ref, o_ref):
    tq = NSA_TQ
    qb = pl.program_id(2)
    q4 = _stack_group_queries(q_ref[...])
    sel = sel_ref[...]

    def block(kb):
        start = pl.multiple_of(kb * tq, tq)
        k = k_ref[pl.ds(start, tq), :]
        v = v_ref[pl.ds(start, tq), :]
        chosen = _dot(sel, ex_ref[:, pl.ds(start, tq)])
        pen = (chosen - 1.0) * (-NEG_INF)
        return _dot_nt(q4, k) + jnp.concatenate([pen] * NSA_GROUP, axis=0), v

    s, v = block(qb)
    carry = _softmax_first(s + bt_ref[0], v)
    s, v = block(jnp.maximum(qb - 1, 0))
    carry = _softmax_step(carry, s + bt_ref[1] + jnp.where(qb >= 1, 0.0, NEG_INF), v)

    def far(i, c):
        s, v = block(qb - 2 - i)
        return _softmax_step(c, s, v)

    m, l, acc = lax.fori_loop(0, jnp.maximum(qb - 1, 0), far, carry)
    o_ref[...] = _unstack_group_outputs(acc / l, tq).astype(o_ref.dtype)


def _sel_attn(nq, ks2, vs2, sel, expand, bt):
    b, s, _ = nq.shape
    tq = NSA_TQ
    gw = NSA_GROUP * NSA_HD
    return pl.pallas_call(
        _sel_attn_kernel,
        grid=(b, NSA_KV_HEADS, s // tq),
        in_specs=[pl.BlockSpec((None, tq, gw), lambda bi, g, qb: (bi, qb, g)),
                  pl.BlockSpec((None, s, LANES), lambda bi, g, qb: (bi, 0, g)),
                  pl.BlockSpec((None, s, LANES), lambda bi, g, qb: (bi, 0, g)),
                  pl.BlockSpec((None, None, tq, LANES), lambda bi, g, qb: (bi, g, qb, 0)),
                  pl.BlockSpec((LANES, s), lambda bi, g, qb: (0, 0)),
                  pl.BlockSpec((None, 2, NSA_GROUP * tq, tq), lambda bi, g, qb: (g, 0, 0, 0))],
        out_specs=pl.BlockSpec((None, tq, gw), lambda bi, g, qb: (bi, qb, g)),
        out_shape=jax.ShapeDtypeStruct((b, s, NSA_WIDTH), _bf16),
        compiler_params=_cparams(("parallel", "parallel", "arbitrary")),
        name="sel_attn",
    )(nq, ks2, vs2, sel, expand, bt)


WIN_TILES = WINDOW // NSA_TQ + 1


def _win_attn_kernel(q_ref, k_ref, v_ref, bt_ref, o_ref):
    tq = NSA_TQ
    qb = pl.program_id(2)
    q4 = _stack_group_queries(q_ref[...])
    carry = None
    for d in range(WIN_TILES):
        kb = jnp.maximum(qb - d, 0)
        start = pl.multiple_of(kb * tq, tq)
        k = k_ref[pl.ds(start, tq), :]
        v = v_ref[pl.ds(start, tq), :]
        s = _dot_nt(q4, k) + bt_ref[d]
        if d == 0:
            carry = _softmax_first(s, v)
        else:
            carry = _softmax_step(carry, s + jnp.where(qb >= d, 0.0, NEG_INF), v)
    m, l, acc = carry
    o_ref[...] = _unstack_group_outputs(acc / l, tq).astype(o_ref.dtype)


def _win_attn(nq, kw2, vw2, bt):
    b, s, _ = nq.shape
    tq = NSA_TQ
    gw = NSA_GROUP * NSA_HD
    return pl.pallas_call(
        _win_attn_kernel,
        grid=(b, NSA_KV_HEADS, s // tq),
        in_specs=[pl.BlockSpec((None, tq, gw), lambda bi, g, qb: (bi, qb, g)),
                  pl.BlockSpec((None, s, LANES), lambda bi, g, qb: (bi, 0, g)),
                  pl.BlockSpec((None, s, LANES), lambda bi, g, qb: (bi, 0, g)),
                  pl.BlockSpec((None, WIN_TILES, NSA_GROUP * tq, tq), lambda bi, g, qb: (g, 0, 0, 0))],
        out_specs=pl.BlockSpec((None, tq, gw), lambda bi, g, qb: (bi, qb, g)),
        out_shape=jax.ShapeDtypeStruct((b, s, NSA_WIDTH), _bf16),
        compiler_params=_cparams(("parallel", "parallel", "arbitrary")),
        name="win_attn",
    )(nq, kw2, vw2, bt)


MIX_TM = 256


def _mix_out_kernel(x_ref, od_ref, oc_ref, os_ref, ow_ref, ng_ref, ge_ref, wo_ref, g_ref, wq_ref,
                    x1_ref, xn_ref, qp_ref):
    sig = jax.nn.sigmoid(ng_ref[...])
    hi, lo = _split_hi_lo(sig)
    o_nsa = jnp.zeros(oc_ref.shape, _f32)
    for br, o_ref in enumerate((oc_ref, os_ref, ow_ref)):
        gate = _dot(hi, ge_ref[br]) + _dot(lo, ge_ref[br])
        o_nsa = o_nsa + gate * o_ref[...].astype(_f32)
    y = _dot(od_ref[...], wo_ref[:DIFF_WIDTH, :]) + _dot(o_nsa.astype(_bf16), wo_ref[DIFF_WIDTH:, :])
    x1 = x_ref[...] + y
    x1_ref[...] = x1
    xn = x1 * lax.rsqrt(jnp.mean(x1 * x1, axis=-1, keepdims=True) + EPS) * g_ref[...]
    xn = xn.astype(_bf16)
    xn_ref[...] = xn
    qp_ref[...] = _dot(xn, wq_ref[...]).astype(qp_ref.dtype)


def _mix_out(x2d, o_diff, o_cmp, o_sel, o_win, ng, gate_expand, w_out, g_ffn, w_q):
    t = x2d.shape[0]
    tm = MIX_TM
    nq = w_q.shape[1]
    row = lambda w: pl.BlockSpec((tm, w), lambda i: (i, 0))
    full = lambda a: pl.BlockSpec(a.shape, lambda i: (0,) * a.ndim)
    g2 = g_ffn.reshape(1, D_MODEL)
    return pl.pallas_call(
        _mix_out_kernel,
        grid=(t // tm,),
        in_specs=[row(D_MODEL), row(DIFF_WIDTH), row(NSA_WIDTH), row(NSA_WIDTH), row(NSA_WIDTH),
                  row(LANES), full(gate_expand), full(w_out), full(g2), full(w_q)],
        out_specs=[row(D_MODEL), row(D_MODEL), row(nq)],
        out_shape=[jax.ShapeDtypeStruct((t, D_MODEL), _f32),
                   jax.ShapeDtypeStruct((t, D_MODEL), _bf16),
                   jax.ShapeDtypeStruct((t, nq), _bf16)],
        compiler_params=_cparams(("parallel",)),
        name="mix_out",
    )(x2d, o_diff, o_cmp, o_sel, o_win, ng, gate_expand, w_out, g2, w_q)


ROUTE_TM = 256
SLOTS = 2 * PEER_HEADS * PEER_TOPK


def _top16_rows(s, rows):
    vals, ids = [], []
    n = s.shape[0]
    for _ in range(PEER_TOPK):
        m = jnp.max(s, axis=0, keepdims=True)
        idx = jnp.min(jnp.where(s == m, rows, n), axis=0, keepdims=True)
        vals.append(m)
        ids.append(idx)
        s = jnp.where(rows == idx, -jnp.inf, s)
    return jnp.concatenate(vals, axis=0), jnp.concatenate(ids, axis=0)


def _peer_route_kernel(q_ref, sk_ref, row_ref, gate_ref, rowt_ref, gatet_ref):
    tm = ROUTE_TM
    rows_k = lax.broadcasted_iota(jnp.int32, (PEER_NKEYS, tm), 0)
    rows_c = lax.broadcasted_iota(jnp.int32, (PEER_TOPK * PEER_TOPK, tm), 0)
    half_d = PEER_DKEY // 2

    def head(h, _):
        col = pl.multiple_of(h * PEER_DKEY, PEER_DKEY)
        q0 = q_ref[:, pl.ds(col, half_d)]
        q1 = q_ref[:, pl.ds(col + half_d, half_d)]
        sv0, si0 = _top16_rows(_dot_nt(sk_ref[2 * h], q0), rows_k)
        sv1, si1 = _top16_rows(_dot_nt(sk_ref[2 * h + 1], q1), rows_k)
        cand = jnp.concatenate([sv0[a:a + 1] + sv1 for a in range(PEER_TOPK)], axis=0)
        cidx = jnp.concatenate([si0[a:a + 1] * PEER_NKEYS + si1 for a in range(PEER_TOPK)], axis=0)
        cv, ci = _top16_rows(cand, rows_c)
        e = jnp.concatenate(
            [jnp.sum(jnp.where(rows_c == ci[k:k + 1], cidx, 0), axis=0, keepdims=True)
             for k in range(PEER_TOPK)], axis=0)
        ex = jnp.exp(cv - cv[0:1])
        gate = ex / jnp.sum(ex, axis=0, keepdims=True)
        upper = e >= HALF_EXPERTS
        base = pl.multiple_of(h * PEER_TOPK, PEER_TOPK)
        rowt_ref[pl.ds(base, PEER_TOPK), :] = (e & (HALF_EXPERTS - 1)) * PAIR_ROWS
        base2 = pl.multiple_of(h * 2 * PEER_TOPK, 2 * PEER_TOPK)
        gatet_ref[pl.ds(base2, PEER_TOPK), :] = jnp.where(upper, 0.0, gate)
        gatet_ref[pl.ds(base2 + PEER_TOPK, PEER_TOPK), :] = jnp.where(upper, gate, 0.0)
        return 0

    lax.fori_loop(0, PEER_HEADS, head, 0)
    row_ref[...] = rowt_ref[...].T
    gate_ref[...] = gatet_ref[...].T


def _peer_route(qp, sk):
    t = qp.shape[0]
    tm = ROUTE_TM
    npick = PEER_HEADS * PEER_TOPK
    return pl.pallas_call(
        _peer_route_kernel,
        grid=(t // tm,),
        in_specs=[pl.BlockSpec((tm, qp.shape[1]), lambda i: (i, 0)),
                  pl.BlockSpec(sk.shape, lambda i: (0, 0, 0))],
        out_specs=[pl.BlockSpec((tm, npick), lambda i: (i, 0)),
                   pl.BlockSpec((tm, SLOTS), lambda i: (i, 0))],
        out_shape=[jax.ShapeDtypeStruct((t, npick), jnp.int32),
                   jax.ShapeDtypeStruct((t, SLOTS), _f32)],
        scratch_shapes=[pltpu.VMEM((npick, tm), jnp.int32), pltpu.VMEM((SLOTS, tm), _f32)],
        compiler_params=_cparams(("parallel",)),
        name="peer_route",
    )(qp, sk)


PEER_TT = 64
NPICK = PEER_HEADS * PEER_TOPK
CHUNK_ROWS = PEER_TOPK * PAIR_ROWS


def _pair_table(w):
    e = w.shape[0]
    w3 = w.astype(_bf16).reshape(2, e // 2, 8, LANES)
    return jnp.transpose(w3, (1, 0, 2, 3)).reshape(e // 2 * PAIR_ROWS, LANES)


def _peer_up_kernel(row_ref, x_ref, tab_ref, bsum_ref, h_ref, stage_ref):
    ones = jnp.ones((8, LANES), _bf16)

    def token(t, _):
        x2 = x_ref[t]
        for p in range(NPICK):
            r = pl.multiple_of(row_ref[t * NPICK + p], PAIR_ROWS)
            stage_ref[p * PAIR_ROWS:(p + 1) * PAIR_ROWS, :] = tab_ref[pl.ds(r, PAIR_ROWS), :] * x2
        parts = [_dot(bsum_ref[...], stage_ref[c * CHUNK_ROWS:(c + 1) * CHUNK_ROWS, :])
                 for c in range(PEER_HEADS)]
        part = jnp.concatenate(parts, axis=0)
        hi, lo = _split_hi_lo(part)
        hrow = _dot_nt(ones, hi) + _dot_nt(ones, lo)
        h_ref[pl.ds(t, 1), :] = hrow[0:1]
        return 0

    lax.fori_loop(0, PEER_TT, token, 0)


def _peer_up(rows_flat, x16, tab, bsum):
    t = x16.shape[0]
    tt = PEER_TT
    return pl.pallas_call(
        _peer_up_kernel,
        grid=(t // tt,),
        in_specs=[pl.BlockSpec((tt * NPICK,), lambda i: (i,), memory_space=pltpu.SMEM),
                  pl.BlockSpec((tt, PAIR_ROWS, LANES), lambda i: (i, 0, 0)),
                  pl.BlockSpec(tab.shape, lambda i: (0, 0), pipeline_mode=pl.Buffered(1)),
                  pl.BlockSpec(bsum.shape, lambda i: (0, 0))],
        out_specs=pl.BlockSpec((tt, SLOTS), lambda i: (i, 0)),
        out_shape=jax.ShapeDtypeStruct((t, SLOTS), _f32),
        scratch_shapes=[pltpu.VMEM((NPICK * PAIR_ROWS, LANES), _bf16)],
        compiler_params=_cparams(("arbitrary",)),
        name="peer_up",
    )(rows_flat, x16, tab, bsum)


def _peer_down_kernel(row_ref, h_ref, gate_ref, tab_ref, ex_ref, diag_ref, y_ref, wexp_ref, stage_ref):
    w = _gelu_tanh(h_ref[...]) * gate_ref[...]
    hi, lo = _split_hi_lo(w)
    wexp_ref[...] = _dot(hi, ex_ref[...]) + _dot(lo, ex_ref[...])

    def token(t, _):
        for p in range(NPICK):
            r = pl.multiple_of(row_ref[t * NPICK + p], PAIR_ROWS)
            stage_ref[p * PAIR_ROWS:(p + 1) * PAIR_ROWS, :] = tab_ref[pl.ds(r, PAIR_ROWS), :]
        a = jnp.broadcast_to(wexp_ref[pl.ds(t, 1), :], diag_ref.shape) * diag_ref[...]
        ahi = a.astype(_bf16).astype(_f32)
        a2 = jnp.concatenate([ahi, a - ahi], axis=0).astype(_bf16)
        y2 = _dot(a2, stage_ref[...])
        y_ref[t] = y2[:8] + y2[8:]
        return 0

    lax.fori_loop(0, PEER_TT, token, 0)


def _peer_down(rows_flat, h2, gate2, tab, expand, diag):
    t = h2.shape[0]
    tt = PEER_TT
    return pl.pallas_call(
        _peer_down_kernel,
        grid=(t // tt,),
        in_specs=[pl.BlockSpec((tt * NPICK,), lambda i: (i,), memory_space=pltpu.SMEM),
                  pl.BlockSpec((tt, SLOTS), lambda i: (i, 0)),
                  pl.BlockSpec((tt, SLOTS), lambda i: (i, 0)),
                  pl.BlockSpec(tab.shape, lambda i: (0, 0), pipeline_mode=pl.Buffered(1)),
                  pl.BlockSpec(expand.shape, lambda i: (0, 0)),
                  pl.BlockSpec(diag.shape, lambda i: (0, 0))],
        out_specs=pl.BlockSpec((tt, 8, LANES), lambda i: (i, 0, 0)),
        out_shape=jax.ShapeDtypeStruct((t, 8, LANES), _f32),
        scratch_shapes=[pltpu.VMEM((tt, NPICK * PAIR_ROWS), _f32),
                        pltpu.VMEM((NPICK * PAIR_ROWS, LANES), _bf16)],
        compiler_params=_cparams(("arbitrary",)),
        name="peer_down",
    )(rows_flat, h2, gate2, tab, expand, diag)


def _peer_constants():
    bsum = np.zeros((2 * PEER_TOPK, CHUNK_ROWS), np.float32)
    for k in range(PEER_TOPK):
        for half in range(2):
            bsum[half * PEER_TOPK + k, k * PAIR_ROWS + half * 8:k * PAIR_ROWS + half * 8 + 8] = 1.0
    expand = np.zeros((SLOTS, NPICK * PAIR_ROWS), np.float32)
    for h in range(PEER_HEADS):
        for half in range(2):
            for k in range(PEER_TOPK):
                slot = h * 2 * PEER_TOPK + half * PEER_TOPK + k
                c0 = (h * PEER_TOPK + k) * PAIR_ROWS + half * 8
                expand[slot, c0:c0 + 8] = 1.0
    col = np.arange(NPICK * PAIR_ROWS)
    diag = (col[None, :] % 8 == np.arange(8)[:, None]).astype(np.float32)
    return jnp.asarray(bsum, _bf16), jnp.asarray(expand, _bf16), jnp.asarray(diag, _f32)


FIN_TM = 512


def _final_norm_kernel(x_ref, y_ref, g_ref, o_ref):
    z = x_ref[...] + y_ref[...]
    o_ref[...] = z * lax.rsqrt(jnp.mean(z * z, axis=-1, keepdims=True) + EPS) * g_ref[...]


def _final_norm(x1, y, g):
    t = x1.shape[0]
    row = pl.BlockSpec((FIN_TM, D_MODEL), lambda i: (i, 0))
    return pl.pallas_call(
        _final_norm_kernel,
        grid=(t // FIN_TM,),
        in_specs=[row, row, pl.BlockSpec((1, D_MODEL), lambda i: (0, 0))],
        out_specs=row,
        out_shape=jax.ShapeDtypeStruct((t, D_MODEL), _f32),
        compiler_params=_cparams(("parallel",)),
        name="final_norm",
    )(x1, y, g.reshape(1, D_MODEL))


def _cmp_to_sel_counts(n_rows, n_sel):
    r = SEL_LEN // CMP_STRIDE
    c = CMP_LEN // CMP_STRIDE
    off = np.arange(n_rows)[:, None] - r * np.arange(n_sel)[None, :] + (c - 1)
    counts = np.zeros((n_rows, LANES), np.float32)
    for n in range(c):
        counts[:, :n_sel] += ((off - n >= 0) & (off - n < r)).astype(np.float32)
    counts[n_rows - 1] = 0.0
    return counts


def _attention(x2d, b, s, w_in, norm_mix_g, rel_bias, lam_rows, subln_g,
               cmp_pos_k, cmp_pos_v, cmp_w1_k, cmp_w2_k, cmp_w1_v, cmp_w2_v):
    dq, dk, dv, nq, kcvc, ks2, vs2, kw2, vw2, ng = _in_proj(x2d, norm_mix_g, _arrange_w_in(w_in))
    to3 = lambda a: a.reshape(b, s, a.shape[-1])

    bt_diff = _bias_tiles(rel_bias[:, :DIFF_HEADS], DIFF_TQ, DIFF_TQ, 2)
    o_diff = _diff_attn(to3(dq), to3(dk), to3(dv), bt_diff, lam_rows, subln_g)

    n = s // CMP_STRIDE
    a = kcvc.reshape(b, n, CMP_STRIDE, 2, NSA_KV_HEADS, NSA_HD)
    a = jnp.transpose(a, (0, 3, 4, 1, 2, 5)).reshape(b, 2, NSA_KV_HEADS, n, CMP_STRIDE * NSA_HD)
    pos = jnp.stack([cmp_pos_k, cmp_pos_v]).reshape(2, 2, 1, CMP_STRIDE * NSA_HD)
    w1 = jnp.stack([cmp_w1_k, cmp_w1_v]).astype(_bf16)
    w2 = jnp.stack([cmp_w2_k, cmp_w2_v])
    w2dup = jnp.concatenate([w2, w2], axis=-1).astype(_bf16)
    cmp_kv = _compress(a, pos, w1, w2dup)

    n_sel = s // SEL_LEN
    counts = jnp.asarray(_cmp_to_sel_counts(n, n_sel), _bf16)
    nq3 = to3(nq)
    o_cmp, sel = _cmp_attn(nq3, cmp_kv[:, 0], cmp_kv[:, 1], counts, n_sel)

    nsa_bias = rel_bias[:, DIFF_HEADS:]
    tq = NSA_TQ
    bt_sel = _bias_tiles(nsa_bias, tq, tq, 2).reshape(NSA_KV_HEADS, NSA_GROUP, 2, tq, tq)
    bt_sel = jnp.transpose(bt_sel, (0, 2, 1, 3, 4)).reshape(NSA_KV_HEADS, 2, NSA_GROUP * tq, tq)
    expand = jnp.asarray(np.arange(s)[None, :] // SEL_LEN == np.arange(LANES)[:, None], _bf16)
    o_sel = _sel_attn(nq3, to3(ks2), to3(vs2), sel, expand, bt_sel)

    bt_win = _bias_tiles(nsa_bias, tq, tq, WIN_TILES, window=WINDOW)
    bt_win = bt_win.reshape(NSA_KV_HEADS, NSA_GROUP, WIN_TILES, tq, tq)
    bt_win = jnp.transpose(bt_win, (0, 2, 1, 3, 4)).reshape(NSA_KV_HEADS, WIN_TILES, NSA_GROUP * tq, tq)
    o_win = _win_attn(nq3, to3(kw2), to3(vw2), bt_win)

    flat = lambda a3: a3.reshape(b * s, a3.shape[-1])
    return flat(o_diff), flat(o_cmp), flat(o_sel), flat(o_win), ng


def _gate_expand():
    ge = np.zeros((3, LANES, NSA_WIDTH), np.float32)
    for head in range(NSA_HEADS):
        for br in range(3):
            ge[br, 3 * head + br, head * NSA_HD:(head + 1) * NSA_HD] = 1.0
    return jnp.asarray(ge, _bf16)


def _peer(x1, xn, qp, peer_sub_keys, peer_u, peer_v):
    t = x1.shape[0]
    sk = peer_sub_keys.reshape(2 * PEER_HEADS, PEER_NKEYS, PEER_DKEY // 2).astype(_bf16)
    rows, gate2 = _peer_route(qp, sk)
    rows_flat = rows.reshape(t * NPICK)
    x8 = xn.reshape(t, 8, LANES)
    x16 = jnp.concatenate([x8, x8], axis=1)
    bsum, expand, diag = _peer_constants()
    h2 = _peer_up(rows_flat, x16, _pair_table(peer_u), bsum)
    y = _peer_down(rows_flat, h2, gate2, _pair_table(peer_v), expand, diag)
    return y.reshape(t, D_MODEL)


def kernel(x, w_in, w_out, norm_mix_g, norm_ffn_g, final_norm_g, rel_bias, diff_lq1, diff_lk1, diff_lq2, diff_lk2, diff_subln_g, cmp_pos_k, cmp_pos_v, cmp_w1_k, cmp_w2_k, cmp_w1_v, cmp_w2_v, peer_w_q, peer_sub_keys, peer_u, peer_v):
    b, s, d = x.shape
    x2d = x.reshape(b * s, d)
    lam_rows = jnp.zeros((8, LANES), _f32).at[0:4, :DIFF_HD].set(
        jnp.stack([diff_lq1[0], diff_lk1[0], diff_lq2[0], diff_lk2[0]]))
    o_diff, o_cmp, o_sel, o_win, ng = _attention(
        x2d, b, s, w_in[0], norm_mix_g[0], rel_bias, lam_rows, diff_subln_g[0],
        cmp_pos_k[0], cmp_pos_v[0], cmp_w1_k[0], cmp_w2_k[0], cmp_w1_v[0], cmp_w2_v[0])
    x1, xn, qp = _mix_out(x2d, o_diff, o_cmp, o_sel, o_win, ng, _gate_expand(),
                          w_out[0].astype(_bf16), norm_ffn_g[0], peer_w_q[0].astype(_bf16))
    y = _peer(x1, xn, qp, peer_sub_keys[0], peer_u[0], peer_v[0])
    return _final_norm(x1, y, final_norm_g).reshape(b, s, d)
```

```python
import functools
import math

import numpy as np
import jax
import jax.numpy as jnp
from jax import lax
from jax.experimental import pallas as pl
from jax.experimental.pallas import tpu as pltpu

D_MODEL = 1024
DIFF_HEADS = 4
DIFF_HD = 64
DIFF_VD = 2 * DIFF_HD
DIFF_WIDTH = DIFF_HEADS * DIFF_VD
NSA_HEADS = 8
NSA_KV_HEADS = 2
NSA_GROUP = NSA_HEADS // NSA_KV_HEADS
NSA_HD = 64
NSA_WIDTH = NSA_HEADS * NSA_HD
CMP_LEN = 32
CMP_STRIDE = 16
CMP_HIDDEN = 256
SEL_LEN = 64
SEL_TOPN = 8
WINDOW = 512
FORCE_BONUS = 1000.0
NEG_INF = -1e30
N_BUCKETS = 32
MAX_DISTANCE = 128
PEER_HEADS = 8
PEER_NKEYS = 128
PEER_EXPERTS = PEER_NKEYS ** 2
PEER_DKEY = 256
PEER_TOPK = 16
EPS = 1e-6
LAMBDA_INIT = 0.8 - 0.6 * math.exp(-0.3 * 0)

KV_COLS = NSA_KV_HEADS * NSA_HD
SPLIT_SIZES = (DIFF_HEADS * 2 * DIFF_HD, DIFF_HEADS * 2 * DIFF_HD, DIFF_WIDTH, NSA_WIDTH,
               KV_COLS, KV_COLS, KV_COLS, KV_COLS, KV_COLS, KV_COLS, 3 * NSA_HEADS)
SPLIT_OFF = tuple(int(v) for v in np.concatenate([[0], np.cumsum(SPLIT_SIZES)]))

LANES = 128
VMEM_LIMIT = 56 * 1024 * 1024
HALF_EXPERTS = PEER_EXPERTS // 2
PAIR_ROWS = 16
PAIR_WORD_ROWS = PAIR_ROWS // 2

_f32 = jnp.float32
_bf16 = jnp.bfloat16


def _cparams(sem):
    return pltpu.CompilerParams(dimension_semantics=sem, vmem_limit_bytes=VMEM_LIMIT)


def _dot(a, b):
    return jnp.dot(a, b, preferred_element_type=_f32)


def _dot_nt(a, b):
    return lax.dot_general(a, b, (((1,), (1,)), ((), ())), preferred_element_type=_f32)


def _split_hi_lo(x):
    hi = x.astype(_bf16)
    lo = (x - hi.astype(_f32)).astype(_bf16)
    return hi, lo


def _gelu_tanh(x):
    c = math.sqrt(2.0 / math.pi)
    return 0.5 * x * (1.0 + jnp.tanh(c * (x + 0.044715 * (x * x * x))))


IN_TM = 512
_IN_GROUPS = (("dq", 512, _bf16), ("dk", 512, _bf16), ("dv", 512, _bf16), ("nq", 512, _bf16),
              ("kcvc", 256, _f32), ("ks2", 256, _bf16), ("vs2", 256, _bf16),
              ("kw2", 256, _bf16), ("vw2", 256, _bf16), ("ng", 128, _f32))


def _in_proj_kernel(x_ref, g_ref, w_ref, *out_refs):
    x = x_ref[...]
    y = x * lax.rsqrt(jnp.mean(x * x, axis=-1, keepdims=True) + EPS)
    h = (y * g_ref[...]).astype(_bf16)
    off = 0
    for (_, width, dt), o_ref in zip(_IN_GROUPS, out_refs):
        o_ref[...] = _dot(h, w_ref[:, off:off + width]).astype(dt)
        off += width


def _arrange_w_in(w_in):
    o = SPLIT_OFF
    sl = lambda i: w_in[:, o[i]:o[i + 1]]
    dup = lambda w: jnp.concatenate([w[:, :64], w[:, :64], w[:, 64:], w[:, 64:]], axis=1)
    ng = jnp.pad(sl(10), ((0, 0), (0, LANES - 3 * NSA_HEADS)))
    cols = [sl(0) * (DIFF_HD ** -0.5), sl(1), sl(2), sl(3) * (NSA_HD ** -0.5),
            sl(4), sl(5), dup(sl(6)), dup(sl(7)), dup(sl(8)), dup(sl(9)), ng]
    return jnp.concatenate(cols, axis=1).astype(_bf16)


def _in_proj(x2d, g, w_arr):
    t = x2d.shape[0]
    ncol = w_arr.shape[1]
    out_shape = [jax.ShapeDtypeStruct((t, wd), dt) for _, wd, dt in _IN_GROUPS]
    out_specs = [pl.BlockSpec((IN_TM, wd), lambda i: (i, 0)) for _, wd, _ in _IN_GROUPS]
    return pl.pallas_call(
        _in_proj_kernel,
        grid=(t // IN_TM,),
        in_specs=[pl.BlockSpec((IN_TM, D_MODEL), lambda i: (i, 0)),
                  pl.BlockSpec((1, D_MODEL), lambda i: (0, 0)),
                  pl.BlockSpec((D_MODEL, ncol), lambda i: (0, 0))],
        out_specs=out_specs,
        out_shape=out_shape,
        compiler_params=_cparams(("parallel",)),
        name="in_proj",
    )(x2d, g.reshape(1, D_MODEL), w_arr)


def _bucket_table(n):
    rel = np.arange(n)
    max_exact = N_BUCKETS // 2
    nf = np.maximum(rel, 1).astype(np.float32)
    large = max_exact + (np.log(nf / np.float32(max_exact)) / np.float32(math.log(MAX_DISTANCE / max_exact))
                         * np.float32(N_BUCKETS - max_exact)).astype(np.int32)
    large = np.minimum(large, N_BUCKETS - 1)
    return np.where(rel < max_exact, rel, large).astype(np.int32)


def _bias_tiles(rel_bias_heads, tq, tk, n_tiles, window=None):
    span = n_tiles * tq + tk
    bucket = _bucket_table(span)
    rel = (np.arange(n_tiles)[:, None, None] * tq + np.arange(tq)[None, :, None]
           - np.arange(tk)[None, None, :])
    ok = rel >= 0
    if window is not None:
        ok &= rel < window
    idx = bucket[np.clip(rel, 0, span - 1)]
    tab = rel_bias_heads.astype(_f32) - rel_bias_heads[N_BUCKETS - 1][None, :].astype(_f32)
    tiles = jnp.transpose(tab[idx], (3, 0, 1, 2))
    return jnp.where(jnp.asarray(ok)[None], tiles, NEG_INF)


def _softmax_first(s, v):
    m = jnp.max(s, axis=-1, keepdims=True)
    p = jnp.exp(s - m)
    l = jnp.sum(p, axis=-1, keepdims=True)
    acc = _dot(p.astype(_bf16), v)
    return m, l, acc


def _softmax_step(carry, s, v):
    m, l, acc = carry
    m_new = jnp.maximum(m, jnp.max(s, axis=-1, keepdims=True))
    alpha = jnp.exp(m - m_new)
    p = jnp.exp(s - m_new)
    l = alpha * l + jnp.sum(p, axis=-1, keepdims=True)
    acc = alpha * acc + _dot(p.astype(_bf16), v)
    return m_new, l, acc


def _lane_half_mask(shape):
    return lax.broadcasted_iota(jnp.int32, shape, len(shape) - 1) < (LANES // 2)


DIFF_TQ = 256


def _diff_attn_kernel(q_ref, k_ref, v_ref, bt_ref, lam_ref, g_ref, o_ref):
    tq = DIFF_TQ
    qb = pl.program_id(2)
    q = q_ref[...]
    lo = _lane_half_mask(q.shape)
    zero = jnp.zeros_like(q)
    q2 = jnp.concatenate([jnp.where(lo, q, zero), jnp.where(lo, zero, q)], axis=0)

    def kv(kb):
        start = pl.multiple_of(kb * tq, tq)
        return k_ref[pl.ds(start, tq), :], v_ref[pl.ds(start, tq), :]

    k, v = kv(qb)
    b0 = bt_ref[0]
    carry = _softmax_first(_dot_nt(q2, k) + jnp.concatenate([b0, b0], axis=0), v)
    k, v = kv(jnp.maximum(qb - 1, 0))
    b1 = bt_ref[1] + jnp.where(qb >= 1, 0.0, NEG_INF)
    carry = _softmax_step(carry, _dot_nt(q2, k) + jnp.concatenate([b1, b1], axis=0), v)

    def far(i, c):
        k, v = kv(qb - 2 - i)
        return _softmax_step(c, _dot_nt(q2, k), v)

    m, l, acc = lax.fori_loop(0, jnp.maximum(qb - 1, 0), far, carry)
    o2 = acc / l
    lv = lam_ref[...]
    lam = (jnp.exp(jnp.sum(lv[0:1] * lv[1:2], axis=-1, keepdims=True))
           - jnp.exp(jnp.sum(lv[2:3] * lv[3:4], axis=-1, keepdims=True)) + LAMBDA_INIT)
    o = o2[:tq] - lam * o2[tq:]
    y = o * lax.rsqrt(jnp.mean(o * o, axis=-1, keepdims=True) + EPS)
    o_ref[...] = (y * g_ref[...] * (1.0 - LAMBDA_INIT)).astype(o_ref.dtype)


def _diff_attn(dq, dk, dv, bt, lam_rows, subln_g):
    b, s, _ = dq.shape
    tq = DIFF_TQ
    return pl.pallas_call(
        _diff_attn_kernel,
        grid=(b, DIFF_HEADS, s // tq),
        in_specs=[pl.BlockSpec((None, tq, LANES), lambda bi, h, qb: (bi, qb, h)),
                  pl.BlockSpec((None, s, LANES), lambda bi, h, qb: (bi, 0, h)),
                  pl.BlockSpec((None, s, LANES), lambda bi, h, qb: (bi, 0, h)),
                  pl.BlockSpec((None, 2, tq, tq), lambda bi, h, qb: (h, 0, 0, 0)),
                  pl.BlockSpec((8, LANES), lambda bi, h, qb: (0, 0)),
                  pl.BlockSpec((1, LANES), lambda bi, h, qb: (0, 0))],
        out_specs=pl.BlockSpec((None, tq, LANES), lambda bi, h, qb: (bi, qb, h)),
        out_shape=jax.ShapeDtypeStruct((b, s, DIFF_WIDTH), _bf16),
        compiler_params=_cparams(("parallel", "parallel", "arbitrary")),
        name="diff_attn",
    )(dq, dk, dv, bt, lam_rows, subln_g.reshape(1, LANES))


def _compress_kernel(a_ref, pos_ref, w1_ref, w2_ref, o_ref):
    a = a_ref[...]
    half = (CMP_LEN // 2) * NSA_HD
    za = _dot((a + pos_ref[0]).astype(_bf16), w1_ref[:half, :])
    zb = _dot((a + pos_ref[1]).astype(_bf16), w1_ref[half:, :])
    n = a.shape[0]
    hid = za + pltpu.roll(zb, n - 1, 0)
    o_ref[...] = _dot(_gelu_tanh(hid).astype(_bf16), w2_ref[...]).astype(o_ref.dtype)


def _compress(a, pos, w1, w2dup):
    b, _, _, n, width = a.shape
    return pl.pallas_call(
        _compress_kernel,
        grid=(b, 2, NSA_KV_HEADS),
        in_specs=[pl.BlockSpec((None, None, None, n, width), lambda bi, kv, g: (bi, kv, g, 0, 0)),
                  pl.BlockSpec((None, 2, 1, width), lambda bi, kv, g: (kv, 0, 0, 0)),
                  pl.BlockSpec((None, 2 * width, CMP_HIDDEN), lambda bi, kv, g: (kv, 0, 0)),
                  pl.BlockSpec((None, CMP_HIDDEN, LANES), lambda bi, kv, g: (kv, 0, 0))],
        out_specs=pl.BlockSpec((None, None, None, n, LANES), lambda bi, kv, g: (bi, kv, g, 0, 0)),
        out_shape=jax.ShapeDtypeStruct((b, 2, NSA_KV_HEADS, n, LANES), _bf16),
        compiler_params=_cparams(("parallel", "parallel", "parallel")),
        name="compress",
    )(a, pos, w1, w2dup)


NSA_TQ = 128


def _stack_group_queries(q):
    rows = []
    for j in range(NSA_GROUP):
        blk = q[:, (j // 2) * LANES:(j // 2 + 1) * LANES]
        lo = _lane_half_mask(blk.shape)
        keep = lo if j % 2 == 0 else jnp.logical_not(lo)
        rows.append(jnp.where(keep, blk, jnp.zeros_like(blk)))
    return jnp.concatenate(rows, axis=0)


def _unstack_group_outputs(o4, tq):
    lo = _lane_half_mask((tq, LANES))
    pairs = [jnp.where(lo, o4[(2 * i) * tq:(2 * i + 1) * tq], o4[(2 * i + 1) * tq:(2 * i + 2) * tq])
             for i in range(NSA_GROUP // 2)]
    return jnp.concatenate(pairs, axis=1)


def _cmp_attn_kernel(q_ref, kc_ref, vc_ref, cs_ref, o_ref, sel_ref, *, n_sel):
    tq = NSA_TQ
    qb = pl.program_id(2)
    q4 = _stack_group_queries(q_ref[...])
    s = _dot_nt(q4, kc_ref[...])
    row = lax.broadcasted_iota(jnp.int32, s.shape, 0)
    pos4 = qb * tq + (row & (tq - 1))
    c = lax.broadcasted_iota(jnp.int32, s.shape, 1)
    ok = (c * CMP_STRIDE + (CMP_LEN - 1)) <= pos4
    s = jnp.where(ok, s, NEG_INF)
    m = jnp.max(s, axis=-1, keepdims=True)
    e = jnp.where(ok, jnp.exp(s - m), 0.0)
    l = jnp.sum(e, axis=-1, keepdims=True)
    p = e / jnp.where(l > 0.0, l, 1.0)
    o_ref[...] = _unstack_group_outputs(_dot(p.astype(_bf16), vc_ref[...]), tq).astype(o_ref.dtype)

    psum = p[0:tq] + p[tq:2 * tq] + p[2 * tq:3 * tq] + p[3 * tq:4 * tq]
    hi, lo = _split_hi_lo(psum)
    imp = _dot(hi, cs_ref[...]) + _dot(lo, cs_ref[...])
    blk = lax.broadcasted_iota(jnp.int32, imp.shape, 1)
    pos = qb * tq + lax.broadcasted_iota(jnp.int32, imp.shape, 0)
    cur = pos // SEL_LEN
    forced = (blk == 0) | (blk == cur) | (blk == cur - 1)
    score = jnp.where(blk <= cur, imp + jnp.where(forced, FORCE_BONUS, 0.0), NEG_INF)
    rank = jnp.zeros(score.shape, jnp.int32)
    for i in range(n_sel):
        si = jnp.max(jnp.where(blk == i, score, -jnp.inf), axis=-1, keepdims=True)
        beats = (si > score) | ((si == score) & (i < blk))
        rank = rank + beats.astype(jnp.int32)
    sel = (rank < SEL_TOPN) & (score > 0.5 * NEG_INF)
    sel_ref[...] = sel.astype(sel_ref.dtype)


def _cmp_attn(nq, kcmp, vcmp, cmp_sel, n_sel):
    b, s, _ = nq.shape
    tq = NSA_TQ
    n = kcmp.shape[-2]
    gw = NSA_GROUP * NSA_HD
    return pl.pallas_call(
        functools.partial(_cmp_attn_kernel, n_sel=n_sel),
        grid=(b, NSA_KV_HEADS, s // tq),
        in_specs=[pl.BlockSpec((None, tq, gw), lambda bi, g, qb: (bi, qb, g)),
                  pl.BlockSpec((None, None, n, LANES), lambda bi, g, qb: (bi, g, 0, 0)),
                  pl.BlockSpec((None, None, n, LANES), lambda bi, g, qb: (bi, g, 0, 0)),
                  pl.BlockSpec((n, LANES), lambda bi, g, qb: (0, 0))],
        out_specs=[pl.BlockSpec((None, tq, gw), lambda bi, g, qb: (bi, qb, g)),
                   pl.BlockSpec((None, None, tq, LANES), lambda bi, g, qb: (bi, g, qb, 0))],
        out_shape=[jax.ShapeDtypeStruct((b, s, NSA_WIDTH), _bf16),
                   jax.ShapeDtypeStruct((b, NSA_KV_HEADS, s, LANES), _bf16)],
        compiler_params=_cparams(("parallel", "parallel", "arbitrary")),
        name="cmp_attn",
    )(nq, kcmp, vcmp, cmp_sel)


def _sel_attn_kernel(q_ref, k_ref, v_ref, sel_ref, ex_ref, bt_ref, o_ref):
    tq = NSA_TQ
    qb = pl.program_id(2)
    q4 = _stack_group_queries(q_ref[...])
    sel = sel_ref[...]

    def block(kb):
        start = pl.multiple_of(kb * tq, tq)
        k = k_ref[pl.ds(start, tq), :]
        v = v_ref[pl.ds(start, tq), :]
        chosen = _dot(sel, ex_ref[:, pl.ds(start, tq)])
        pen = (chosen - 1.0) * (-NEG_INF)
        return _dot_nt(q4, k) + jnp.concatenate([pen] * NSA_GROUP, axis=0), v

    s, v = block(qb)
    carry = _softmax_first(s + bt_ref[0], v)
    s, v = block(jnp.maximum(qb - 1, 0))
    carry = _softmax_step(carry, s + bt_ref[1] + jnp.where(qb >= 1, 0.0, NEG_INF), v)

    def far(i, c):
        s, v = block(qb - 2 - i)
        return _softmax_step(c, s, v)

    m, l, acc = lax.fori_loop(0, jnp.maximum(qb - 1, 0), far, carry)
    o_ref[...] = _unstack_group_outputs(acc / l, tq).astype(o_ref.dtype)


def _sel_attn(nq, ks2, vs2, sel, expand, bt):
    b, s, _ = nq.shape
    tq = NSA_TQ
    gw = NSA_GROUP * NSA_HD
    return pl.pallas_call(
        _sel_attn_kernel,
        grid=(b, NSA_KV_HEADS, s // tq),
        in_specs=[pl.BlockSpec((None, tq, gw), lambda bi, g, qb: (bi, qb, g)),
                  pl.BlockSpec((None, s, LANES), lambda bi, g, qb: (bi, 0, g)),
                  pl.BlockSpec((None, s, LANES), lambda bi, g, qb: (bi, 0, g)),
                  pl.BlockSpec((None, None, tq, LANES), lambda bi, g, qb: (bi, g, qb, 0)),
                  pl.BlockSpec((LANES, s), lambda bi, g, qb: (0, 0)),
                  pl.BlockSpec((None, 2, NSA_GROUP * tq, tq), lambda bi, g, qb: (g, 0, 0, 0))],
        out_specs=pl.BlockSpec((None, tq, gw), lambda bi, g, qb: (bi, qb, g)),
        out_shape=jax.ShapeDtypeStruct((b, s, NSA_WIDTH), _bf16),
        compiler_params=_cparams(("parallel", "parallel", "arbitrary")),
        name="sel_attn",
    )(nq, ks2, vs2, sel, expand, bt)


WIN_TILES = WINDOW // NSA_TQ + 1


def _win_attn_kernel(q_ref, k_ref, v_ref, bt_ref, o_ref):
    tq = NSA_TQ
    qb = pl.program_id(2)
    q4 = _stack_group_queries(q_ref[...])
    carry = None
    for d in range(WIN_TILES):
        kb = jnp.maximum(qb - d, 0)
        start = pl.multiple_of(kb * tq, tq)
        k = k_ref[pl.ds(start, tq), :]
        v = v_ref[pl.ds(start, tq), :]
        s = _dot_nt(q4, k) + bt_ref[d]
        if d == 0:
            carry = _softmax_first(s, v)
        else:
            carry = _softmax_step(carry, s + jnp.where(qb >= d, 0.0, NEG_INF), v)
    m, l, acc = carry
    o_ref[...] = _unstack_group_outputs(acc / l, tq).astype(o_ref.dtype)


def _win_attn(nq, kw2, vw2, bt):
    b, s, _ = nq.shape
    tq = NSA_TQ
    gw = NSA_GROUP * NSA_HD
    return pl.pallas_call(
        _win_attn_kernel,
        grid=(b, NSA_KV_HEADS, s // tq),
        in_specs=[pl.BlockSpec((None, tq, gw), lambda bi, g, qb: (bi, qb, g)),
                  pl.BlockSpec((None, s, LANES), lambda bi, g, qb: (bi, 0, g)),
                  pl.BlockSpec((None, s, LANES), lambda bi, g, qb: (bi, 0, g)),
                  pl.BlockSpec((None, WIN_TILES, NSA_GROUP * tq, tq), lambda bi, g, qb: (g, 0, 0, 0))],
        out_specs=pl.BlockSpec((None, tq, gw), lambda bi, g, qb: (bi, qb, g)),
        out_shape=jax.ShapeDtypeStruct((b, s, NSA_WIDTH), _bf16),
        compiler_params=_cparams(("parallel", "parallel", "arbitrary")),
        name="win_attn",
    )(nq, kw2, vw2, bt)


MIX_TM = 256


def _mix_out_kernel(x_ref, od_ref, oc_ref, os_ref, ow_ref, ng_ref, ge_ref, wo_ref, g_ref, wq_ref,
                    x1_ref, xn_ref, qp_ref):
    sig = jax.nn.sigmoid(ng_ref[...])
    hi, lo = _split_hi_lo(sig)
    o_nsa = jnp.zeros(oc_ref.shape, _f32)
    for br, o_ref in enumerate((oc_ref, os_ref, ow_ref)):
        gate = _dot(hi, ge_ref[br]) + _dot(lo, ge_ref[br])
        o_nsa = o_nsa + gate * o_ref[...].astype(_f32)
    y = _dot(od_ref[...], wo_ref[:DIFF_WIDTH, :]) + _dot(o_nsa.astype(_bf16), wo_ref[DIFF_WIDTH:, :])
    x1 = x_ref[...] + y
    x1_ref[...] = x1
    xn = x1 * lax.rsqrt(jnp.mean(x1 * x1, axis=-1, keepdims=True) + EPS) * g_ref[...]
    xn = xn.astype(_bf16)
    xn_ref[...] = xn
    qp_ref[...] = _dot(xn, wq_ref[...]).astype(qp_ref.dtype)


def _mix_out(x2d, o_diff, o_cmp, o_sel, o_win, ng, gate_expand, w_out, g_ffn, w_q):
    t = x2d.shape[0]
    tm = MIX_TM
    nq = w_q.shape[1]
    row = lambda w: pl.BlockSpec((tm, w), lambda i: (i, 0))
    full = lambda a: pl.BlockSpec(a.shape, lambda i: (0,) * a.ndim)
    g2 = g_ffn.reshape(1, D_MODEL)
    return pl.pallas_call(
        _mix_out_kernel,
        grid=(t // tm,),
        in_specs=[row(D_MODEL), row(DIFF_WIDTH), row(NSA_WIDTH), row(NSA_WIDTH), row(NSA_WIDTH),
                  row(LANES), full(gate_expand), full(w_out), full(g2), full(w_q)],
        out_specs=[row(D_MODEL), row(D_MODEL), row(nq)],
        out_shape=[jax.ShapeDtypeStruct((t, D_MODEL), _f32),
                   jax.ShapeDtypeStruct((t, D_MODEL), _bf16),
                   jax.ShapeDtypeStruct((t, nq), _bf16)],
        compiler_params=_cparams(("parallel",)),
        name="mix_out",
    )(x2d, o_diff, o_cmp, o_sel, o_win, ng, gate_expand, w_out, g2, w_q)


ROUTE_TM = 256
SLOTS = 2 * PEER_HEADS * PEER_TOPK


def _top16_rows(s, rows):
    vals, ids = [], []
    n = s.shape[0]
    for _ in range(PEER_TOPK):
        m = jnp.max(s, axis=0, keepdims=True)
        idx = jnp.min(jnp.where(s == m, rows, n), axis=0, keepdims=True)
        vals.append(m)
        ids.append(idx)
        s = jnp.where(rows == idx, -jnp.inf, s)
    return jnp.concatenate(vals, axis=0), jnp.concatenate(ids, axis=0)


def _peer_route_kernel(q_ref, sk_ref, row_ref, gate_ref, rowt_ref, gatet_ref):
    tm = ROUTE_TM
    rows_k = lax.broadcasted_iota(jnp.int32, (PEER_NKEYS, tm), 0)
    rows_c = lax.broadcasted_iota(jnp.int32, (PEER_TOPK * PEER_TOPK, tm), 0)
    half_d = PEER_DKEY // 2

    def head(h, _):
        col = pl.multiple_of(h * PEER_DKEY, PEER_DKEY)
        q0 = q_ref[:, pl.ds(col, half_d)]
        q1 = q_ref[:, pl.ds(col + half_d, half_d)]
        sv0, si0 = _top16_rows(_dot_nt(sk_ref[2 * h], q0), rows_k)
        sv1, si1 = _top16_rows(_dot_nt(sk_ref[2 * h + 1], q1), rows_k)
        cand = jnp.concatenate([sv0[a:a + 1] + sv1 for a in range(PEER_TOPK)], axis=0)
        cidx = jnp.concatenate([si0[a:a + 1] * PEER_NKEYS + si1 for a in range(PEER_TOPK)], axis=0)
        cv, ci = _top16_rows(cand, rows_c)
        e = jnp.concatenate(
            [jnp.sum(jnp.where(rows_c == ci[k:k + 1], cidx, 0), axis=0, keepdims=True)
             for k in range(PEER_TOPK)], axis=0)
        ex = jnp.exp(cv - cv[0:1])
        gate = ex / jnp.sum(ex, axis=0, keepdims=True)
        upper = e >= HALF_EXPERTS
        base = pl.multiple_of(h * PEER_TOPK, PEER_TOPK)
        rowt_ref[pl.ds(base, PEER_TOPK), :] = (e & (HALF_EXPERTS - 1)) * PAIR_WORD_ROWS
        base2 = pl.multiple_of(h * 2 * PEER_TOPK, 2 * PEER_TOPK)
        gatet_ref[pl.ds(base2, PEER_TOPK), :] = jnp.where(upper, 0.0, gate)
        gatet_ref[pl.ds(base2 + PEER_TOPK, PEER_TOPK), :] = jnp.where(upper, gate, 0.0)
        return 0

    lax.fori_loop(0, PEER_HEADS, head, 0)
    row_ref[...] = rowt_ref[...].T
    gate_ref[...] = gatet_ref[...].T


def _peer_route(qp, sk):
    t = qp.shape[0]
    tm = ROUTE_TM
    npick = PEER_HEADS * PEER_TOPK
    return pl.pallas_call(
        _peer_route_kernel,
        grid=(t // tm,),
        in_specs=[pl.BlockSpec((tm, qp.shape[1]), lambda i: (i, 0)),
                  pl.BlockSpec(sk.shape, lambda i: (0, 0, 0))],
        out_specs=[pl.BlockSpec((tm, npick), lambda i: (i, 0)),
                   pl.BlockSpec((tm, SLOTS), lambda i: (i, 0))],
        out_shape=[jax.ShapeDtypeStruct((t, npick), jnp.int32),
                   jax.ShapeDtypeStruct((t, SLOTS), _f32)],
        scratch_shapes=[pltpu.VMEM((npick, tm), jnp.int32), pltpu.VMEM((SLOTS, tm), _f32)],
        compiler_params=_cparams(("parallel",)),
        name="peer_route",
    )(qp, sk)


PEER_TT = 64
NPICK = PEER_HEADS * PEER_TOPK
CHUNK_ROWS = PEER_TOPK * PAIR_ROWS
STAGE_COLS = NPICK * PAIR_ROWS


def _pair_table(w):
    e = w.shape[0]
    w4 = w.astype(_bf16).reshape(2, e // 2, PAIR_WORD_ROWS, LANES)
    words = lax.bitcast_convert_type(jnp.transpose(w4, (1, 2, 3, 0)), jnp.int32)
    return words.reshape(e // 2 * PAIR_WORD_ROWS, LANES)


def _gather_head(tab_ref, tok_rows, c):
    tiles = []
    for k in range(PEER_TOPK):
        r = pl.multiple_of(tok_rows[c * PEER_TOPK + k], PAIR_WORD_ROWS)
        tiles.append(pltpu.bitcast(tab_ref[pl.ds(r, PAIR_WORD_ROWS), :], _bf16))
    return jnp.concatenate(tiles, axis=0)


def _peer_up_kernel(row_ref, x_ref, tab_ref, diag_ref, fold_ref, gsum_ref, h_ref, z_ref):
    def token(t, _):
        x8 = x_ref[t]
        tok_rows = row_ref.at[pl.ds(t * NPICK, NPICK)]
        for c in range(PEER_HEADS):
            rows = _gather_head(tab_ref, tok_rows, c)
            z_ref[t, :, c * CHUNK_ROWS:(c + 1) * CHUNK_ROWS] = _dot_nt(x8, rows)
        return 0

    lax.fori_loop(0, PEER_TT, token, 0, unroll=2)
    zm = (z_ref[...] * diag_ref[...][None]).reshape(PEER_TT * 8, STAGE_COLS)
    hi, lo = _split_hi_lo(zm)
    part = _dot(hi, fold_ref[...]) + _dot(lo, fold_ref[...])
    hi, lo = _split_hi_lo(part)
    h_ref[...] = _dot(gsum_ref[...], hi) + _dot(gsum_ref[...], lo)


def _peer_up(rows_flat, x8, tab, diag, fold, gsum):
    t = x8.shape[0]
    tt = PEER_TT
    full = lambda a: pl.BlockSpec(a.shape, lambda i: (0,) * a.ndim)
    return pl.pallas_call(
        _peer_up_kernel,
        grid=(t // tt,),
        in_specs=[pl.BlockSpec((tt * NPICK,), lambda i: (i,), memory_space=pltpu.SMEM),
                  pl.BlockSpec((tt, 8, LANES), lambda i: (i, 0, 0)),
                  pl.BlockSpec(tab.shape, lambda i: (0, 0), pipeline_mode=pl.Buffered(1)),
                  full(diag), full(fold), full(gsum)],
        out_specs=pl.BlockSpec((tt, SLOTS), lambda i: (i, 0)),
        out_shape=jax.ShapeDtypeStruct((t, SLOTS), _f32),
        scratch_shapes=[pltpu.VMEM((tt, 8, STAGE_COLS), _f32)],
        compiler_params=_cparams(("arbitrary",)),
        name="peer_up",
    )(rows_flat, x8, tab, diag, fold, gsum)


def _peer_down_kernel(row_ref, h_ref, gate_ref, tab_ref, ex_ref, diag_ref, y_ref, wexp_ref):
    w = _gelu_tanh(h_ref[...]) * gate_ref[...]
    hi, lo = _split_hi_lo(w)
    wexp_ref[...] = _dot(hi, ex_ref[...]) + _dot(lo, ex_ref[...])

    def token(t, _):
        a = jnp.broadcast_to(wexp_ref[pl.ds(t, 1), :], diag_ref.shape) * diag_ref[...]
        ahi = a.astype(_bf16).astype(_f32)
        a2 = jnp.concatenate([ahi, a - ahi], axis=0).astype(_bf16)
        acc = jnp.zeros((2 * 8, LANES), _f32)
        tok_rows = row_ref.at[pl.ds(t * NPICK, NPICK)]
        for c in range(PEER_HEADS):
            rows = _gather_head(tab_ref, tok_rows, c)
            acc = acc + _dot(a2[:, c * CHUNK_ROWS:(c + 1) * CHUNK_ROWS], rows)
        y_ref[t] = acc[:8] + acc[8:]
        return 0

    lax.fori_loop(0, PEER_TT, token, 0, unroll=2)


def _peer_down(rows_flat, h2, gate2, tab, expand, diag):
    t = h2.shape[0]
    tt = PEER_TT
    full = lambda a: pl.BlockSpec(a.shape, lambda i: (0,) * a.ndim)
    return pl.pallas_call(
        _peer_down_kernel,
        grid=(t // tt,),
        in_specs=[pl.BlockSpec((tt * NPICK,), lambda i: (i,), memory_space=pltpu.SMEM),
                  pl.BlockSpec((tt, SLOTS), lambda i: (i, 0)),
                  pl.BlockSpec((tt, SLOTS), lambda i: (i, 0)),
                  pl.BlockSpec(tab.shape, lambda i: (0, 0), pipeline_mode=pl.Buffered(1)),
                  full(expand), full(diag)],
        out_specs=pl.BlockSpec((tt, 8, LANES), lambda i: (i, 0, 0)),
        out_shape=jax.ShapeDtypeStruct((t, 8, LANES), _f32),
        scratch_shapes=[pltpu.VMEM((tt, STAGE_COLS), _f32)],
        compiler_params=_cparams(("arbitrary",)),
        name="peer_down",
    )(rows_flat, h2, gate2, tab, expand, diag)


def _peer_constants():
    col = np.arange(STAGE_COLS)
    p, r = col // PAIR_ROWS, col % PAIR_ROWS
    slot = (p // PEER_TOPK) * 2 * PEER_TOPK + (r % 2) * PEER_TOPK + p % PEER_TOPK
    diag = (r[None, :] // 2 == np.arange(8)[:, None]).astype(np.float32)
    fold = (slot[:, None] == np.arange(SLOTS)[None, :]).astype(np.float32)
    gsum = (np.arange(PEER_TT * 8)[None, :] // 8 == np.arange(PEER_TT)[:, None]).astype(np.float32)
    return (jnp.asarray(diag, _f32), jnp.asarray(fold, _bf16), jnp.asarray(fold.T, _bf16),
            jnp.asarray(gsum, _bf16))


FIN_TM = 512


def _final_norm_kernel(x_ref, y_ref, g_ref, o_ref):
    z = x_ref[...] + y_ref[...]
    o_ref[...] = z * lax.rsqrt(jnp.mean(z * z, axis=-1, keepdims=True) + EPS) * g_ref[...]


def _final_norm(x1, y, g):
    t = x1.shape[0]
    row = pl.BlockSpec((FIN_TM, D_MODEL), lambda i: (i, 0))
    return pl.pallas_call(
        _final_norm_kernel,
        grid=(t // FIN_TM,),
        in_specs=[row, row, pl.BlockSpec((1, D_MODEL), lambda i: (0, 0))],
        out_specs=row,
        out_shape=jax.ShapeDtypeStruct((t, D_MODEL), _f32),
        compiler_params=_cparams(("parallel",)),
        name="final_norm",
    )(x1, y, g.reshape(1, D_MODEL))


def _cmp_to_sel_counts(n_rows, n_sel):
    r = SEL_LEN // CMP_STRIDE
    c = CMP_LEN // CMP_STRIDE
    off = np.arange(n_rows)[:, None] - r * np.arange(n_sel)[None, :] + (c - 1)
    counts = np.zeros((n_rows, LANES), np.float32)
    for n in range(c):
        counts[:, :n_sel] += ((off - n >= 0) & (off - n < r)).astype(np.float32)
    counts[n_rows - 1] = 0.0
    return counts


def _attention(x2d, b, s, w_in, norm_mix_g, rel_bias, lam_rows, subln_g,
               cmp_pos_k, cmp_pos_v, cmp_w1_k, cmp_w2_k, cmp_w1_v, cmp_w2_v):
    dq, dk, dv, nq, kcvc, ks2, vs2, kw2, vw2, ng = _in_proj(x2d, norm_mix_g, _arrange_w_in(w_in))
    to3 = lambda a: a.reshape(b, s, a.shape[-1])

    bt_diff = _bias_tiles(rel_bias[:, :DIFF_HEADS], DIFF_TQ, DIFF_TQ, 2)
    o_diff = _diff_attn(to3(dq), to3(dk), to3(dv), bt_diff, lam_rows, subln_g)

    n = s // CMP_STRIDE
    a = kcvc.reshape(b, n, CMP_STRIDE, 2, NSA_KV_HEADS, NSA_HD)
    a = jnp.transpose(a, (0, 3, 4, 1, 2, 5)).reshape(b, 2, NSA_KV_HEADS, n, CMP_STRIDE * NSA_HD)
    pos = jnp.stack([cmp_pos_k, cmp_pos_v]).reshape(2, 2, 1, CMP_STRIDE * NSA_HD)
    w1 = jnp.stack([cmp_w1_k, cmp_w1_v]).astype(_bf16)
    w2 = jnp.stack([cmp_w2_k, cmp_w2_v])
    w2dup = jnp.concatenate([w2, w2], axis=-1).astype(_bf16)
    cmp_kv = _compress(a, pos, w1, w2dup)

    n_sel = s // SEL_LEN
    counts = jnp.asarray(_cmp_to_sel_counts(n, n_sel), _bf16)
    nq3 = to3(nq)
    o_cmp, sel = _cmp_attn(nq3, cmp_kv[:, 0], cmp_kv[:, 1], counts, n_sel)

    nsa_bias = rel_bias[:, DIFF_HEADS:]
    tq = NSA_TQ
    bt_sel = _bias_tiles(nsa_bias, tq, tq, 2).reshape(NSA_KV_HEADS, NSA_GROUP, 2, tq, tq)
    bt_sel = jnp.transpose(bt_sel, (0, 2, 1, 3, 4)).reshape(NSA_KV_HEADS, 2, NSA_GROUP * tq, tq)
    expand = jnp.asarray(np.arange(s)[None, :] // SEL_LEN == np.arange(LANES)[:, None], _bf16)
    o_sel = _sel_attn(nq3, to3(ks2), to3(vs2), sel, expand, bt_sel)

    bt_win = _bias_tiles(nsa_bias, tq, tq, WIN_TILES, window=WINDOW)
    bt_win = bt_win.reshape(NSA_KV_HEADS, NSA_GROUP, WIN_TILES, tq, tq)
    bt_win = jnp.transpose(bt_win, (0, 2, 1, 3, 4)).reshape(NSA_KV_HEADS, WIN_TILES, NSA_GROUP * tq, tq)
    o_win = _win_attn(nq3, to3(kw2), to3(vw2), bt_win)

    flat = lambda a3: a3.reshape(b * s, a3.shape[-1])
    return flat(o_diff), flat(o_cmp), flat(o_sel), flat(o_win), ng


def _gate_expand():
    ge = np.zeros((3, LANES, NSA_WIDTH), np.float32)
    for head in range(NSA_HEADS):
        for br in range(3):
            ge[br, 3 * head + br, head * NSA_HD:(head + 1) * NSA_HD] = 1.0
    return jnp.asarray(ge, _bf16)


def _peer(x1, xn, qp, peer_sub_keys, peer_u, peer_v):
    t = x1.shape[0]
    sk = peer_sub_keys.reshape(2 * PEER_HEADS, PEER_NKEYS, PEER_DKEY // 2).astype(_bf16)
    rows, gate2 = _peer_route(qp, sk)
    rows_flat = rows.reshape(t * NPICK)
    diag, fold, expand, gsum = _peer_constants()
    h2 = _peer_up(rows_flat, xn.reshape(t, 8, LANES), _pair_table(peer_u), diag, fold, gsum)
    y = _peer_down(rows_flat, h2, gate2, _pair_table(peer_v), expand, diag)
    return y.reshape(t, D_MODEL)


def kernel(x, w_in, w_out, norm_mix_g, norm_ffn_g, final_norm_g, rel_bias, diff_lq1, diff_lk1, diff_lq2, diff_lk2, diff_subln_g, cmp_pos_k, cmp_pos_v, cmp_w1_k, cmp_w2_k, cmp_w1_v, cmp_w2_v, peer_w_q, peer_sub_keys, peer_u, peer_v):
    b, s, d = x.shape
    x2d = x.reshape(b * s, d)
    lam_rows = jnp.zeros((8, LANES), _f32).at[0:4, :DIFF_HD].set(
        jnp.stack([diff_lq1[0], diff_lk1[0], diff_lq2[0], diff_lk2[0]]))
    o_diff, o_cmp, o_sel, o_win, ng = _attention(
        x2d, b, s, w_in[0], norm_mix_g[0], rel_bias, lam_rows, diff_subln_g[0],
        cmp_pos_k[0], cmp_pos_v[0], cmp_w1_k[0], cmp_w2_k[0], cmp_w1_v[0], cmp_w2_v[0])
    x1, xn, qp = _mix_out(x2d, o_diff, o_cmp, o_sel, o_win, ng, _gate_expand(),
                          w_out[0].astype(_bf16), norm_ffn_g[0], peer_w_q[0].astype(_bf16))
    y = _peer(x1, xn, qp, peer_sub_keys[0], peer_u[0], peer_v[0])
    return _final_norm(x1, y, final_norm_g).reshape(b, s, d)
```

```python
import functools
import math

import numpy as np
import jax
import jax.numpy as jnp
from jax import lax
from jax.experimental import pallas as pl
from jax.experimental.pallas import tpu as pltpu

D_MODEL = 1024
DIFF_HEADS = 4
DIFF_HD = 64
DIFF_VD = 2 * DIFF_HD
DIFF_WIDTH = DIFF_HEADS * DIFF_VD
NSA_HEADS = 8
NSA_KV_HEADS = 2
NSA_GROUP = NSA_HEADS // NSA_KV_HEADS
NSA_HD = 64
NSA_WIDTH = NSA_HEADS * NSA_HD
CMP_LEN = 32
CMP_STRIDE = 16
CMP_HIDDEN = 256
SEL_LEN = 64
SEL_TOPN = 8
WINDOW = 512
FORCE_BONUS = 1000.0
NEG_INF = -1e30
N_BUCKETS = 32
MAX_DISTANCE = 128
PEER_HEADS = 8
PEER_NKEYS = 128
PEER_EXPERTS = PEER_NKEYS ** 2
PEER_DKEY = 256
PEER_TOPK = 16
EPS = 1e-6
LAMBDA_INIT = 0.8 - 0.6 * math.exp(-0.3 * 0)

KV_COLS = NSA_KV_HEADS * NSA_HD
SPLIT_SIZES = (DIFF_HEADS * 2 * DIFF_HD, DIFF_HEADS * 2 * DIFF_HD, DIFF_WIDTH, NSA_WIDTH,
               KV_COLS, KV_COLS, KV_COLS, KV_COLS, KV_COLS, KV_COLS, 3 * NSA_HEADS)
SPLIT_OFF = tuple(int(v) for v in np.concatenate([[0], np.cumsum(SPLIT_SIZES)]))

LANES = 128
VMEM_LIMIT = 56 * 1024 * 1024
HALF_EXPERTS = PEER_EXPERTS // 2
PAIR_ROWS = 16
PAIR_WORD_ROWS = PAIR_ROWS // 2

_f32 = jnp.float32
_bf16 = jnp.bfloat16


def _cparams(sem):
    return pltpu.CompilerParams(dimension_semantics=sem, vmem_limit_bytes=VMEM_LIMIT)


def _dot(a, b):
    return jnp.dot(a, b, preferred_element_type=_f32)


def _dot_nt(a, b):
    return lax.dot_general(a, b, (((1,), (1,)), ((), ())), preferred_element_type=_f32)


def _split_hi_lo(x):
    hi = x.astype(_bf16)
    lo = (x - hi.astype(_f32)).astype(_bf16)
    return hi, lo


def _gelu_tanh(x):
    c = math.sqrt(2.0 / math.pi)
    return 0.5 * x * (1.0 + jnp.tanh(c * (x + 0.044715 * (x * x * x))))


IN_TM = 512
_IN_GROUPS = (("dq", 512, _bf16), ("dk", 512, _bf16), ("dv", 512, _bf16), ("nq", 512, _bf16),
              ("kcvc", 256, _f32), ("ks2", 256, _bf16), ("vs2", 256, _bf16),
              ("kw2", 256, _bf16), ("vw2", 256, _bf16), ("ng", 128, _f32))


def _in_proj_kernel(x_ref, g_ref, w_ref, *out_refs):
    x = x_ref[...]
    y = x * lax.rsqrt(jnp.mean(x * x, axis=-1, keepdims=True) + EPS)
    h = (y * g_ref[...]).astype(_bf16)
    off = 0
    for (_, width, dt), o_ref in zip(_IN_GROUPS, out_refs):
        o_ref[...] = _dot(h, w_ref[:, off:off + width]).astype(dt)
        off += width


def _arrange_w_in(w_in):
    o = SPLIT_OFF
    sl = lambda i: w_in[:, o[i]:o[i + 1]]
    dup = lambda w: jnp.concatenate([w[:, :64], w[:, :64], w[:, 64:], w[:, 64:]], axis=1)
    ng = jnp.pad(sl(10), ((0, 0), (0, LANES - 3 * NSA_HEADS)))
    cols = [sl(0) * (DIFF_HD ** -0.5), sl(1), sl(2), sl(3) * (NSA_HD ** -0.5),
            sl(4), sl(5), dup(sl(6)), dup(sl(7)), dup(sl(8)), dup(sl(9)), ng]
    return jnp.concatenate(cols, axis=1).astype(_bf16)


def _in_proj(x2d, g, w_arr):
    t = x2d.shape[0]
    ncol = w_arr.shape[1]
    out_shape = [jax.ShapeDtypeStruct((t, wd), dt) for _, wd, dt in _IN_GROUPS]
    out_specs = [pl.BlockSpec((IN_TM, wd), lambda i: (i, 0)) for _, wd, _ in _IN_GROUPS]
    return pl.pallas_call(
        _in_proj_kernel,
        grid=(t // IN_TM,),
        in_specs=[pl.BlockSpec((IN_TM, D_MODEL), lambda i: (i, 0)),
                  pl.BlockSpec((1, D_MODEL), lambda i: (0, 0)),
                  pl.BlockSpec((D_MODEL, ncol), lambda i: (0, 0))],
        out_specs=out_specs,
        out_shape=out_shape,
        compiler_params=_cparams(("parallel",)),
        name="in_proj",
    )(x2d, g.reshape(1, D_MODEL), w_arr)


def _bucket_table(n):
    rel = np.arange(n)
    max_exact = N_BUCKETS // 2
    nf = np.maximum(rel, 1).astype(np.float32)
    large = max_exact + (np.log(nf / np.float32(max_exact)) / np.float32(math.log(MAX_DISTANCE / max_exact))
                         * np.float32(N_BUCKETS - max_exact)).astype(np.int32)
    large = np.minimum(large, N_BUCKETS - 1)
    return np.where(rel < max_exact, rel, large).astype(np.int32)


def _bias_tiles(rel_bias_heads, tq, tk, n_tiles, window=None):
    span = n_tiles * tq + tk
    tab = rel_bias_heads.astype(_f32) - rel_bias_heads[N_BUCKETS - 1][None, :].astype(_f32)
    rev = tab[_bucket_table(span)[::-1]].T
    rev = jnp.pad(rev, ((0, 0), (0, tk)))
    starts = span - 1 - np.arange(n_tiles * tq)
    rows = jax.vmap(lambda st: lax.dynamic_slice_in_dim(rev, st, tk, axis=1))(jnp.asarray(starts))
    tiles = jnp.transpose(rows, (1, 0, 2)).reshape(rev.shape[0], n_tiles, tq, tk)
    rel = (np.arange(n_tiles)[:, None, None] * tq + np.arange(tq)[None, :, None]
           - np.arange(tk)[None, None, :])
    ok = rel >= 0
    if window is not None:
        ok &= rel < window
    return jnp.where(jnp.asarray(ok)[None], tiles, NEG_INF)


def _softmax_first(s, v):
    m = jnp.max(s, axis=-1, keepdims=True)
    p = jnp.exp(s - m)
    l = jnp.sum(p, axis=-1, keepdims=True)
    acc = _dot(p.astype(_bf16), v)
    return m, l, acc


def _softmax_step(carry, s, v):
    m, l, acc = carry
    m_new = jnp.maximum(m, jnp.max(s, axis=-1, keepdims=True))
    alpha = jnp.exp(m - m_new)
    p = jnp.exp(s - m_new)
    l = alpha * l + jnp.sum(p, axis=-1, keepdims=True)
    acc = alpha * acc + _dot(p.astype(_bf16), v)
    return m_new, l, acc


def _lane_half_mask(shape):
    return lax.broadcasted_iota(jnp.int32, shape, len(shape) - 1) < (LANES // 2)


DIFF_TQ = 256


def _diff_attn_kernel(q_ref, k_ref, v_ref, bt_ref, lam_ref, g_ref, o_ref):
    tq = DIFF_TQ
    qb = pl.program_id(2)
    q = q_ref[...]
    lo = _lane_half_mask(q.shape)
    zero = jnp.zeros_like(q)
    q2 = jnp.concatenate([jnp.where(lo, q, zero), jnp.where(lo, zero, q)], axis=0)

    def kv(kb):
        start = pl.multiple_of(kb * tq, tq)
        return k_ref[pl.ds(start, tq), :], v_ref[pl.ds(start, tq), :]

    k, v = kv(qb)
    b0 = bt_ref[0]
    carry = _softmax_first(_dot_nt(q2, k) + jnp.concatenate([b0, b0], axis=0), v)
    k, v = kv(jnp.maximum(qb - 1, 0))
    b1 = bt_ref[1] + jnp.where(qb >= 1, 0.0, NEG_INF)
    carry = _softmax_step(carry, _dot_nt(q2, k) + jnp.concatenate([b1, b1], axis=0), v)

    def far(i, c):
        k, v = kv(qb - 2 - i)
        return _softmax_step(c, _dot_nt(q2, k), v)

    m, l, acc = lax.fori_loop(0, jnp.maximum(qb - 1, 0), far, carry)
    o2 = acc / l
    lv = lam_ref[...]
    lam = (jnp.exp(jnp.sum(lv[0:1] * lv[1:2], axis=-1, keepdims=True))
           - jnp.exp(jnp.sum(lv[2:3] * lv[3:4], axis=-1, keepdims=True)) + LAMBDA_INIT)
    o = o2[:tq] - lam * o2[tq:]
    y = o * lax.rsqrt(jnp.mean(o * o, axis=-1, keepdims=True) + EPS)
    o_ref[...] = (y * g_ref[...] * (1.0 - LAMBDA_INIT)).astype(o_ref.dtype)


def _diff_attn(dq, dk, dv, bt, lam_rows, subln_g):
    b, s, _ = dq.shape
    tq = DIFF_TQ
    return pl.pallas_call(
        _diff_attn_kernel,
        grid=(b, DIFF_HEADS, s // tq),
        in_specs=[pl.BlockSpec((None, tq, LANES), lambda bi, h, qb: (bi, qb, h)),
                  pl.BlockSpec((None, s, LANES), lambda bi, h, qb: (bi, 0, h)),
                  pl.BlockSpec((None, s, LANES), lambda bi, h, qb: (bi, 0, h)),
                  pl.BlockSpec((None, 2, tq, tq), lambda bi, h, qb: (h, 0, 0, 0)),
                  pl.BlockSpec((8, LANES), lambda bi, h, qb: (0, 0)),
                  pl.BlockSpec((1, LANES), lambda bi, h, qb: (0, 0))],
        out_specs=pl.BlockSpec((None, tq, LANES), lambda bi, h, qb: (bi, qb, h)),
        out_shape=jax.ShapeDtypeStruct((b, s, DIFF_WIDTH), _bf16),
        compiler_params=_cparams(("parallel", "parallel", "arbitrary")),
        name="diff_attn",
    )(dq, dk, dv, bt, lam_rows, subln_g.reshape(1, LANES))


def _compress_kernel(a_ref, pos_ref, w1_ref, w2_ref, o_ref):
    a = a_ref[...]
    half = (CMP_LEN // 2) * NSA_HD
    za = _dot((a + pos_ref[0]).astype(_bf16), w1_ref[:half, :])
    zb = _dot((a + pos_ref[1]).astype(_bf16), w1_ref[half:, :])
    n = a.shape[0]
    hid = za + pltpu.roll(zb, n - 1, 0)
    o_ref[...] = _dot(_gelu_tanh(hid).astype(_bf16), w2_ref[...]).astype(o_ref.dtype)


def _compress(a, pos, w1, w2dup):
    b, _, _, n, width = a.shape
    return pl.pallas_call(
        _compress_kernel,
        grid=(b, 2, NSA_KV_HEADS),
        in_specs=[pl.BlockSpec((None, None, None, n, width), lambda bi, kv, g: (bi, kv, g, 0, 0)),
                  pl.BlockSpec((None, 2, 1, width), lambda bi, kv, g: (kv, 0, 0, 0)),
                  pl.BlockSpec((None, 2 * width, CMP_HIDDEN), lambda bi, kv, g: (kv, 0, 0)),
                  pl.BlockSpec((None, CMP_HIDDEN, LANES), lambda bi, kv, g: (kv, 0, 0))],
        out_specs=pl.BlockSpec((None, None, None, n, LANES), lambda bi, kv, g: (bi, kv, g, 0, 0)),
        out_shape=jax.ShapeDtypeStruct((b, 2, NSA_KV_HEADS, n, LANES), _bf16),
        compiler_params=_cparams(("parallel", "parallel", "parallel")),
        name="compress",
    )(a, pos, w1, w2dup)


NSA_TQ = 128


def _stack_group_queries(q):
    rows = []
    for j in range(NSA_GROUP):
        blk = q[:, (j // 2) * LANES:(j // 2 + 1) * LANES]
        lo = _lane_half_mask(blk.shape)
        keep = lo if j % 2 == 0 else jnp.logical_not(lo)
        rows.append(jnp.where(keep, blk, jnp.zeros_like(blk)))
    return jnp.concatenate(rows, axis=0)


def _unstack_group_outputs(o4, tq):
    lo = _lane_half_mask((tq, LANES))
    pairs = [jnp.where(lo, o4[(2 * i) * tq:(2 * i + 1) * tq], o4[(2 * i + 1) * tq:(2 * i + 2) * tq])
             for i in range(NSA_GROUP // 2)]
    return jnp.concatenate(pairs, axis=1)


def _cmp_attn_kernel(q_ref, kc_ref, vc_ref, cs_ref, o_ref, sel_ref, *, n_sel):
    tq = NSA_TQ
    qb = pl.program_id(2)
    q4 = _stack_group_queries(q_ref[...])
    s = _dot_nt(q4, kc_ref[...])
    row = lax.broadcasted_iota(jnp.int32, s.shape, 0)
    pos4 = qb * tq + (row & (tq - 1))
    c = lax.broadcasted_iota(jnp.int32, s.shape, 1)
    ok = (c * CMP_STRIDE + (CMP_LEN - 1)) <= pos4
    s = jnp.where(ok, s, NEG_INF)
    m = jnp.max(s, axis=-1, keepdims=True)
    e = jnp.where(ok, jnp.exp(s - m), 0.0)
    l = jnp.sum(e, axis=-1, keepdims=True)
    p = e / jnp.where(l > 0.0, l, 1.0)
    o_ref[...] = _unstack_group_outputs(_dot(p.astype(_bf16), vc_ref[...]), tq).astype(o_ref.dtype)

    psum = p[0:tq] + p[tq:2 * tq] + p[2 * tq:3 * tq] + p[3 * tq:4 * tq]
    hi, lo = _split_hi_lo(psum)
    imp = _dot(hi, cs_ref[...]) + _dot(lo, cs_ref[...])
    blk = lax.broadcasted_iota(jnp.int32, imp.shape, 1)
    pos = qb * tq + lax.broadcasted_iota(jnp.int32, imp.shape, 0)
    cur = pos // SEL_LEN
    forced = (blk == 0) | (blk == cur) | (blk == cur - 1)
    score = jnp.where(blk <= cur, imp + jnp.where(forced, FORCE_BONUS, 0.0), NEG_INF)
    rank = jnp.zeros(score.shape, jnp.int32)
    for i in range(n_sel):
        si = jnp.max(jnp.where(blk == i, score, -jnp.inf), axis=-1, keepdims=True)
        beats = (si > score) | ((si == score) & (i < blk))
        rank = rank + beats.astype(jnp.int32)
    sel = (rank < SEL_TOPN) & (score > 0.5 * NEG_INF)
    sel_ref[...] = sel.astype(sel_ref.dtype)


def _cmp_attn(nq, kcmp, vcmp, cmp_sel, n_sel):
    b, s, _ = nq.shape
    tq = NSA_TQ
    n = kcmp.shape[-2]
    gw = NSA_GROUP * NSA_HD
    return pl.pallas_call(
        functools.partial(_cmp_attn_kernel, n_sel=n_sel),
        grid=(b, NSA_KV_HEADS, s // tq),
        in_specs=[pl.BlockSpec((None, tq, gw), lambda bi, g, qb: (bi, qb, g)),
                  pl.BlockSpec((None, None, n, LANES), lambda bi, g, qb: (bi, g, 0, 0)),
                  pl.BlockSpec((None, None, n, LANES), lambda bi, g, qb: (bi, g, 0, 0)),
                  pl.BlockSpec((n, LANES), lambda bi, g, qb: (0, 0))],
        out_specs=[pl.BlockSpec((None, tq, gw), lambda bi, g, qb: (bi, qb, g)),
                   pl.BlockSpec((None, None, tq, LANES), lambda bi, g, qb: (bi, g, qb, 0))],
        out_shape=[jax.ShapeDtypeStruct((b, s, NSA_WIDTH), _bf16),
                   jax.ShapeDtypeStruct((b, NSA_KV_HEADS, s, LANES), _bf16)],
        compiler_params=_cparams(("parallel", "parallel", "arbitrary")),
        name="cmp_attn",
    )(nq, kcmp, vcmp, cmp_sel)


def _sel_attn_kernel(q_ref, k_ref, v_ref, sel_ref, ex_ref, bt_ref, o_ref):
    tq = NSA_TQ
    qb = pl.program_id(2)
    q4 = _stack_group_queries(q_ref[...])
    sel = sel_ref[...]

    def block(kb):
        start = pl.multiple_of(kb * tq, tq)
        k = k_ref[pl.ds(start, tq), :]
        v = v_ref[pl.ds(start, tq), :]
        chosen = _dot(sel, ex_ref[:, pl.ds(start, tq)])
        pen = (chosen - 1.0) * (-NEG_INF)
        return _dot_nt(q4, k) + jnp.concatenate([pen] * NSA_GROUP, axis=0), v

    s, v = block(qb)
    carry = _softmax_first(s + bt_ref[0], v)
    s, v = block(jnp.maximum(qb - 1, 0))
    carry = _softmax_step(carry, s + bt_ref[1] + jnp.where(qb >= 1, 0.0, NEG_INF), v)

    def far(i, c):
        s, v = block(qb - 2 - i)
        return _softmax_step(c, s, v)

    m, l, acc = lax.fori_loop(0, jnp.maximum(qb - 1, 0), far, carry)
    o_ref[...] = _unstack_group_outputs(acc / l, tq).astype(o_ref.dtype)


def _sel_attn(nq, ks2, vs2, sel, expand, bt):
    b, s, _ = nq.shape
    tq = NSA_TQ
    gw = NSA_GROUP * NSA_HD
    return pl.pallas_call(
        _sel_attn_kernel,
        grid=(b, NSA_KV_HEADS, s // tq),
        in_specs=[pl.BlockSpec((None, tq, gw), lambda bi, g, qb: (bi, qb, g)),
                  pl.BlockSpec((None, s, LANES), lambda bi, g, qb: (bi, 0, g)),
                  pl.BlockSpec((None, s, LANES), lambda bi, g, qb: (bi, 0, g)),
                  pl.BlockSpec((None, None, tq, LANES), lambda bi, g, qb: (bi, g, qb, 0)),
                  pl.BlockSpec((LANES, s), lambda bi, g, qb: (0, 0)),
                  pl.BlockSpec((None, 2, NSA_GROUP * tq, tq), lambda bi, g, qb: (g, 0, 0, 0))],
        out_specs=pl.BlockSpec((None, tq, gw), lambda bi, g, qb: (bi, qb, g)),
        out_shape=jax.ShapeDtypeStruct((b, s, NSA_WIDTH), _bf16),
        compiler_params=_cparams(("parallel", "parallel", "arbitrary")),
        name="sel_attn",
    )(nq, ks2, vs2, sel, expand, bt)


WIN_TILES = WINDOW // NSA_TQ + 1


def _win_attn_kernel(q_ref, k_ref, v_ref, bt_ref, o_ref):
    tq = NSA_TQ
    qb = pl.program_id(2)
    q4 = _stack_group_queries(q_ref[...])
    carry = None
    for d in range(WIN_TILES):
        kb = jnp.maximum(qb - d, 0)
        start = pl.multiple_of(kb * tq, tq)
        k = k_ref[pl.ds(start, tq), :]
        v = v_ref[pl.ds(start, tq), :]
        s = _dot_nt(q4, k) + bt_ref[d]
        if d == 0:
            carry = _softmax_first(s, v)
        else:
            carry = _softmax_step(carry, s + jnp.where(qb >= d, 0.0, NEG_INF), v)
    m, l, acc = carry
    o_ref[...] = _unstack_group_outputs(acc / l, tq).astype(o_ref.dtype)


def _win_attn(nq, kw2, vw2, bt):
    b, s, _ = nq.shape
    tq = NSA_TQ
    gw = NSA_GROUP * NSA_HD
    return pl.pallas_call(
        _win_attn_kernel,
        grid=(b, NSA_KV_HEADS, s // tq),
        in_specs=[pl.BlockSpec((None, tq, gw), lambda bi, g, qb: (bi, qb, g)),
                  pl.BlockSpec((None, s, LANES), lambda bi, g, qb: (bi, 0, g)),
                  pl.BlockSpec((None, s, LANES), lambda bi, g, qb: (bi, 0, g)),
                  pl.BlockSpec((None, WIN_TILES, NSA_GROUP * tq, tq), lambda bi, g, qb: (g, 0, 0, 0))],
        out_specs=pl.BlockSpec((None, tq, gw), lambda bi, g, qb: (bi, qb, g)),
        out_shape=jax.ShapeDtypeStruct((b, s, NSA_WIDTH), _bf16),
        compiler_params=_cparams(("parallel", "parallel", "arbitrary")),
        name="win_attn",
    )(nq, kw2, vw2, bt)


MIX_TM = 256


def _mix_out_kernel(x_ref, od_ref, oc_ref, os_ref, ow_ref, ng_ref, ge_ref, wo_ref, g_ref, wq_ref,
                    x1_ref, xn_ref, qp_ref):
    sig = jax.nn.sigmoid(ng_ref[...])
    hi, lo = _split_hi_lo(sig)
    o_nsa = jnp.zeros(oc_ref.shape, _f32)
    for br, o_ref in enumerate((oc_ref, os_ref, ow_ref)):
        gate = _dot(hi, ge_ref[br]) + _dot(lo, ge_ref[br])
        o_nsa = o_nsa + gate * o_ref[...].astype(_f32)
    y = _dot(od_ref[...], wo_ref[:DIFF_WIDTH, :]) + _dot(o_nsa.astype(_bf16), wo_ref[DIFF_WIDTH:, :])
    x1 = x_ref[...] + y
    x1_ref[...] = x1
    xn = x1 * lax.rsqrt(jnp.mean(x1 * x1, axis=-1, keepdims=True) + EPS) * g_ref[...]
    xn = xn.astype(_bf16)
    xn_ref[...] = xn
    qp_ref[...] = _dot(xn, wq_ref[...]).astype(qp_ref.dtype)


def _mix_out(x2d, o_diff, o_cmp, o_sel, o_win, ng, gate_expand, w_out, g_ffn, w_q):
    t = x2d.shape[0]
    tm = MIX_TM
    nq = w_q.shape[1]
    row = lambda w: pl.BlockSpec((tm, w), lambda i: (i, 0))
    full = lambda a: pl.BlockSpec(a.shape, lambda i: (0,) * a.ndim)
    g2 = g_ffn.reshape(1, D_MODEL)
    return pl.pallas_call(
        _mix_out_kernel,
        grid=(t // tm,),
        in_specs=[row(D_MODEL), row(DIFF_WIDTH), row(NSA_WIDTH), row(NSA_WIDTH), row(NSA_WIDTH),
                  row(LANES), full(gate_expand), full(w_out), full(g2), full(w_q)],
        out_specs=[row(D_MODEL), row(D_MODEL), row(nq)],
        out_shape=[jax.ShapeDtypeStruct((t, D_MODEL), _f32),
                   jax.ShapeDtypeStruct((t, D_MODEL), _bf16),
                   jax.ShapeDtypeStruct((t, nq), _bf16)],
        compiler_params=_cparams(("parallel",)),
        name="mix_out",
    )(x2d, o_diff, o_cmp, o_sel, o_win, ng, gate_expand, w_out, g2, w_q)


ROUTE_TM = 256
SLOTS = 2 * PEER_HEADS * PEER_TOPK


def _top16_rows(s, rows, n):
    vals, ids = [], []
    for _ in range(PEER_TOPK):
        m = jnp.max(s, axis=0, keepdims=True)
        idx = jnp.min(jnp.where(s == m, rows, n), axis=0, keepdims=True)
        vals.append(m)
        ids.append(idx)
        s = jnp.where(rows == idx, -jnp.inf, s)
    return jnp.concatenate(vals, axis=0), jnp.concatenate(ids, axis=0)


def _peer_route_kernel(q_ref, sk_ref, flat_ref, row_ref, gate_ref, rowt_ref, gatet_ref):
    tm = ROUTE_TM
    rows_k = lax.broadcasted_iota(jnp.int32, (PEER_NKEYS, tm), 0)
    flat_c = jnp.broadcast_to(flat_ref[...], (flat_ref.shape[0], tm))
    keep_c = flat_c >= 0
    half_d = PEER_DKEY // 2

    def head(h, _):
        col = pl.multiple_of(h * PEER_DKEY, PEER_DKEY)
        q0 = q_ref[:, pl.ds(col, half_d)]
        q1 = q_ref[:, pl.ds(col + half_d, half_d)]
        sv0, si0 = _top16_rows(_dot_nt(sk_ref[2 * h], q0), rows_k, PEER_NKEYS)
        sv1, si1 = _top16_rows(_dot_nt(sk_ref[2 * h + 1], q1), rows_k, PEER_NKEYS)
        slabs, ids = [], []
        for a, b0 in _PAIR_SLABS:
            if a is None:
                slabs.append(sv0[8:16] + sv1[0:1])
                ids.append(si0[8:16] * PEER_NKEYS + si1[0:1])
            else:
                slabs.append(sv0[a:a + 1] + sv1[b0:b0 + 8])
                ids.append(si0[a:a + 1] * PEER_NKEYS + si1[b0:b0 + 8])
        cand = jnp.where(keep_c, jnp.concatenate(slabs, axis=0), -jnp.inf)
        cidx = jnp.concatenate(ids, axis=0)
        cv, ci = _top16_rows(cand, flat_c, PEER_TOPK * PEER_TOPK)
        e = jnp.concatenate(
            [jnp.sum(jnp.where(flat_c == ci[k:k + 1], cidx, 0), axis=0, keepdims=True)
             for k in range(PEER_TOPK)], axis=0)
        ex = jnp.exp(cv - cv[0:1])
        gate = ex / jnp.sum(ex, axis=0, keepdims=True)
        upper = e >= HALF_EXPERTS
        base = pl.multiple_of(h * PEER_TOPK, PEER_TOPK)
        rowt_ref[pl.ds(base, PEER_TOPK), :] = (e & (HALF_EXPERTS - 1)) * PAIR_WORD_ROWS
        base2 = pl.multiple_of(h * 2 * PEER_TOPK, 2 * PEER_TOPK)
        gatet_ref[pl.ds(base2, PEER_TOPK), :] = jnp.where(upper, 0.0, gate)
        gatet_ref[pl.ds(base2 + PEER_TOPK, PEER_TOPK), :] = jnp.where(upper, gate, 0.0)
        return 0

    lax.fori_loop(0, PEER_HEADS, head, 0)
    row_ref[...] = rowt_ref[...].T
    gate_ref[...] = gatet_ref[...].T


def _pair_slabs():
    slabs, flat = [(0, 0), (0, 8)] + [(a, 0) for a in range(1, 8)] + [(None, 0)], []
    for a, b0 in slabs:
        for i in range(8):
            aa, bb = (8 + i, 0) if a is None else (a, b0 + i)
            flat.append(aa * PEER_TOPK + bb if (aa + 1) * (bb + 1) <= PEER_TOPK else -1)
    return tuple(slabs), np.asarray(flat, np.int32).reshape(-1, 1)


_PAIR_SLABS, _PAIR_FLAT = _pair_slabs()


def _peer_route(qp, sk):
    t = qp.shape[0]
    tm = ROUTE_TM
    npick = PEER_HEADS * PEER_TOPK
    return pl.pallas_call(
        _peer_route_kernel,
        grid=(t // tm,),
        in_specs=[pl.BlockSpec((tm, qp.shape[1]), lambda i: (i, 0)),
                  pl.BlockSpec(sk.shape, lambda i: (0, 0, 0)),
                  pl.BlockSpec(_PAIR_FLAT.shape, lambda i: (0, 0))],
        out_specs=[pl.BlockSpec((tm, npick), lambda i: (i, 0)),
                   pl.BlockSpec((tm, SLOTS), lambda i: (i, 0))],
        out_shape=[jax.ShapeDtypeStruct((t, npick), jnp.int32),
                   jax.ShapeDtypeStruct((t, SLOTS), _f32)],
        scratch_shapes=[pltpu.VMEM((npick, tm), jnp.int32), pltpu.VMEM((SLOTS, tm), _f32)],
        compiler_params=_cparams(("parallel",)),
        name="peer_route",
    )(qp, sk, jnp.asarray(_PAIR_FLAT))


PEER_TT = 64
PEER_UNROLL = 8
NPICK = PEER_HEADS * PEER_TOPK
CHUNK_ROWS = PEER_TOPK * PAIR_ROWS
STAGE_COLS = NPICK * PAIR_ROWS


def _pair_table(w):
    e = w.shape[0]
    w4 = w.astype(_bf16).reshape(2, e // 2, PAIR_WORD_ROWS, LANES)
    words = lax.bitcast_convert_type(jnp.transpose(w4, (1, 2, 3, 0)), jnp.int32)
    return words.reshape(e // 2 * PAIR_WORD_ROWS, LANES)


def _gather_head(tab_ref, tok_rows, c):
    tiles = []
    for k in range(PEER_TOPK):
        r = pl.multiple_of(tok_rows[c * PEER_TOPK + k], PAIR_WORD_ROWS)
        tiles.append(pltpu.bitcast(tab_ref[pl.ds(r, PAIR_WORD_ROWS), :], _bf16))
    return jnp.concatenate(tiles, axis=0)


def _peer_up_kernel(row_ref, x_ref, tab_ref, diag_ref, fold_ref, gsum_ref, h_ref, z_ref):
    def token(t, _):
        x8 = x_ref[t]
        tok_rows = row_ref.at[pl.ds(t * NPICK, NPICK)]
        for c in range(PEER_HEADS):
            rows = _gather_head(tab_ref, tok_rows, c)
            z_ref[t, :, c * CHUNK_ROWS:(c + 1) * CHUNK_ROWS] = _dot_nt(x8, rows)
        return 0

    lax.fori_loop(0, PEER_TT, token, 0, unroll=PEER_UNROLL)
    zm = (z_ref[...] * diag_ref[...][None]).reshape(PEER_TT * 8, STAGE_COLS)
    hi, lo = _split_hi_lo(zm)
    part = _dot(hi, fold_ref[...]) + _dot(lo, fold_ref[...])
    hi, lo = _split_hi_lo(part)
    h_ref[...] = _dot(gsum_ref[...], hi) + _dot(gsum_ref[...], lo)


def _peer_up(rows_flat, x8, tab, diag, fold, gsum):
    t = x8.shape[0]
    tt = PEER_TT
    full = lambda a: pl.BlockSpec(a.shape, lambda i: (0,) * a.ndim)
    return pl.pallas_call(
        _peer_up_kernel,
        grid=(t // tt,),
        in_specs=[pl.BlockSpec((tt * NPICK,), lambda i: (i,), memory_space=pltpu.SMEM),
                  pl.BlockSpec((tt, 8, LANES), lambda i: (i, 0, 0)),
                  pl.BlockSpec(tab.shape, lambda i: (0, 0), pipeline_mode=pl.Buffered(1)),
                  full(diag), full(fold), full(gsum)],
        out_specs=pl.BlockSpec((tt, SLOTS), lambda i: (i, 0)),
        out_shape=jax.ShapeDtypeStruct((t, SLOTS), _f32),
        scratch_shapes=[pltpu.VMEM((tt, 8, STAGE_COLS), _f32)],
        compiler_params=_cparams(("arbitrary",)),
        name="peer_up",
    )(rows_flat, x8, tab, diag, fold, gsum)


def _peer_down_kernel(row_ref, h_ref, gate_ref, tab_ref, ex_ref, diag_ref, y_ref, wexp_ref):
    w = _gelu_tanh(h_ref[...]) * gate_ref[...]
    hi, lo = _split_hi_lo(w)
    wexp_ref[...] = _dot(hi, ex_ref[...]) + _dot(lo, ex_ref[...])

    def token(t, _):
        a = jnp.broadcast_to(wexp_ref[pl.ds(t, 1), :], diag_ref.shape) * diag_ref[...]
        ahi = a.astype(_bf16).astype(_f32)
        a2 = jnp.concatenate([ahi, a - ahi], axis=0).astype(_bf16)
        acc = jnp.zeros((2 * 8, LANES), _f32)
        tok_rows = row_ref.at[pl.ds(t * NPICK, NPICK)]
        for c in range(PEER_HEADS):
            rows = _gather_head(tab_ref, tok_rows, c)
            acc = acc + _dot(a2[:, c * CHUNK_ROWS:(c + 1) * CHUNK_ROWS], rows)
        y_ref[t] = acc[:8] + acc[8:]
        return 0

    lax.fori_loop(0, PEER_TT, token, 0, unroll=PEER_UNROLL)


def _peer_down(rows_flat, h2, gate2, tab, expand, diag):
    t = h2.shape[0]
    tt = PEER_TT
    full = lambda a: pl.BlockSpec(a.shape, lambda i: (0,) * a.ndim)
    return pl.pallas_call(
        _peer_down_kernel,
        grid=(t // tt,),
        in_specs=[pl.BlockSpec((tt * NPICK,), lambda i: (i,), memory_space=pltpu.SMEM),
                  pl.BlockSpec((tt, SLOTS), lambda i: (i, 0)),
                  pl.BlockSpec((tt, SLOTS), lambda i: (i, 0)),
                  pl.BlockSpec(tab.shape, lambda i: (0, 0), pipeline_mode=pl.Buffered(1)),
                  full(expand), full(diag)],
        out_specs=pl.BlockSpec((tt, 8, LANES), lambda i: (i, 0, 0)),
        out_shape=jax.ShapeDtypeStruct((t, 8, LANES), _f32),
        scratch_shapes=[pltpu.VMEM((tt, STAGE_COLS), _f32)],
        compiler_params=_cparams(("arbitrary",)),
        name="peer_down",
    )(rows_flat, h2, gate2, tab, expand, diag)


def _peer_constants():
    col = np.arange(STAGE_COLS)
    p, r = col // PAIR_ROWS, col % PAIR_ROWS
    slot = (p // PEER_TOPK) * 2 * PEER_TOPK + (r % 2) * PEER_TOPK + p % PEER_TOPK
    diag = (r[None, :] // 2 == np.arange(8)[:, None]).astype(np.float32)
    fold = (slot[:, None] == np.arange(SLOTS)[None, :]).astype(np.float32)
    gsum = (np.arange(PEER_TT * 8)[None, :] // 8 == np.arange(PEER_TT)[:, None]).astype(np.float32)
    return (jnp.asarray(diag, _f32), jnp.asarray(fold, _bf16), jnp.asarray(fold.T, _bf16),
            jnp.asarray(gsum, _bf16))


FIN_TM = 512


def _final_norm_kernel(x_ref, y_ref, g_ref, o_ref):
    z = x_ref[...] + y_ref[...]
    o_ref[...] = z * lax.rsqrt(jnp.mean(z * z, axis=-1, keepdims=True) + EPS) * g_ref[...]


def _final_norm(x1, y, g):
    t = x1.shape[0]
    row = pl.BlockSpec((FIN_TM, D_MODEL), lambda i: (i, 0))
    return pl.pallas_call(
        _final_norm_kernel,
        grid=(t // FIN_TM,),
        in_specs=[row, row, pl.BlockSpec((1, D_MODEL), lambda i: (0, 0))],
        out_specs=row,
        out_shape=jax.ShapeDtypeStruct((t, D_MODEL), _f32),
        compiler_params=_cparams(("parallel",)),
        name="final_norm",
    )(x1, y, g.reshape(1, D_MODEL))


def _cmp_to_sel_counts(n_rows, n_sel):
    r = SEL_LEN // CMP_STRIDE
    c = CMP_LEN // CMP_STRIDE
    off = np.arange(n_rows)[:, None] - r * np.arange(n_sel)[None, :] + (c - 1)
    counts = np.zeros((n_rows, LANES), np.float32)
    for n in range(c):
        counts[:, :n_sel] += ((off - n >= 0) & (off - n < r)).astype(np.float32)
    counts[n_rows - 1] = 0.0
    return counts


def _attention(x2d, b, s, w_in, norm_mix_g, rel_bias, lam_rows, subln_g,
               cmp_pos_k, cmp_pos_v, cmp_w1_k, cmp_w2_k, cmp_w1_v, cmp_w2_v):
    dq, dk, dv, nq, kcvc, ks2, vs2, kw2, vw2, ng = _in_proj(x2d, norm_mix_g, _arrange_w_in(w_in))
    to3 = lambda a: a.reshape(b, s, a.shape[-1])

    bt_diff = _bias_tiles(rel_bias[:, :DIFF_HEADS], DIFF_TQ, DIFF_TQ, 2)
    o_diff = _diff_attn(to3(dq), to3(dk), to3(dv), bt_diff, lam_rows, subln_g)

    n = s // CMP_STRIDE
    a = kcvc.reshape(b, n, CMP_STRIDE, 2, NSA_KV_HEADS, NSA_HD)
    a = jnp.transpose(a, (0, 3, 4, 1, 2, 5)).reshape(b, 2, NSA_KV_HEADS, n, CMP_STRIDE * NSA_HD)
    pos = jnp.stack([cmp_pos_k, cmp_pos_v]).reshape(2, 2, 1, CMP_STRIDE * NSA_HD)
    w1 = jnp.stack([cmp_w1_k, cmp_w1_v]).astype(_bf16)
    w2 = jnp.stack([cmp_w2_k, cmp_w2_v])
    w2dup = jnp.concatenate([w2, w2], axis=-1).astype(_bf16)
    cmp_kv = _compress(a, pos, w1, w2dup)

    n_sel = s // SEL_LEN
    counts = jnp.asarray(_cmp_to_sel_counts(n, n_sel), _bf16)
    nq3 = to3(nq)
    o_cmp, sel = _cmp_attn(nq3, cmp_kv[:, 0], cmp_kv[:, 1], counts, n_sel)

    nsa_bias = rel_bias[:, DIFF_HEADS:]
    tq = NSA_TQ
    bt_sel = _bias_tiles(nsa_bias, tq, tq, 2).reshape(NSA_KV_HEADS, NSA_GROUP, 2, tq, tq)
    bt_sel = jnp.transpose(bt_sel, (0, 2, 1, 3, 4)).reshape(NSA_KV_HEADS, 2, NSA_GROUP * tq, tq)
    expand = jnp.asarray(np.arange(s)[None, :] // SEL_LEN == np.arange(LANES)[:, None], _bf16)
    o_sel = _sel_attn(nq3, to3(ks2), to3(vs2), sel, expand, bt_sel)

    bt_win = _bias_tiles(nsa_bias, tq, tq, WIN_TILES, window=WINDOW)
    bt_win = bt_win.reshape(NSA_KV_HEADS, NSA_GROUP, WIN_TILES, tq, tq)
    bt_win = jnp.transpose(bt_win, (0, 2, 1, 3, 4)).reshape(NSA_KV_HEADS, WIN_TILES, NSA_GROUP * tq, tq)
    o_win = _win_attn(nq3, to3(kw2), to3(vw2), bt_win)

    flat = lambda a3: a3.reshape(b * s, a3.shape[-1])
    return flat(o_diff), flat(o_cmp), flat(o_sel), flat(o_win), ng


def _gate_expand():
    ge = np.zeros((3, LANES, NSA_WIDTH), np.float32)
    for head in range(NSA_HEADS):
        for br in range(3):
            ge[br, 3 * head + br, head * NSA_HD:(head + 1) * NSA_HD] = 1.0
    return jnp.asarray(ge, _bf16)


def _peer(x1, xn, qp, peer_sub_keys, peer_u, peer_v):
    t = x1.shape[0]
    sk = peer_sub_keys.reshape(2 * PEER_HEADS, PEER_NKEYS, PEER_DKEY // 2).astype(_bf16)
    rows, gate2 = _peer_route(qp, sk)
    rows_flat = rows.reshape(t * NPICK)
    diag, fold, expand, gsum = _peer_constants()
    h2 = _peer_up(rows_flat, xn.reshape(t, 8, LANES), _pair_table(peer_u), diag, fold, gsum)
    y = _peer_down(rows_flat, h2, gate2, _pair_table(peer_v), expand, diag)
    return y.reshape(t, D_MODEL)


def kernel(x, w_in, w_out, norm_mix_g, norm_ffn_g, final_norm_g, rel_bias, diff_lq1, diff_lk1, diff_lq2, diff_lk2, diff_subln_g, cmp_pos_k, cmp_pos_v, cmp_w1_k, cmp_w2_k, cmp_w1_v, cmp_w2_v, peer_w_q, peer_sub_keys, peer_u, peer_v):
    b, s, d = x.shape
    x2d = x.reshape(b * s, d)
    lam_rows = jnp.zeros((8, LANES), _f32).at[0:4, :DIFF_HD].set(
        jnp.stack([diff_lq1[0], diff_lk1[0], diff_lq2[0], diff_lk2[0]]))
    o_diff, o_cmp, o_sel, o_win, ng = _attention(
        x2d, b, s, w_in[0], norm_mix_g[0], rel_bias, lam_rows, diff_subln_g[0],
        cmp_pos_k[0], cmp_pos_v[0], cmp_w1_k[0], cmp_w2_k[0], cmp_w1_v[0], cmp_w2_v[0])
    x1, xn, qp = _mix_out(x2d, o_diff, o_cmp, o_sel, o_win, ng, _gate_expand(),
                          w_out[0].astype(_bf16), norm_ffn_g[0], peer_w_q[0].astype(_bf16))
    y = _peer(x1, xn, qp, peer_sub_keys[0], peer_u[0], peer_v[0])
    return _final_norm(x1, y, final_norm_g).reshape(b, s, d)
```

```python
import functools
import math

import numpy as np
import jax
import jax.numpy as jnp
from jax import lax
from jax.experimental import pallas as pl
from jax.experimental.pallas import tpu as pltpu

D_MODEL = 1024
DIFF_HEADS = 4
DIFF_HD = 64
DIFF_VD = 2 * DIFF_HD
DIFF_WIDTH = DIFF_HEADS * DIFF_VD
NSA_HEADS = 8
NSA_KV_HEADS = 2
NSA_GROUP = NSA_HEADS // NSA_KV_HEADS
NSA_HD = 64
NSA_WIDTH = NSA_HEADS * NSA_HD
CMP_LEN = 32
CMP_STRIDE = 16
CMP_HIDDEN = 256
SEL_LEN = 64
SEL_TOPN = 8
WINDOW = 512
FORCE_BONUS = 1000.0
NEG_INF = -1e30
N_BUCKETS = 32
MAX_DISTANCE = 128
PEER_HEADS = 8
PEER_NKEYS = 128
PEER_EXPERTS = PEER_NKEYS ** 2
PEER_DKEY = 256
PEER_TOPK = 16
EPS = 1e-6
LAMBDA_INIT = 0.8 - 0.6 * math.exp(-0.3 * 0)
LOG2E = math.log2(math.e)

KV_COLS = NSA_KV_HEADS * NSA_HD
SPLIT_SIZES = (DIFF_HEADS * 2 * DIFF_HD, DIFF_HEADS * 2 * DIFF_HD, DIFF_WIDTH, NSA_WIDTH,
               KV_COLS, KV_COLS, KV_COLS, KV_COLS, KV_COLS, KV_COLS, 3 * NSA_HEADS)
SPLIT_OFF = tuple(int(v) for v in np.concatenate([[0], np.cumsum(SPLIT_SIZES)]))

LANES = 128
VMEM_LIMIT = 56 * 1024 * 1024
HALF_EXPERTS = PEER_EXPERTS // 2
PAIR_ROWS = 16
PAIR_WORD_ROWS = PAIR_ROWS // 2

_f32 = jnp.float32
_bf16 = jnp.bfloat16


def _cparams(sem):
    return pltpu.CompilerParams(dimension_semantics=sem, vmem_limit_bytes=VMEM_LIMIT)


def _dot(a, b):
    return jnp.dot(a, b, preferred_element_type=_f32)


def _dot_nt(a, b):
    return lax.dot_general(a, b, (((1,), (1,)), ((), ())), preferred_element_type=_f32)


def _split_hi_lo(x):
    hi = x.astype(_bf16)
    lo = (x - hi.astype(_f32)).astype(_bf16)
    return hi, lo


def _gelu_tanh(x):
    c = math.sqrt(2.0 / math.pi)
    return 0.5 * x * (1.0 + jnp.tanh(c * (x + 0.044715 * (x * x * x))))


IN_TM = 512
_IN_GROUPS = (("dq", 512, _bf16), ("dv", 512, _bf16), ("nq", 512, _bf16), ("kcvc", 256, _f32),
              ("vs2", 256, _bf16), ("vw2", 256, _bf16), ("ng", 128, _f32))
_IN_KEYS_T = (("dkT", 512), ("ksT", 256), ("kwT", 256))


def _in_proj_kernel(x_ref, g_ref, w_ref, wt_ref, *out_refs):
    x = x_ref[...]
    y = x * lax.rsqrt(jnp.mean(x * x, axis=-1, keepdims=True) + EPS)
    h = (y * g_ref[...]).astype(_bf16)
    off = 0
    for (_, width, dt), o_ref in zip(_IN_GROUPS, out_refs):
        o_ref[...] = _dot(h, w_ref[:, off:off + width]).astype(dt)
        off += width
    off = 0
    for (_, width), o_ref in zip(_IN_KEYS_T, out_refs[len(_IN_GROUPS):]):
        o_ref[...] = _dot_nt(wt_ref[off:off + width, :], h).astype(o_ref.dtype)
        off += width


def _arrange_w_in(w_in):
    o = SPLIT_OFF
    sl = lambda i: w_in[:, o[i]:o[i + 1]]
    dup = lambda w: jnp.concatenate([w[:, :64], w[:, :64], w[:, 64:], w[:, 64:]], axis=1)
    ng = jnp.pad(sl(10), ((0, 0), (0, LANES - 3 * NSA_HEADS)))
    cols = [sl(0) * (DIFF_HD ** -0.5 * LOG2E), sl(2), sl(3) * (NSA_HD ** -0.5 * LOG2E),
            sl(4), sl(5), dup(sl(7)), dup(sl(9)), ng]
    keys = [sl(1), dup(sl(6)), dup(sl(8))]
    return (jnp.concatenate(cols, axis=1).astype(_bf16),
            jnp.concatenate(keys, axis=1).T.astype(_bf16))


def _in_proj(x2d, g, w_arr, wt_arr):
    t = x2d.shape[0]
    out_shape = ([jax.ShapeDtypeStruct((t, wd), dt) for _, wd, dt in _IN_GROUPS]
                 + [jax.ShapeDtypeStruct((wd, t), _bf16) for _, wd in _IN_KEYS_T])
    out_specs = ([pl.BlockSpec((IN_TM, wd), lambda i: (i, 0)) for _, wd, _ in _IN_GROUPS]
                 + [pl.BlockSpec((wd, IN_TM), lambda i: (0, i)) for _, wd in _IN_KEYS_T])
    return pl.pallas_call(
        _in_proj_kernel,
        grid=(t // IN_TM,),
        in_specs=[pl.BlockSpec((IN_TM, D_MODEL), lambda i: (i, 0)),
                  pl.BlockSpec((1, D_MODEL), lambda i: (0, 0)),
                  pl.BlockSpec(w_arr.shape, lambda i: (0, 0)),
                  pl.BlockSpec(wt_arr.shape, lambda i: (0, 0))],
        out_specs=out_specs,
        out_shape=out_shape,
        compiler_params=_cparams(("parallel",)),
        name="in_proj",
    )(x2d, g.reshape(1, D_MODEL), w_arr, wt_arr)


def _bucket_table(n):
    rel = np.arange(n)
    max_exact = N_BUCKETS // 2
    nf = np.maximum(rel, 1).astype(np.float32)
    large = max_exact + (np.log(nf / np.float32(max_exact)) / np.float32(math.log(MAX_DISTANCE / max_exact))
                         * np.float32(N_BUCKETS - max_exact)).astype(np.int32)
    large = np.minimum(large, N_BUCKETS - 1)
    return np.where(rel < max_exact, rel, large).astype(np.int32)


def _bias_tiles(rel_bias_heads, tq, tk, n_tiles, window=None):
    tab = (rel_bias_heads.astype(_f32) - rel_bias_heads[N_BUCKETS - 1][None, :].astype(_f32)) * LOG2E
    period = tq + tk
    m = np.arange(period)
    off = np.where(m < tk, -m, period - m)
    rel_w = np.arange(n_tiles)[:, None] * tq + off[None, :]
    bucket = _bucket_table(n_tiles * tq + period)[np.clip(rel_w, 0, None)]
    w = jnp.transpose(tab[bucket], (2, 0, 1))
    tiles = jnp.tile(w, (1, 1, tq))[..., :tq * (period - 1)]
    tiles = tiles.reshape(w.shape[0], n_tiles, tq, period - 1)[..., :tk]
    rel = (np.arange(n_tiles)[:, None, None] * tq + np.arange(tq)[None, :, None]
           - np.arange(tk)[None, None, :])
    ok = rel >= 0
    if window is not None:
        ok &= rel < window
    return jnp.where(jnp.asarray(ok)[None], tiles, NEG_INF)


def _attend_first(s, v_aug):
    m = jnp.max(s, axis=-1, keepdims=True)
    return m, _dot(jnp.exp2(s - m).astype(_bf16), v_aug)


def _attend_step(carry, s, v_aug):
    m, acc = carry
    m_new = jnp.maximum(m, jnp.max(s, axis=-1, keepdims=True))
    acc = jnp.exp2(m - m_new) * acc + _dot(jnp.exp2(s - m_new).astype(_bf16), v_aug)
    return m_new, acc


def _lane_half_mask(shape):
    return lax.broadcasted_iota(jnp.int32, shape, len(shape) - 1) < (LANES // 2)


DIFF_TQ = 256


def _diff_attn_kernel(q_ref, k_ref, v_ref, bt_ref, lam_ref, g_ref, o_ref):
    tq = DIFF_TQ
    qb = pl.program_id(2)
    q = q_ref[...]
    lo = _lane_half_mask(q.shape)
    zero = jnp.zeros_like(q)
    q2 = jnp.concatenate([jnp.where(lo, q, zero), jnp.where(lo, zero, q)], axis=0)

    def kv(start, n):
        start = pl.multiple_of(start, tq)
        v = v_ref[pl.ds(start, n), :]
        return k_ref[:, pl.ds(start, n)], jnp.concatenate([v, jnp.ones_like(v)], axis=1)

    def both(b):
        return jnp.concatenate([b, b], axis=0)

    k, v = kv(qb * tq, tq)
    carry = _attend_first(_dot(q2, k) + both(bt_ref[0]), v)
    k, v = kv(jnp.maximum(qb - 1, 0) * tq, tq)
    carry = _attend_step(carry, _dot(q2, k) + both(bt_ref[1] + jnp.where(qb >= 1, 0.0, NEG_INF)), v)

    n_far = jnp.maximum(qb - 1, 0)

    def odd(c):
        k, v = kv((n_far - 1) * tq, tq)
        return _attend_step(c, _dot(q2, k), v)

    carry = lax.cond(n_far % 2 == 1, odd, lambda c: c, carry)

    def pair(i, c):
        k, v = kv(i * 2 * tq, 2 * tq)
        return _attend_step(c, _dot(q2, k), v)

    m, acc = lax.fori_loop(0, n_far // 2, pair, carry)
    o2 = acc[:, :DIFF_VD] / acc[:, DIFF_VD:]
    lv = lam_ref[...]
    lam = (jnp.exp(jnp.sum(lv[0:1] * lv[1:2], axis=-1, keepdims=True))
           - jnp.exp(jnp.sum(lv[2:3] * lv[3:4], axis=-1, keepdims=True)) + LAMBDA_INIT)
    o = o2[:tq] - lam * o2[tq:]
    y = o * lax.rsqrt(jnp.mean(o * o, axis=-1, keepdims=True) + EPS)
    o_ref[...] = (y * g_ref[...] * (1.0 - LAMBDA_INIT)).astype(o_ref.dtype)


def _diff_attn(dq, dkT, dv, bt, lam_rows, subln_g):
    b, s, _ = dq.shape
    tq = DIFF_TQ
    return pl.pallas_call(
        _diff_attn_kernel,
        grid=(b, DIFF_HEADS, s // tq),
        in_specs=[pl.BlockSpec((None, tq, LANES), lambda bi, h, qb: (bi, qb, h)),
                  pl.BlockSpec((LANES, s), lambda bi, h, qb: (h, bi)),
                  pl.BlockSpec((None, s, LANES), lambda bi, h, qb: (bi, 0, h)),
                  pl.BlockSpec((None, 2, tq, tq), lambda bi, h, qb: (h, 0, 0, 0)),
                  pl.BlockSpec((8, LANES), lambda bi, h, qb: (0, 0)),
                  pl.BlockSpec((1, LANES), lambda bi, h, qb: (0, 0))],
        out_specs=pl.BlockSpec((None, tq, LANES), lambda bi, h, qb: (bi, qb, h)),
        out_shape=jax.ShapeDtypeStruct((b, s, DIFF_WIDTH), _bf16),
        compiler_params=_cparams(("parallel", "parallel", "arbitrary")),
        name="diff_attn",
    )(dq, dkT, dv, bt, lam_rows, subln_g.reshape(1, LANES))


def _compress_kernel(a_ref, pos_ref, w1_ref, w2_ref, o_ref):
    a = a_ref[...]
    half = (CMP_LEN // 2) * NSA_HD
    za = _dot((a + pos_ref[0]).astype(_bf16), w1_ref[:half, :])
    zb = _dot((a + pos_ref[1]).astype(_bf16), w1_ref[half:, :])
    n = a.shape[0]
    hid = za + pltpu.roll(zb, n - 1, 0)
    o_ref[...] = _dot(_gelu_tanh(hid).astype(_bf16), w2_ref[...]).astype(o_ref.dtype)


def _compress(a, pos, w1, w2dup):
    b, _, _, n, width = a.shape
    return pl.pallas_call(
        _compress_kernel,
        grid=(b, 2, NSA_KV_HEADS),
        in_specs=[pl.BlockSpec((None, None, None, n, width), lambda bi, kv, g: (bi, kv, g, 0, 0)),
                  pl.BlockSpec((None, 2, 1, width), lambda bi, kv, g: (kv, 0, 0, 0)),
                  pl.BlockSpec((None, 2 * width, CMP_HIDDEN), lambda bi, kv, g: (kv, 0, 0)),
                  pl.BlockSpec((None, CMP_HIDDEN, LANES), lambda bi, kv, g: (kv, 0, 0))],
        out_specs=pl.BlockSpec((None, None, None, n, LANES), lambda bi, kv, g: (bi, kv, g, 0, 0)),
        out_shape=jax.ShapeDtypeStruct((b, 2, NSA_KV_HEADS, n, LANES), _bf16),
        compiler_params=_cparams(("parallel", "parallel", "parallel")),
        name="compress",
    )(a, pos, w1, w2dup)


NSA_TQ = 128


GROUP_ORDER = (0, 2, 1, 3)


def _stack_group_queries(q):
    rows = []
    for j in GROUP_ORDER:
        blk = q[:, (j // 2) * LANES:(j // 2 + 1) * LANES]
        lo = _lane_half_mask(blk.shape)
        keep = lo if j % 2 == 0 else jnp.logical_not(lo)
        rows.append(jnp.where(keep, blk, jnp.zeros_like(blk)))
    return jnp.concatenate(rows, axis=0)


def _unstack_group_outputs(o4, tq):
    lo = _lane_half_mask((tq, LANES))
    half = NSA_GROUP // 2
    pairs = [jnp.where(lo, o4[i * tq:(i + 1) * tq], o4[(half + i) * tq:(half + i + 1) * tq])
             for i in range(half)]
    return jnp.concatenate(pairs, axis=1)


def _group_values(v):
    lo = _lane_half_mask(v.shape)
    one = jnp.ones_like(v)
    return jnp.where(lo, v, one), jnp.where(lo, one, v)


def _group_pv(p, v_pair):
    n = p.shape[0] // 2
    p = p.astype(_bf16)
    return jnp.concatenate([_dot(p[:n], v_pair[0]), _dot(p[n:], v_pair[1])], axis=0)


def _group_attend_first(s, v_pair):
    m = jnp.max(s, axis=-1, keepdims=True)
    return m, _group_pv(jnp.exp2(s - m), v_pair)


def _group_attend_step(carry, s, v_pair):
    m, acc = carry
    m_new = jnp.maximum(m, jnp.max(s, axis=-1, keepdims=True))
    return m_new, jnp.exp2(m - m_new) * acc + _group_pv(jnp.exp2(s - m_new), v_pair)


def _group_normalise(acc):
    return acc / pltpu.roll(acc, LANES // 2, 1)


def _cmp_attn_kernel(q_ref, kc_ref, vc_ref, cs_ref, o_ref, sel_ref, *, n_sel):
    tq = NSA_TQ
    qb = pl.program_id(2)
    q4 = _stack_group_queries(q_ref[...])
    s = _dot_nt(q4, kc_ref[...])
    row = lax.broadcasted_iota(jnp.int32, s.shape, 0)
    pos4 = qb * tq + (row & (tq - 1))
    c = lax.broadcasted_iota(jnp.int32, s.shape, 1)
    ok = (c * CMP_STRIDE + (CMP_LEN - 1)) <= pos4
    s = jnp.where(ok, s, NEG_INF)
    m = jnp.max(s, axis=-1, keepdims=True)
    e = jnp.where(ok, jnp.exp2(s - m), 0.0)
    l = jnp.sum(e, axis=-1, keepdims=True)
    p = e / jnp.where(l > 0.0, l, 1.0)
    o_ref[...] = _unstack_group_outputs(_dot(p.astype(_bf16), vc_ref[...]), tq).astype(o_ref.dtype)

    psum = p[0:tq] + p[tq:2 * tq] + p[2 * tq:3 * tq] + p[3 * tq:4 * tq]
    hi, lo = _split_hi_lo(psum)
    imp = _dot(hi, cs_ref[...]) + _dot(lo, cs_ref[...])
    blk = lax.broadcasted_iota(jnp.int32, imp.shape, 1)
    pos = qb * tq + lax.broadcasted_iota(jnp.int32, imp.shape, 0)
    cur = pos // SEL_LEN
    forced = (blk == 0) | (blk == cur) | (blk == cur - 1)
    score = jnp.where(blk <= cur, imp + jnp.where(forced, FORCE_BONUS, 0.0), NEG_INF)
    rank = jnp.zeros(score.shape, jnp.int32)
    for i in range(n_sel):
        si = jnp.max(jnp.where(blk == i, score, -jnp.inf), axis=-1, keepdims=True)
        beats = (si > score) | ((si == score) & (i < blk))
        rank = rank + beats.astype(jnp.int32)
    sel = (rank < SEL_TOPN) & (score > 0.5 * NEG_INF)
    sel_ref[...] = sel.astype(sel_ref.dtype)


def _cmp_attn(nq, kcmp, vcmp, cmp_sel, n_sel):
    b, s, _ = nq.shape
    tq = NSA_TQ
    n = kcmp.shape[-2]
    gw = NSA_GROUP * NSA_HD
    return pl.pallas_call(
        functools.partial(_cmp_attn_kernel, n_sel=n_sel),
        grid=(b, NSA_KV_HEADS, s // tq),
        in_specs=[pl.BlockSpec((None, tq, gw), lambda bi, g, qb: (bi, qb, g)),
                  pl.BlockSpec((None, None, n, LANES), lambda bi, g, qb: (bi, g, 0, 0)),
                  pl.BlockSpec((None, None, n, LANES), lambda bi, g, qb: (bi, g, 0, 0)),
                  pl.BlockSpec((n, LANES), lambda bi, g, qb: (0, 0))],
        out_specs=[pl.BlockSpec((None, tq, gw), lambda bi, g, qb: (bi, qb, g)),
                   pl.BlockSpec((None, None, tq, LANES), lambda bi, g, qb: (bi, g, qb, 0))],
        out_shape=[jax.ShapeDtypeStruct((b, s, NSA_WIDTH), _bf16),
                   jax.ShapeDtypeStruct((b, NSA_KV_HEADS, s, LANES), _bf16)],
        compiler_params=_cparams(("parallel", "parallel", "arbitrary")),
        name="cmp_attn",
    )(nq, kcmp, vcmp, cmp_sel)


SEL_CHUNK = 4 * NSA_TQ


def _sel_attn_kernel(q_ref, k_ref, v_ref, sel_ref, ex_ref, bt_ref, o_ref):
    tq = NSA_TQ
    qb = pl.program_id(2)
    q4 = _stack_group_queries(q_ref[...])
    sel = sel_ref[...]

    def scores(pen_rows, start, n):
        start = pl.multiple_of(start, tq)
        kx = jnp.concatenate([k_ref[:, pl.ds(start, n)], ex_ref[:, pl.ds(start, n)]], axis=0)
        qx = jnp.concatenate([q4, jnp.concatenate([pen_rows] * NSA_GROUP, axis=0)], axis=1)
        return _dot(qx, kx), _group_values(v_ref[pl.ds(start, n), :])

    def penalty(sel_rows):
        return ((sel_rows.astype(_f32) - 1.0) * (-NEG_INF)).astype(_bf16)

    s, v = scores(penalty(sel), qb * tq, tq)
    carry = _group_attend_first(s + bt_ref[0], v)
    s, v = scores(penalty(sel), jnp.maximum(qb - 1, 0) * tq, tq)
    carry = _group_attend_step(carry, s + bt_ref[1] + jnp.where(qb >= 1, 0.0, NEG_INF), v)

    far_len = jnp.maximum(qb - 1, 0) * tq
    blk = lax.broadcasted_iota(jnp.int32, sel.shape, 1)
    pen_far = penalty(jnp.where(blk * SEL_LEN < far_len, sel, jnp.zeros_like(sel)))

    def far(i, c):
        s, v = scores(pen_far, i * SEL_CHUNK, SEL_CHUNK)
        return _group_attend_step(c, s, v)

    m, acc = lax.fori_loop(0, (far_len + SEL_CHUNK - 1) // SEL_CHUNK, far, carry)
    o_ref[...] = _unstack_group_outputs(_group_normalise(acc), tq).astype(o_ref.dtype)


def _sel_attn(nq, ks2, vs2, sel, expand, bt):
    b, s, _ = nq.shape
    tq = NSA_TQ
    gw = NSA_GROUP * NSA_HD
    return pl.pallas_call(
        _sel_attn_kernel,
        grid=(b, NSA_KV_HEADS, s // tq),
        in_specs=[pl.BlockSpec((None, tq, gw), lambda bi, g, qb: (bi, qb, g)),
                  pl.BlockSpec((LANES, s), lambda bi, g, qb: (g, bi)),
                  pl.BlockSpec((None, s, LANES), lambda bi, g, qb: (bi, 0, g)),
                  pl.BlockSpec((None, None, tq, LANES), lambda bi, g, qb: (bi, g, qb, 0)),
                  pl.BlockSpec((LANES, s), lambda bi, g, qb: (0, 0)),
                  pl.BlockSpec((None, 2, NSA_GROUP * tq, tq), lambda bi, g, qb: (g, 0, 0, 0))],
        out_specs=pl.BlockSpec((None, tq, gw), lambda bi, g, qb: (bi, qb, g)),
        out_shape=jax.ShapeDtypeStruct((b, s, NSA_WIDTH), _bf16),
        compiler_params=_cparams(("parallel", "parallel", "arbitrary")),
        name="sel_attn",
    )(nq, ks2, vs2, sel, expand, bt)


WIN_TILES = WINDOW // NSA_TQ + 1


def _win_attn_kernel(q_ref, k_ref, v_ref, bt_ref, o_ref):
    tq = NSA_TQ
    qb = pl.program_id(2)
    q4 = _stack_group_queries(q_ref[...])
    s_blocks, va, vb = [], [], []
    for d in range(WIN_TILES):
        start = pl.multiple_of(jnp.maximum(qb - d, 0) * tq, tq)
        s = _dot(q4, k_ref[:, pl.ds(start, tq)])
        s_blocks.append(s if d == 0 else s + jnp.where(qb >= d, 0.0, NEG_INF))
        a, b = _group_values(v_ref[pl.ds(start, tq), :])
        va.append(a)
        vb.append(b)
    s = jnp.concatenate(s_blocks, axis=1) + bt_ref[...]
    m, acc = _group_attend_first(s, (jnp.concatenate(va, axis=0), jnp.concatenate(vb, axis=0)))
    o_ref[...] = _unstack_group_outputs(_group_normalise(acc), tq).astype(o_ref.dtype)


def _win_attn(nq, kw2, vw2, bt):
    b, s, _ = nq.shape
    tq = NSA_TQ
    gw = NSA_GROUP * NSA_HD
    return pl.pallas_call(
        _win_attn_kernel,
        grid=(b, NSA_KV_HEADS, s // tq),
        in_specs=[pl.BlockSpec((None, tq, gw), lambda bi, g, qb: (bi, qb, g)),
                  pl.BlockSpec((LANES, s), lambda bi, g, qb: (g, bi)),
                  pl.BlockSpec((None, s, LANES), lambda bi, g, qb: (bi, 0, g)),
                  pl.BlockSpec((None, NSA_GROUP * tq, WIN_TILES * tq), lambda bi, g, qb: (g, 0, 0))],
        out_specs=pl.BlockSpec((None, tq, gw), lambda bi, g, qb: (bi, qb, g)),
        out_shape=jax.ShapeDtypeStruct((b, s, NSA_WIDTH), _bf16),
        compiler_params=_cparams(("parallel", "parallel", "arbitrary")),
        name="win_attn",
    )(nq, kw2, vw2, bt)


MIX_TM = 256


def _mix_out_kernel(x_ref, od_ref, oc_ref, os_ref, ow_ref, ng_ref, ge_ref, wo_ref, g_ref, wq_ref,
                    x1_ref, xn_ref, qp_ref):
    sig = jax.nn.sigmoid(ng_ref[...])
    hi, lo = _split_hi_lo(sig)
    o_nsa = jnp.zeros(oc_ref.shape, _f32)
    for br, o_ref in enumerate((oc_ref, os_ref, ow_ref)):
        gate = _dot(hi, ge_ref[br]) + _dot(lo, ge_ref[br])
        o_nsa = o_nsa + gate * o_ref[...].astype(_f32)
    y = _dot(od_ref[...], wo_ref[:DIFF_WIDTH, :]) + _dot(o_nsa.astype(_bf16), wo_ref[DIFF_WIDTH:, :])
    x1 = x_ref[...] + y
    x1_ref[...] = x1
    xn = x1 * lax.rsqrt(jnp.mean(x1 * x1, axis=-1, keepdims=True) + EPS) * g_ref[...]
    xn = xn.astype(_bf16)
    xn_ref[...] = xn
    qp_ref[...] = _dot(xn, wq_ref[...]).astype(qp_ref.dtype)


def _mix_out(x2d, o_diff, o_cmp, o_sel, o_win, ng, gate_expand, w_out, g_ffn, w_q):
    t = x2d.shape[0]
    tm = MIX_TM
    nq = w_q.shape[1]
    row = lambda w: pl.BlockSpec((tm, w), lambda i: (i, 0))
    full = lambda a: pl.BlockSpec(a.shape, lambda i: (0,) * a.ndim)
    g2 = g_ffn.reshape(1, D_MODEL)
    return pl.pallas_call(
        _mix_out_kernel,
        grid=(t // tm,),
        in_specs=[row(D_MODEL), row(DIFF_WIDTH), row(NSA_WIDTH), row(NSA_WIDTH), row(NSA_WIDTH),
                  row(LANES), full(gate_expand), full(w_out), full(g2), full(w_q)],
        out_specs=[row(D_MODEL), row(D_MODEL), row(nq)],
        out_shape=[jax.ShapeDtypeStruct((t, D_MODEL), _f32),
                   jax.ShapeDtypeStruct((t, D_MODEL), _bf16),
                   jax.ShapeDtypeStruct((t, nq), _bf16)],
        compiler_params=_cparams(("parallel",)),
        name="mix_out",
    )(x2d, o_diff, o_cmp, o_sel, o_win, ng, gate_expand, w_out, g2, w_q)


ROUTE_TM = 256
SLOTS = 2 * PEER_HEADS * PEER_TOPK


def _top16_rows(s, rows, n):
    vals, ids = [], []
    for _ in range(PEER_TOPK):
        m = jnp.max(s, axis=0, keepdims=True)
        idx = jnp.min(jnp.where(s == m, rows, n), axis=0, keepdims=True)
        vals.append(m)
        ids.append(idx)
        s = jnp.where(rows == idx, -jnp.inf, s)
    return jnp.concatenate(vals, axis=0), jnp.concatenate(ids, axis=0)


def _peer_route_kernel(q_ref, sk_ref, flat_ref, row_ref, gate_ref, rowt_ref, gatet_ref):
    tm = ROUTE_TM
    rows_k = lax.broadcasted_iota(jnp.int32, (PEER_NKEYS, tm), 0)
    flat_c = jnp.broadcast_to(flat_ref[...], (flat_ref.shape[0], tm))
    keep_c = flat_c >= 0
    half_d = PEER_DKEY // 2

    def head(h, _):
        col = pl.multiple_of(h * PEER_DKEY, PEER_DKEY)
        q0 = q_ref[:, pl.ds(col, half_d)]
        q1 = q_ref[:, pl.ds(col + half_d, half_d)]
        sv0, si0 = _top16_rows(_dot_nt(sk_ref[2 * h], q0), rows_k, PEER_NKEYS)
        sv1, si1 = _top16_rows(_dot_nt(sk_ref[2 * h + 1], q1), rows_k, PEER_NKEYS)
        slabs, ids = [], []
        for a, b0 in _PAIR_SLABS:
            if a is None:
                slabs.append(sv0[8:16] + sv1[0:1])
                ids.append(si0[8:16] * PEER_NKEYS + si1[0:1])
            else:
                slabs.append(sv0[a:a + 1] + sv1[b0:b0 + 8])
                ids.append(si0[a:a + 1] * PEER_NKEYS + si1[b0:b0 + 8])
        cand = jnp.where(keep_c, jnp.concatenate(slabs, axis=0), -jnp.inf)
        cidx = jnp.concatenate(ids, axis=0)
        cv, ci = _top16_rows(cand, flat_c, PEER_TOPK * PEER_TOPK)
        e = jnp.concatenate(
            [jnp.sum(jnp.where(flat_c == ci[k:k + 1], cidx, 0), axis=0, keepdims=True)
             for k in range(PEER_TOPK)], axis=0)
        ex = jnp.exp(cv - cv[0:1])
        gate = ex / jnp.sum(ex, axis=0, keepdims=True)
        upper = e >= HALF_EXPERTS
        base = pl.multiple_of(h * PEER_TOPK, PEER_TOPK)
        rowt_ref[pl.ds(base, PEER_TOPK), :] = (e & (HALF_EXPERTS - 1)) * PAIR_WORD_ROWS
        base2 = pl.multiple_of(h * 2 * PEER_TOPK, 2 * PEER_TOPK)
        gatet_ref[pl.ds(base2, PEER_TOPK), :] = jnp.where(upper, 0.0, gate)
        gatet_ref[pl.ds(base2 + PEER_TOPK, PEER_TOPK), :] = jnp.where(upper, gate, 0.0)
        return 0

    lax.fori_loop(0, PEER_HEADS, head, 0)
    row_ref[...] = rowt_ref[...].T
    gate_ref[...] = gatet_ref[...].T


def _pair_slabs():
    slabs, flat = [(0, 0), (0, 8)] + [(a, 0) for a in range(1, 8)] + [(None, 0)], []
    for a, b0 in slabs:
        for i in range(8):
            aa, bb = (8 + i, 0) if a is None else (a, b0 + i)
            flat.append(aa * PEER_TOPK + bb if (aa + 1) * (bb + 1) <= PEER_TOPK else -1)
    return tuple(slabs), np.asarray(flat, np.int32).reshape(-1, 1)


_PAIR_SLABS, _PAIR_FLAT = _pair_slabs()


def _peer_route(qp, sk):
    t = qp.shape[0]
    tm = ROUTE_TM
    npick = PEER_HEADS * PEER_TOPK
    return pl.pallas_call(
        _peer_route_kernel,
        grid=(t // tm,),
        in_specs=[pl.BlockSpec((tm, qp.shape[1]), lambda i: (i, 0)),
                  pl.BlockSpec(sk.shape, lambda i: (0, 0, 0)),
                  pl.BlockSpec(_PAIR_FLAT.shape, lambda i: (0, 0))],
        out_specs=[pl.BlockSpec((tm, npick), lambda i: (i, 0)),
                   pl.BlockSpec((tm, SLOTS), lambda i: (i, 0))],
        out_shape=[jax.ShapeDtypeStruct((t, npick), jnp.int32),
                   jax.ShapeDtypeStruct((t, SLOTS), _f32)],
        scratch_shapes=[pltpu.VMEM((npick, tm), jnp.int32), pltpu.VMEM((SLOTS, tm), _f32)],
        compiler_params=_cparams(("parallel",)),
        name="peer_route",
    )(qp, sk, jnp.asarray(_PAIR_FLAT))


PEER_TT = 64
PEER_UNROLL = 8
NPICK = PEER_HEADS * PEER_TOPK
CHUNK_ROWS = PEER_TOPK * PAIR_ROWS
STAGE_COLS = NPICK * PAIR_ROWS


def _pair_table(w):
    e = w.shape[0]
    w4 = w.astype(_bf16).reshape(2, e // 2, PAIR_WORD_ROWS, LANES)
    words = lax.bitcast_convert_type(jnp.transpose(w4, (1, 2, 3, 0)), jnp.int32)
    return words.reshape(e // 2 * PAIR_WORD_ROWS, LANES)


def _gather_head(tab_ref, tok_rows, c):
    tiles = []
    for k in range(PEER_TOPK):
        r = pl.multiple_of(tok_rows[c * PEER_TOPK + k], PAIR_WORD_ROWS)
        tiles.append(pltpu.bitcast(tab_ref[pl.ds(r, PAIR_WORD_ROWS), :], _bf16))
    return jnp.concatenate(tiles, axis=0)


def _peer_up_kernel(row_ref, x_ref, tab_ref, diag_ref, fold_ref, gsum_ref, h_ref, z_ref):
    def token(t, _):
        x8 = x_ref[t]
        tok_rows = row_ref.at[pl.ds(t * NPICK, NPICK)]
        for c in range(PEER_HEADS):
            rows = _gather_head(tab_ref, tok_rows, c)
            z_ref[t, :, c * CHUNK_ROWS:(c + 1) * CHUNK_ROWS] = _dot_nt(x8, rows)
        return 0

    lax.fori_loop(0, PEER_TT, token, 0, unroll=PEER_UNROLL)
    zm = (z_ref[...] * diag_ref[...][None]).reshape(PEER_TT * 8, STAGE_COLS)
    hi, lo = _split_hi_lo(zm)
    part = _dot(hi, fold_ref[...]) + _dot(lo, fold_ref[...])
    hi, lo = _split_hi_lo(part)
    h_ref[...] = _dot(gsum_ref[...], hi) + _dot(gsum_ref[...], lo)


def _peer_up(rows_flat, x8, tab, diag, fold, gsum):
    t = x8.shape[0]
    tt = PEER_TT
    full = lambda a: pl.BlockSpec(a.shape, lambda i: (0,) * a.ndim)
    return pl.pallas_call(
        _peer_up_kernel,
        grid=(t // tt,),
        in_specs=[pl.BlockSpec((tt * NPICK,), lambda i: (i,), memory_space=pltpu.SMEM),
                  pl.BlockSpec((tt, 8, LANES), lambda i: (i, 0, 0)),
                  pl.BlockSpec(tab.shape, lambda i: (0, 0), pipeline_mode=pl.Buffered(1)),
                  full(diag), full(fold), full(gsum)],
        out_specs=pl.BlockSpec((tt, SLOTS), lambda i: (i, 0)),
        out_shape=jax.ShapeDtypeStruct((t, SLOTS), _f32),
        scratch_shapes=[pltpu.VMEM((tt, 8, STAGE_COLS), _f32)],
        compiler_params=_cparams(("arbitrary",)),
        name="peer_up",
    )(rows_flat, x8, tab, diag, fold, gsum)


def _peer_down_kernel(row_ref, h_ref, gate_ref, tab_ref, ex_ref, diag_ref, y_ref, wexp_ref):
    w = _gelu_tanh(h_ref[...]) * gate_ref[...]
    hi, lo = _split_hi_lo(w)
    wexp_ref[...] = _dot(hi, ex_ref[...]) + _dot(lo, ex_ref[...])

    def token(t, _):
        a = jnp.broadcast_to(wexp_ref[pl.ds(t, 1), :], diag_ref.shape) * diag_ref[...]
        ahi = a.astype(_bf16).astype(_f32)
        a2 = jnp.concatenate([ahi, a - ahi], axis=0).astype(_bf16)
        acc = jnp.zeros((2 * 8, LANES), _f32)
        tok_rows = row_ref.at[pl.ds(t * NPICK, NPICK)]
        for c in range(PEER_HEADS):
            rows = _gather_head(tab_ref, tok_rows, c)
            acc = acc + _dot(a2[:, c * CHUNK_ROWS:(c + 1) * CHUNK_ROWS], rows)
        y_ref[t] = acc[:8] + acc[8:]
        return 0

    lax.fori_loop(0, PEER_TT, token, 0, unroll=PEER_UNROLL)


def _peer_down(rows_flat, h2, gate2, tab, expand, diag):
    t = h2.shape[0]
    tt = PEER_TT
    full = lambda a: pl.BlockSpec(a.shape, lambda i: (0,) * a.ndim)
    return pl.pallas_call(
        _peer_down_kernel,
        grid=(t // tt,),
        in_specs=[pl.BlockSpec((tt * NPICK,), lambda i: (i,), memory_space=pltpu.SMEM),
                  pl.BlockSpec((tt, SLOTS), lambda i: (i, 0)),
                  pl.BlockSpec((tt, SLOTS), lambda i: (i, 0)),
                  pl.BlockSpec(tab.shape, lambda i: (0, 0), pipeline_mode=pl.Buffered(1)),
                  full(expand), full(diag)],
        out_specs=pl.BlockSpec((tt, 8, LANES), lambda i: (i, 0, 0)),
        out_shape=jax.ShapeDtypeStruct((t, 8, LANES), _f32),
        scratch_shapes=[pltpu.VMEM((tt, STAGE_COLS), _f32)],
        compiler_params=_cparams(("arbitrary",)),
        name="peer_down",
    )(rows_flat, h2, gate2, tab, expand, diag)


def _peer_constants():
    col = np.arange(STAGE_COLS)
    p, r = col // PAIR_ROWS, col % PAIR_ROWS
    slot = (p // PEER_TOPK) * 2 * PEER_TOPK + (r % 2) * PEER_TOPK + p % PEER_TOPK
    diag = (r[None, :] // 2 == np.arange(8)[:, None]).astype(np.float32)
    fold = (slot[:, None] == np.arange(SLOTS)[None, :]).astype(np.float32)
    gsum = (np.arange(PEER_TT * 8)[None, :] // 8 == np.arange(PEER_TT)[:, None]).astype(np.float32)
    return (jnp.asarray(diag, _f32), jnp.asarray(fold, _bf16), jnp.asarray(fold.T, _bf16),
            jnp.asarray(gsum, _bf16))


FIN_TM = 512


def _final_norm_kernel(x_ref, y_ref, g_ref, o_ref):
    z = x_ref[...] + y_ref[...]
    o_ref[...] = z * lax.rsqrt(jnp.mean(z * z, axis=-1, keepdims=True) + EPS) * g_ref[...]


def _final_norm(x1, y, g):
    t = x1.shape[0]
    row = pl.BlockSpec((FIN_TM, D_MODEL), lambda i: (i, 0))
    return pl.pallas_call(
        _final_norm_kernel,
        grid=(t // FIN_TM,),
        in_specs=[row, row, pl.BlockSpec((1, D_MODEL), lambda i: (0, 0))],
        out_specs=row,
        out_shape=jax.ShapeDtypeStruct((t, D_MODEL), _f32),
        compiler_params=_cparams(("parallel",)),
        name="final_norm",
    )(x1, y, g.reshape(1, D_MODEL))


def _cmp_to_sel_counts(n_rows, n_sel):
    r = SEL_LEN // CMP_STRIDE
    c = CMP_LEN // CMP_STRIDE
    off = np.arange(n_rows)[:, None] - r * np.arange(n_sel)[None, :] + (c - 1)
    counts = np.zeros((n_rows, LANES), np.float32)
    for n in range(c):
        counts[:, :n_sel] += ((off - n >= 0) & (off - n < r)).astype(np.float32)
    counts[n_rows - 1] = 0.0
    return counts


def _attention(x2d, b, s, w_in, norm_mix_g, rel_bias, lam_rows, subln_g,
               cmp_pos_k, cmp_pos_v, cmp_w1_k, cmp_w2_k, cmp_w1_v, cmp_w2_v):
    dq, dv, nq, kcvc, vs2, vw2, ng, dkT, ksT, kwT = _in_proj(x2d, norm_mix_g, *_arrange_w_in(w_in))
    to3 = lambda a: a.reshape(b, s, a.shape[-1])

    bt_diff = _bias_tiles(rel_bias[:, :DIFF_HEADS], DIFF_TQ, DIFF_TQ, 2)
    o_diff = _diff_attn(to3(dq), dkT, to3(dv), bt_diff, lam_rows, subln_g)

    n = s // CMP_STRIDE
    a = kcvc.reshape(b, n, CMP_STRIDE, 2, NSA_KV_HEADS, NSA_HD)
    a = jnp.transpose(a, (0, 3, 4, 1, 2, 5)).reshape(b, 2, NSA_KV_HEADS, n, CMP_STRIDE * NSA_HD)
    pos = jnp.stack([cmp_pos_k, cmp_pos_v]).reshape(2, 2, 1, CMP_STRIDE * NSA_HD)
    w1 = jnp.stack([cmp_w1_k, cmp_w1_v]).astype(_bf16)
    w2 = jnp.stack([cmp_w2_k, cmp_w2_v])
    w2dup = jnp.concatenate([w2, w2], axis=-1).astype(_bf16)
    cmp_kv = _compress(a, pos, w1, w2dup)

    n_sel = s // SEL_LEN
    counts = jnp.asarray(_cmp_to_sel_counts(n, n_sel), _bf16)
    nq3 = to3(nq)
    o_cmp, sel = _cmp_attn(nq3, cmp_kv[:, 0], cmp_kv[:, 1], counts, n_sel)

    nsa_bias = rel_bias[:, DIFF_HEADS:]
    tq = NSA_TQ
    def group_tiles(n_tiles, window=None):
        bt = _bias_tiles(nsa_bias, tq, tq, n_tiles, window=window)
        bt = bt.reshape(NSA_KV_HEADS, NSA_GROUP, n_tiles, tq, tq)[:, np.asarray(GROUP_ORDER)]
        return jnp.transpose(bt, (0, 2, 1, 3, 4)).reshape(NSA_KV_HEADS, n_tiles, NSA_GROUP * tq, tq)

    expand = jnp.asarray(np.arange(s)[None, :] // SEL_LEN == np.arange(LANES)[:, None], _bf16)
    o_sel = _sel_attn(nq3, ksT, to3(vs2), sel, expand, group_tiles(2))

    bt_win = jnp.transpose(group_tiles(WIN_TILES, window=WINDOW), (0, 2, 1, 3))
    bt_win = bt_win.reshape(NSA_KV_HEADS, NSA_GROUP * tq, WIN_TILES * tq)
    o_win = _win_attn(nq3, kwT, to3(vw2), bt_win)

    flat = lambda a3: a3.reshape(b * s, a3.shape[-1])
    return flat(o_diff), flat(o_cmp), flat(o_sel), flat(o_win), ng


def _gate_expand():
    ge = np.zeros((3, LANES, NSA_WIDTH), np.float32)
    for head in range(NSA_HEADS):
        for br in range(3):
            ge[br, 3 * head + br, head * NSA_HD:(head + 1) * NSA_HD] = 1.0
    return jnp.asarray(ge, _bf16)


def _peer(x1, xn, qp, peer_sub_keys, peer_u, peer_v):
    t = x1.shape[0]
    sk = peer_sub_keys.reshape(2 * PEER_HEADS, PEER_NKEYS, PEER_DKEY // 2).astype(_bf16)
    rows, gate2 = _peer_route(qp, sk)
    rows_flat = rows.reshape(t * NPICK)
    diag, fold, expand, gsum = _peer_constants()
    h2 = _peer_up(rows_flat, xn.reshape(t, 8, LANES), _pair_table(peer_u), diag, fold, gsum)
    y = _peer_down(rows_flat, h2, gate2, _pair_table(peer_v), expand, diag)
    return y.reshape(t, D_MODEL)


def kernel(x, w_in, w_out, norm_mix_g, norm_ffn_g, final_norm_g, rel_bias, diff_lq1, diff_lk1, diff_lq2, diff_lk2, diff_subln_g, cmp_pos_k, cmp_pos_v, cmp_w1_k, cmp_w2_k, cmp_w1_v, cmp_w2_v, peer_w_q, peer_sub_keys, peer_u, peer_v):
    b, s, d = x.shape
    x2d = x.reshape(b * s, d)
    lam_rows = jnp.zeros((8, LANES), _f32).at[0:4, :DIFF_HD].set(
        jnp.stack([diff_lq1[0], diff_lk1[0], diff_lq2[0], diff_lk2[0]]))
    o_diff, o_cmp, o_sel, o_win, ng = _attention(
        x2d, b, s, w_in[0], norm_mix_g[0], rel_bias, lam_rows, diff_subln_g[0],
        cmp_pos_k[0], cmp_pos_v[0], cmp_w1_k[0], cmp_w2_k[0], cmp_w1_v[0], cmp_w2_v[0])
    x1, xn, qp = _mix_out(x2d, o_diff, o_cmp, o_sel, o_win, ng, _gate_expand(),
                          w_out[0].astype(_bf16), norm_ffn_g[0], peer_w_q[0].astype(_bf16))
    y = _peer(x1, xn, qp, peer_sub_keys[0], peer_u[0], peer_v[0])
    return _final_norm(x1, y, final_norm_g).reshape(b, s, d)
```

```python
import functools
import math

import numpy as np
import jax
import jax.numpy as jnp
from jax import lax
from jax.experimental import pallas as pl
from jax.experimental.pallas import tpu as pltpu

D_MODEL = 1024
DIFF_HEADS = 4
DIFF_HD = 64
DIFF_VD = 2 * DIFF_HD
DIFF_WIDTH = DIFF_HEADS * DIFF_VD
NSA_HEADS = 8
NSA_KV_HEADS = 2
NSA_GROUP = NSA_HEADS // NSA_KV_HEADS
NSA_HD = 64
NSA_WIDTH = NSA_HEADS * NSA_HD
CMP_LEN = 32
CMP_STRIDE = 16
CMP_HIDDEN = 256
SEL_LEN = 64
SEL_TOPN = 8
WINDOW = 512
FORCE_BONUS = 1000.0
NEG_INF = -1e30
N_BUCKETS = 32
MAX_DISTANCE = 128
PEER_HEADS = 8
PEER_NKEYS = 128
PEER_EXPERTS = PEER_NKEYS ** 2
PEER_DKEY = 256
PEER_TOPK = 16
EPS = 1e-6
LAMBDA_INIT = 0.8 - 0.6 * math.exp(-0.3 * 0)
LOG2E = math.log2(math.e)

KV_COLS = NSA_KV_HEADS * NSA_HD
SPLIT_SIZES = (DIFF_HEADS * 2 * DIFF_HD, DIFF_HEADS * 2 * DIFF_HD, DIFF_WIDTH, NSA_WIDTH,
               KV_COLS, KV_COLS, KV_COLS, KV_COLS, KV_COLS, KV_COLS, 3 * NSA_HEADS)
SPLIT_OFF = tuple(int(v) for v in np.concatenate([[0], np.cumsum(SPLIT_SIZES)]))

LANES = 128
VMEM_LIMIT = 56 * 1024 * 1024
HALF_EXPERTS = PEER_EXPERTS // 2
PAIR_ROWS = 16
PAIR_WORD_ROWS = PAIR_ROWS // 2

_f32 = jnp.float32
_bf16 = jnp.bfloat16


def _cparams(sem):
    return pltpu.CompilerParams(dimension_semantics=sem, vmem_limit_bytes=VMEM_LIMIT)


def _dot(a, b):
    return jnp.dot(a, b, preferred_element_type=_f32)


def _dot_nt(a, b):
    return lax.dot_general(a, b, (((1,), (1,)), ((), ())), preferred_element_type=_f32)


def _split_hi_lo(x):
    hi = x.astype(_bf16)
    lo = (x - hi.astype(_f32)).astype(_bf16)
    return hi, lo


def _gelu_tanh(x):
    c = math.sqrt(2.0 / math.pi)
    return 0.5 * x * (1.0 + jnp.tanh(c * (x + 0.044715 * (x * x * x))))


IN_TM = 512
_IN_GROUPS = (("dq", 512, _bf16), ("dv", 512, _bf16), ("nq", 512, _bf16), ("kcvc", 256, _f32),
              ("vs2", 256, _bf16), ("vw2", 256, _bf16), ("ng", 128, _f32))
_IN_KEYS_T = (("dkT", 512), ("ksT", 256), ("kwT", 256))


def _in_proj_kernel(x_ref, g_ref, w_ref, wt_ref, *out_refs):
    x = x_ref[...]
    y = x * lax.rsqrt(jnp.mean(x * x, axis=-1, keepdims=True) + EPS)
    h = (y * g_ref[...]).astype(_bf16)
    off = 0
    for (_, width, dt), o_ref in zip(_IN_GROUPS, out_refs):
        o_ref[...] = _dot(h, w_ref[:, off:off + width]).astype(dt)
        off += width
    off = 0
    for (_, width), o_ref in zip(_IN_KEYS_T, out_refs[len(_IN_GROUPS):]):
        o_ref[...] = _dot_nt(wt_ref[off:off + width, :], h).astype(o_ref.dtype)
        off += width


def _arrange_w_in(w_in):
    o = SPLIT_OFF
    sl = lambda i: w_in[:, o[i]:o[i + 1]]
    dup = lambda w: jnp.concatenate([w[:, :64], w[:, :64], w[:, 64:], w[:, 64:]], axis=1)
    ng = jnp.pad(sl(10), ((0, 0), (0, LANES - 3 * NSA_HEADS)))
    cols = [sl(0) * (DIFF_HD ** -0.5 * LOG2E), sl(2), sl(3) * (NSA_HD ** -0.5 * LOG2E),
            sl(4), sl(5), dup(sl(7)), dup(sl(9)), ng]
    keys = [sl(1), dup(sl(6)), dup(sl(8))]
    return (jnp.concatenate(cols, axis=1).astype(_bf16),
            jnp.concatenate(keys, axis=1).T.astype(_bf16))


def _in_proj(x2d, g, w_arr, wt_arr):
    t = x2d.shape[0]
    out_shape = ([jax.ShapeDtypeStruct((t, wd), dt) for _, wd, dt in _IN_GROUPS]
                 + [jax.ShapeDtypeStruct((wd, t), _bf16) for _, wd in _IN_KEYS_T])
    out_specs = ([pl.BlockSpec((IN_TM, wd), lambda i: (i, 0)) for _, wd, _ in _IN_GROUPS]
                 + [pl.BlockSpec((wd, IN_TM), lambda i: (0, i)) for _, wd in _IN_KEYS_T])
    return pl.pallas_call(
        _in_proj_kernel,
        grid=(t // IN_TM,),
        in_specs=[pl.BlockSpec((IN_TM, D_MODEL), lambda i: (i, 0)),
                  pl.BlockSpec((1, D_MODEL), lambda i: (0, 0)),
                  pl.BlockSpec(w_arr.shape, lambda i: (0, 0)),
                  pl.BlockSpec(wt_arr.shape, lambda i: (0, 0))],
        out_specs=out_specs,
        out_shape=out_shape,
        compiler_params=_cparams(("parallel",)),
        name="in_proj",
    )(x2d, g.reshape(1, D_MODEL), w_arr, wt_arr)


def _bucket_table(n):
    rel = np.arange(n)
    max_exact = N_BUCKETS // 2
    nf = np.maximum(rel, 1).astype(np.float32)
    large = max_exact + (np.log(nf / np.float32(max_exact)) / np.float32(math.log(MAX_DISTANCE / max_exact))
                         * np.float32(N_BUCKETS - max_exact)).astype(np.int32)
    large = np.minimum(large, N_BUCKETS - 1)
    return np.where(rel < max_exact, rel, large).astype(np.int32)


def _bias_tiles(rel_bias_heads, tq, tk, n_tiles, window=None):
    tab = (rel_bias_heads.astype(_f32) - rel_bias_heads[N_BUCKETS - 1][None, :].astype(_f32)) * LOG2E
    period = tq + tk
    m = np.arange(period)
    off = np.where(m < tk, -m, period - m)
    rel_w = np.arange(n_tiles)[:, None] * tq + off[None, :]
    bucket = _bucket_table(n_tiles * tq + period)[np.clip(rel_w, 0, None)]
    w = jnp.transpose(tab[bucket], (2, 0, 1))
    tiles = jnp.tile(w, (1, 1, tq))[..., :tq * (period - 1)]
    tiles = tiles.reshape(w.shape[0], n_tiles, tq, period - 1)[..., :tk]
    rel = (np.arange(n_tiles)[:, None, None] * tq + np.arange(tq)[None, :, None]
           - np.arange(tk)[None, None, :])
    ok = rel >= 0
    if window is not None:
        ok &= rel < window
    return jnp.where(jnp.asarray(ok)[None], tiles, NEG_INF)


def _near_tiles(bt):
    first = jnp.concatenate([bt[..., 0, :, :], jnp.full_like(bt[..., 0, :, :], NEG_INF)], axis=-1)
    later = jnp.concatenate([bt[..., 1, :, :], bt[..., 0, :, :]], axis=-1)
    return jnp.stack([first, later], axis=-3)


def _attend_first(s, v_aug):
    m = jnp.max(s, axis=-1, keepdims=True)
    return m, _dot(jnp.exp2(s - m).astype(_bf16), v_aug)


def _attend_step(carry, s, v_aug):
    m, acc = carry
    m_new = jnp.maximum(m, jnp.max(s, axis=-1, keepdims=True))
    acc = jnp.exp2(m - m_new) * acc + _dot(jnp.exp2(s - m_new).astype(_bf16), v_aug)
    return m_new, acc


def _lane_half_mask(shape):
    return lax.broadcasted_iota(jnp.int32, shape, len(shape) - 1) < (LANES // 2)


DIFF_TQ = 256


def _diff_attn_kernel(q_ref, k_ref, v_ref, bt_ref, lam_ref, g_ref, o_ref):
    tq = DIFF_TQ
    qb = pl.program_id(2)
    q = q_ref[...]
    lo = _lane_half_mask(q.shape)
    zero = jnp.zeros_like(q)
    q2 = jnp.concatenate([jnp.where(lo, q, zero), jnp.where(lo, zero, q)], axis=0)

    def kv(start, n):
        start = pl.multiple_of(start, tq)
        v = v_ref[pl.ds(start, n), :]
        return k_ref[:, pl.ds(start, n)], jnp.concatenate([v, jnp.ones_like(v)], axis=1)

    def both(b):
        return jnp.concatenate([b, b], axis=0)

    k, v = kv(jnp.maximum(qb - 1, 0) * tq, 2 * tq)
    carry = _attend_first(_dot(q2, k) + both(bt_ref[jnp.minimum(qb, 1)]), v)

    n_far = jnp.maximum(qb - 1, 0)

    def odd(c):
        k, v = kv((n_far - 1) * tq, tq)
        return _attend_step(c, _dot(q2, k), v)

    carry = lax.cond(n_far % 2 == 1, odd, lambda c: c, carry)

    def pair(i, c):
        k, v = kv(i * 2 * tq, 2 * tq)
        return _attend_step(c, _dot(q2, k), v)

    m, acc = lax.fori_loop(0, n_far // 2, pair, carry)
    o2 = acc[:, :DIFF_VD] / acc[:, DIFF_VD:]
    lv = lam_ref[...]
    lam = (jnp.exp(jnp.sum(lv[0:1] * lv[1:2], axis=-1, keepdims=True))
           - jnp.exp(jnp.sum(lv[2:3] * lv[3:4], axis=-1, keepdims=True)) + LAMBDA_INIT)
    o = o2[:tq] - lam * o2[tq:]
    y = o * lax.rsqrt(jnp.mean(o * o, axis=-1, keepdims=True) + EPS)
    o_ref[...] = (y * g_ref[...] * (1.0 - LAMBDA_INIT)).astype(o_ref.dtype)


def _diff_attn(dq, dkT, dv, bt, lam_rows, subln_g):
    b, s, _ = dq.shape
    tq = DIFF_TQ
    return pl.pallas_call(
        _diff_attn_kernel,
        grid=(b, DIFF_HEADS, s // tq),
        in_specs=[pl.BlockSpec((None, tq, LANES), lambda bi, h, qb: (bi, qb, h)),
                  pl.BlockSpec((LANES, s), lambda bi, h, qb: (h, bi)),
                  pl.BlockSpec((None, s, LANES), lambda bi, h, qb: (bi, 0, h)),
                  pl.BlockSpec((None, 2, tq, 2 * tq), lambda bi, h, qb: (h, 0, 0, 0)),
                  pl.BlockSpec((8, LANES), lambda bi, h, qb: (0, 0)),
                  pl.BlockSpec((1, LANES), lambda bi, h, qb: (0, 0))],
        out_specs=pl.BlockSpec((None, tq, LANES), lambda bi, h, qb: (bi, qb, h)),
        out_shape=jax.ShapeDtypeStruct((b, s, DIFF_WIDTH), _bf16),
        compiler_params=_cparams(("parallel", "parallel", "arbitrary")),
        name="diff_attn",
    )(dq, dkT, dv, bt, lam_rows, subln_g.reshape(1, LANES))


def _compress_kernel(a_ref, pos_ref, w1_ref, w2_ref, o_ref):
    a = a_ref[...]
    half = (CMP_LEN // 2) * NSA_HD
    za = _dot((a + pos_ref[0]).astype(_bf16), w1_ref[:half, :])
    zb = _dot((a + pos_ref[1]).astype(_bf16), w1_ref[half:, :])
    n = a.shape[0]
    hid = za + pltpu.roll(zb, n - 1, 0)
    o_ref[...] = _dot(_gelu_tanh(hid).astype(_bf16), w2_ref[...]).astype(o_ref.dtype)


def _compress(a, pos, w1, w2dup):
    b, _, _, n, width = a.shape
    return pl.pallas_call(
        _compress_kernel,
        grid=(b, 2, NSA_KV_HEADS),
        in_specs=[pl.BlockSpec((None, None, None, n, width), lambda bi, kv, g: (bi, kv, g, 0, 0)),
                  pl.BlockSpec((None, 2, 1, width), lambda bi, kv, g: (kv, 0, 0, 0)),
                  pl.BlockSpec((None, 2 * width, CMP_HIDDEN), lambda bi, kv, g: (kv, 0, 0)),
                  pl.BlockSpec((None, CMP_HIDDEN, LANES), lambda bi, kv, g: (kv, 0, 0))],
        out_specs=pl.BlockSpec((None, None, None, n, LANES), lambda bi, kv, g: (bi, kv, g, 0, 0)),
        out_shape=jax.ShapeDtypeStruct((b, 2, NSA_KV_HEADS, n, LANES), _bf16),
        compiler_params=_cparams(("parallel", "parallel", "parallel")),
        name="compress",
    )(a, pos, w1, w2dup)


NSA_TQ = 128


GROUP_ORDER = (0, 2, 1, 3)


def _stack_group_queries(q):
    rows = []
    for j in GROUP_ORDER:
        blk = q[:, (j // 2) * LANES:(j // 2 + 1) * LANES]
        lo = _lane_half_mask(blk.shape)
        keep = lo if j % 2 == 0 else jnp.logical_not(lo)
        rows.append(jnp.where(keep, blk, jnp.zeros_like(blk)))
    return jnp.concatenate(rows, axis=0)


def _unstack_group_outputs(o4, tq):
    lo = _lane_half_mask((tq, LANES))
    half = NSA_GROUP // 2
    pairs = [jnp.where(lo, o4[i * tq:(i + 1) * tq], o4[(half + i) * tq:(half + i + 1) * tq])
             for i in range(half)]
    return jnp.concatenate(pairs, axis=1)


def _group_values(v):
    lo = _lane_half_mask(v.shape)
    one = jnp.ones_like(v)
    return jnp.where(lo, v, one), jnp.where(lo, one, v)


def _group_pv(p, v_pair):
    n = p.shape[0] // 2
    p = p.astype(_bf16)
    return jnp.concatenate([_dot(p[:n], v_pair[0]), _dot(p[n:], v_pair[1])], axis=0)


def _group_attend_first(s, v_pair):
    m = jnp.max(s, axis=-1, keepdims=True)
    return m, _group_pv(jnp.exp2(s - m), v_pair)


def _group_attend_step(carry, s, v_pair):
    m, acc = carry
    m_new = jnp.maximum(m, jnp.max(s, axis=-1, keepdims=True))
    return m_new, jnp.exp2(m - m_new) * acc + _group_pv(jnp.exp2(s - m_new), v_pair)


def _group_normalise(acc):
    return acc / pltpu.roll(acc, LANES // 2, 1)


def _cmp_attn_kernel(q_ref, kc_ref, vc_ref, cs_ref, o_ref, sel_ref, *, n_sel):
    tq = NSA_TQ
    qb = pl.program_id(2)
    q4 = _stack_group_queries(q_ref[...])
    s = _dot_nt(q4, kc_ref[...])
    row = lax.broadcasted_iota(jnp.int32, s.shape, 0)
    pos4 = qb * tq + (row & (tq - 1))
    c = lax.broadcasted_iota(jnp.int32, s.shape, 1)
    ok = (c * CMP_STRIDE + (CMP_LEN - 1)) <= pos4
    s = jnp.where(ok, s, NEG_INF)
    m = jnp.max(s, axis=-1, keepdims=True)
    e = jnp.where(ok, jnp.exp2(s - m), 0.0)
    l = jnp.sum(e, axis=-1, keepdims=True)
    p = e / jnp.where(l > 0.0, l, 1.0)
    o_ref[...] = _unstack_group_outputs(_dot(p.astype(_bf16), vc_ref[...]), tq).astype(o_ref.dtype)

    psum = p[0:tq] + p[tq:2 * tq] + p[2 * tq:3 * tq] + p[3 * tq:4 * tq]
    hi, lo = _split_hi_lo(psum)
    imp = _dot(hi, cs_ref[...]) + _dot(lo, cs_ref[...])
    blk = lax.broadcasted_iota(jnp.int32, imp.shape, 1)
    pos = qb * tq + lax.broadcasted_iota(jnp.int32, imp.shape, 0)
    cur = pos // SEL_LEN
    forced = (blk == 0) | (blk == cur) | (blk == cur - 1)
    score = jnp.where(blk <= cur, imp + jnp.where(forced, FORCE_BONUS, 0.0), NEG_INF)
    st = score.T[:n_sel]
    blk_t = lax.broadcasted_iota(jnp.int32, st.shape, 0)
    rank = jnp.zeros(st.shape, jnp.int32)
    for i in range(n_sel):
        si = st[i:i + 1]
        beats = (si > st) | ((si == st) & (i < blk_t))
        rank = rank + beats.astype(jnp.int32)
    sel_t = ((rank < SEL_TOPN) & (st > 0.5 * NEG_INF)).astype(_f32)
    sel = jnp.concatenate([sel_t, jnp.zeros((LANES - n_sel, tq), _f32)], axis=0).T
    sel_ref[...] = sel.astype(sel_ref.dtype)


def _cmp_attn(nq, kcmp, vcmp, cmp_sel, n_sel):
    b, s, _ = nq.shape
    tq = NSA_TQ
    n = kcmp.shape[-2]
    gw = NSA_GROUP * NSA_HD
    return pl.pallas_call(
        functools.partial(_cmp_attn_kernel, n_sel=n_sel),
        grid=(b, NSA_KV_HEADS, s // tq),
        in_specs=[pl.BlockSpec((None, tq, gw), lambda bi, g, qb: (bi, qb, g)),
                  pl.BlockSpec((None, None, n, LANES), lambda bi, g, qb: (bi, g, 0, 0)),
                  pl.BlockSpec((None, None, n, LANES), lambda bi, g, qb: (bi, g, 0, 0)),
                  pl.BlockSpec((n, LANES), lambda bi, g, qb: (0, 0))],
        out_specs=[pl.BlockSpec((None, tq, gw), lambda bi, g, qb: (bi, qb, g)),
                   pl.BlockSpec((None, None, tq, LANES), lambda bi, g, qb: (bi, g, qb, 0))],
        out_shape=[jax.ShapeDtypeStruct((b, s, NSA_WIDTH), _bf16),
                   jax.ShapeDtypeStruct((b, NSA_KV_HEADS, s, LANES), _bf16)],
        compiler_params=_cparams(("parallel", "parallel", "arbitrary")),
        name="cmp_attn",
    )(nq, kcmp, vcmp, cmp_sel)


SEL_CHUNK = 4 * NSA_TQ


def _sel_attn_kernel(q_ref, k_ref, v_ref, sel_ref, ex_ref, bt_ref, o_ref):
    tq = NSA_TQ
    qb = pl.program_id(2)
    q4 = _stack_group_queries(q_ref[...])
    sel = sel_ref[...]

    def scores(pen_rows, start, n):
        start = pl.multiple_of(start, tq)
        kx = jnp.concatenate([k_ref[:, pl.ds(start, n)], ex_ref[:, pl.ds(start, n)]], axis=0)
        qx = jnp.concatenate([q4, jnp.concatenate([pen_rows] * NSA_GROUP, axis=0)], axis=1)
        return _dot(qx, kx), _group_values(v_ref[pl.ds(start, n), :])

    def penalty(sel_rows):
        return ((sel_rows.astype(_f32) - 1.0) * (-NEG_INF)).astype(_bf16)

    s, v = scores(penalty(sel), jnp.maximum(qb - 1, 0) * tq, 2 * tq)
    carry = _group_attend_first(s + bt_ref[jnp.minimum(qb, 1)], v)

    far_len = jnp.maximum(qb - 1, 0) * tq
    blk = lax.broadcasted_iota(jnp.int32, sel.shape, 1)
    pen_far = penalty(jnp.where(blk * SEL_LEN < far_len, sel, jnp.zeros_like(sel)))

    def far(i, c):
        s, v = scores(pen_far, i * SEL_CHUNK, SEL_CHUNK)
        return _group_attend_step(c, s, v)

    m, acc = lax.fori_loop(0, (far_len + SEL_CHUNK - 1) // SEL_CHUNK, far, carry)
    o_ref[...] = _unstack_group_outputs(_group_normalise(acc), tq).astype(o_ref.dtype)


def _sel_attn(nq, ks2, vs2, sel, expand, bt):
    b, s, _ = nq.shape
    tq = NSA_TQ
    gw = NSA_GROUP * NSA_HD
    return pl.pallas_call(
        _sel_attn_kernel,
        grid=(b, NSA_KV_HEADS, s // tq),
        in_specs=[pl.BlockSpec((None, tq, gw), lambda bi, g, qb: (bi, qb, g)),
                  pl.BlockSpec((LANES, s), lambda bi, g, qb: (g, bi)),
                  pl.BlockSpec((None, s, LANES), lambda bi, g, qb: (bi, 0, g)),
                  pl.BlockSpec((None, None, tq, LANES), lambda bi, g, qb: (bi, g, qb, 0)),
                  pl.BlockSpec((LANES, s), lambda bi, g, qb: (0, 0)),
                  pl.BlockSpec((None, 2, NSA_GROUP * tq, 2 * tq), lambda bi, g, qb: (g, 0, 0, 0))],
        out_specs=pl.BlockSpec((None, tq, gw), lambda bi, g, qb: (bi, qb, g)),
        out_shape=jax.ShapeDtypeStruct((b, s, NSA_WIDTH), _bf16),
        compiler_params=_cparams(("parallel", "parallel", "arbitrary")),
        name="sel_attn",
    )(nq, ks2, vs2, sel, expand, bt)


WIN_TILES = WINDOW // NSA_TQ + 1


def _win_attn_kernel(q_ref, k_ref, v_ref, bt_ref, o_ref):
    tq = NSA_TQ
    qb = pl.program_id(2)
    q4 = _stack_group_queries(q_ref[...])
    s_blocks, va, vb = [], [], []
    for d in range(WIN_TILES):
        start = pl.multiple_of(jnp.maximum(qb - d, 0) * tq, tq)
        s = _dot(q4, k_ref[:, pl.ds(start, tq)])
        s_blocks.append(s if d == 0 else s + jnp.where(qb >= d, 0.0, NEG_INF))
        a, b = _group_values(v_ref[pl.ds(start, tq), :])
        va.append(a)
        vb.append(b)
    s = jnp.concatenate(s_blocks, axis=1) + bt_ref[...]
    m, acc = _group_attend_first(s, (jnp.concatenate(va, axis=0), jnp.concatenate(vb, axis=0)))
    o_ref[...] = _unstack_group_outputs(_group_normalise(acc), tq).astype(o_ref.dtype)


def _win_attn(nq, kw2, vw2, bt):
    b, s, _ = nq.shape
    tq = NSA_TQ
    gw = NSA_GROUP * NSA_HD
    return pl.pallas_call(
        _win_attn_kernel,
        grid=(b, NSA_KV_HEADS, s // tq),
        in_specs=[pl.BlockSpec((None, tq, gw), lambda bi, g, qb: (bi, qb, g)),
                  pl.BlockSpec((LANES, s), lambda bi, g, qb: (g, bi)),
                  pl.BlockSpec((None, s, LANES), lambda bi, g, qb: (bi, 0, g)),
                  pl.BlockSpec((None, NSA_GROUP * tq, WIN_TILES * tq), lambda bi, g, qb: (g, 0, 0))],
        out_specs=pl.BlockSpec((None, tq, gw), lambda bi, g, qb: (bi, qb, g)),
        out_shape=jax.ShapeDtypeStruct((b, s, NSA_WIDTH), _bf16),
        compiler_params=_cparams(("parallel", "parallel", "arbitrary")),
        name="win_attn",
    )(nq, kw2, vw2, bt)


MIX_TM = 256


def _mix_out_kernel(x_ref, od_ref, oc_ref, os_ref, ow_ref, ng_ref, ge_ref, wo_ref, g_ref, wq_ref,
                    x1_ref, xn_ref, qp_ref):
    sig = jax.nn.sigmoid(ng_ref[...])
    hi, lo = _split_hi_lo(sig)
    o_nsa = jnp.zeros(oc_ref.shape, _f32)
    for br, o_ref in enumerate((oc_ref, os_ref, ow_ref)):
        gate = _dot(hi, ge_ref[br]) + _dot(lo, ge_ref[br])
        o_nsa = o_nsa + gate * o_ref[...].astype(_f32)
    y = _dot(od_ref[...], wo_ref[:DIFF_WIDTH, :]) + _dot(o_nsa.astype(_bf16), wo_ref[DIFF_WIDTH:, :])
    x1 = x_ref[...] + y
    x1_ref[...] = x1
    xn = x1 * lax.rsqrt(jnp.mean(x1 * x1, axis=-1, keepdims=True) + EPS) * g_ref[...]
    xn = xn.astype(_bf16)
    xn_ref[...] = xn
    qp_ref[...] = _dot(xn, wq_ref[...]).astype(qp_ref.dtype)


def _mix_out(x2d, o_diff, o_cmp, o_sel, o_win, ng, gate_expand, w_out, g_ffn, w_q):
    t = x2d.shape[0]
    tm = MIX_TM
    nq = w_q.shape[1]
    row = lambda w: pl.BlockSpec((tm, w), lambda i: (i, 0))
    full = lambda a: pl.BlockSpec(a.shape, lambda i: (0,) * a.ndim)
    g2 = g_ffn.reshape(1, D_MODEL)
    return pl.pallas_call(
        _mix_out_kernel,
        grid=(t // tm,),
        in_specs=[row(D_MODEL), row(DIFF_WIDTH), row(NSA_WIDTH), row(NSA_WIDTH), row(NSA_WIDTH),
                  row(LANES), full(gate_expand), full(w_out), full(g2), full(w_q)],
        out_specs=[row(D_MODEL), row(D_MODEL), row(nq)],
        out_shape=[jax.ShapeDtypeStruct((t, D_MODEL), _f32),
                   jax.ShapeDtypeStruct((t, D_MODEL), _bf16),
                   jax.ShapeDtypeStruct((t, nq), _bf16)],
        compiler_params=_cparams(("parallel",)),
        name="mix_out",
    )(x2d, o_diff, o_cmp, o_sel, o_win, ng, gate_expand, w_out, g2, w_q)


ROUTE_TM = 256
SLOTS = 2 * PEER_HEADS * PEER_TOPK


def _top16_rows(s, rows, n):
    vals, ids = [], []
    for _ in range(PEER_TOPK):
        m = jnp.max(s, axis=0, keepdims=True)
        idx = jnp.min(jnp.where(s == m, rows, n), axis=0, keepdims=True)
        vals.append(m)
        ids.append(idx)
        s = jnp.where(rows == idx, -jnp.inf, s)
    return jnp.concatenate(vals, axis=0), jnp.concatenate(ids, axis=0)


def _peer_route_kernel(q_ref, sk_ref, flat_ref, row_ref, gate_ref, rowt_ref, gatet_ref):
    tm = ROUTE_TM
    rows_k = lax.broadcasted_iota(jnp.int32, (PEER_NKEYS, tm), 0)
    flat_c = jnp.broadcast_to(flat_ref[...], (flat_ref.shape[0], tm))
    keep_c = flat_c >= 0
    half_d = PEER_DKEY // 2

    def head(h, _):
        col = pl.multiple_of(h * PEER_DKEY, PEER_DKEY)
        q0 = q_ref[:, pl.ds(col, half_d)]
        q1 = q_ref[:, pl.ds(col + half_d, half_d)]
        sv0, si0 = _top16_rows(_dot_nt(sk_ref[2 * h], q0), rows_k, PEER_NKEYS)
        sv1, si1 = _top16_rows(_dot_nt(sk_ref[2 * h + 1], q1), rows_k, PEER_NKEYS)
        slabs, ids = [], []
        for a, b0 in _PAIR_SLABS:
            if a is None:
                slabs.append(sv0[8:16] + sv1[0:1])
                ids.append(si0[8:16] * PEER_NKEYS + si1[0:1])
            else:
                slabs.append(sv0[a:a + 1] + sv1[b0:b0 + 8])
                ids.append(si0[a:a + 1] * PEER_NKEYS + si1[b0:b0 + 8])
        cand = jnp.where(keep_c, jnp.concatenate(slabs, axis=0), -jnp.inf)
        cidx = jnp.concatenate(ids, axis=0)
        cv, ci = _top16_rows(cand, flat_c, PEER_TOPK * PEER_TOPK)
        e = jnp.concatenate(
            [jnp.sum(jnp.where(flat_c == ci[k:k + 1], cidx, 0), axis=0, keepdims=True)
             for k in range(PEER_TOPK)], axis=0)
        ex = jnp.exp(cv - cv[0:1])
        gate = ex / jnp.sum(ex, axis=0, keepdims=True)
        upper = e >= HALF_EXPERTS
        base = pl.multiple_of(h * PEER_TOPK, PEER_TOPK)
        rowt_ref[pl.ds(base, PEER_TOPK), :] = (e & (HALF_EXPERTS - 1)) * PAIR_WORD_ROWS
        base2 = pl.multiple_of(h * 2 * PEER_TOPK, 2 * PEER_TOPK)
        gatet_ref[pl.ds(base2, PEER_TOPK), :] = jnp.where(upper, 0.0, gate)
        gatet_ref[pl.ds(base2 + PEER_TOPK, PEER_TOPK), :] = jnp.where(upper, gate, 0.0)
        return 0

    lax.fori_loop(0, PEER_HEADS, head, 0)
    row_ref[...] = rowt_ref[...].T
    gate_ref[...] = gatet_ref[...].T


def _pair_slabs():
    slabs, flat = [(0, 0), (0, 8)] + [(a, 0) for a in range(1, 8)] + [(None, 0)], []
    for a, b0 in slabs:
        for i in range(8):
            aa, bb = (8 + i, 0) if a is None else (a, b0 + i)
            flat.append(aa * PEER_TOPK + bb if (aa + 1) * (bb + 1) <= PEER_TOPK else -1)
    return tuple(slabs), np.asarray(flat, np.int32).reshape(-1, 1)


_PAIR_SLABS, _PAIR_FLAT = _pair_slabs()


def _peer_route(qp, sk):
    t = qp.shape[0]
    tm = ROUTE_TM
    npick = PEER_HEADS * PEER_TOPK
    return pl.pallas_call(
        _peer_route_kernel,
        grid=(t // tm,),
        in_specs=[pl.BlockSpec((tm, qp.shape[1]), lambda i: (i, 0)),
                  pl.BlockSpec(sk.shape, lambda i: (0, 0, 0)),
                  pl.BlockSpec(_PAIR_FLAT.shape, lambda i: (0, 0))],
        out_specs=[pl.BlockSpec((tm, npick), lambda i: (i, 0)),
                   pl.BlockSpec((tm, SLOTS), lambda i: (i, 0))],
        out_shape=[jax.ShapeDtypeStruct((t, npick), jnp.int32),
                   jax.ShapeDtypeStruct((t, SLOTS), _f32)],
        scratch_shapes=[pltpu.VMEM((npick, tm), jnp.int32), pltpu.VMEM((SLOTS, tm), _f32)],
        compiler_params=_cparams(("parallel",)),
        name="peer_route",
    )(qp, sk, jnp.asarray(_PAIR_FLAT))


PEER_TT = 64
N_OFFS = 8
PEER_UNROLL = 8
NPICK = PEER_HEADS * PEER_TOPK
CHUNK_ROWS = PEER_TOPK * PAIR_ROWS
STAGE_COLS = NPICK * PAIR_ROWS


def _pair_table(w):
    e = w.shape[0]
    w4 = w.astype(_bf16).reshape(2, e // 2, PAIR_WORD_ROWS, LANES)
    words = lax.bitcast_convert_type(jnp.transpose(w4, (1, 2, 3, 0)), jnp.int32)
    return words.reshape(e // 2 * PAIR_WORD_ROWS, LANES)


def _gather_head(tab_ref, tok_rows, c, offs):
    tiles = []
    n_off = len(offs)
    for k in range(PEER_TOPK):
        if k % n_off == 0:
            part_rows = tok_rows.at[pl.ds(c * PEER_TOPK + k, n_off)]
        r = pl.multiple_of(part_rows[offs[k % n_off]], PAIR_WORD_ROWS)
        tiles.append(pltpu.bitcast(tab_ref[pl.ds(r, PAIR_WORD_ROWS), :], _bf16))
    return jnp.concatenate(tiles, axis=0)


def _peer_up_kernel(row_ref, off_ref, x_ref, tab_ref, diag_ref, fold_ref, gsum_ref, h_ref, z_ref):
    offs = [off_ref[k] for k in range(N_OFFS)]

    def token(t, _):
        x8 = x_ref[t]
        tok_rows = row_ref.at[pl.ds(t * NPICK, NPICK)]
        for c in range(PEER_HEADS):
            rows = _gather_head(tab_ref, tok_rows, c, offs)
            z_ref[t, :, c * CHUNK_ROWS:(c + 1) * CHUNK_ROWS] = _dot_nt(x8, rows)
        return 0

    lax.fori_loop(0, PEER_TT, token, 0, unroll=PEER_UNROLL)
    zm = (z_ref[...] * diag_ref[...][None]).reshape(PEER_TT * 8, STAGE_COLS)
    part = _dot(zm.astype(_bf16), fold_ref[...])
    hi, lo = _split_hi_lo(part)
    h_ref[...] = _dot(gsum_ref[...], hi) + _dot(gsum_ref[...], lo)


def _peer_up(rows_flat, x8, tab, diag, fold, gsum):
    t = x8.shape[0]
    tt = PEER_TT
    full = lambda a: pl.BlockSpec(a.shape, lambda i: (0,) * a.ndim)
    return pl.pallas_call(
        _peer_up_kernel,
        grid=(t // tt,),
        in_specs=[pl.BlockSpec((tt * NPICK,), lambda i: (i,), memory_space=pltpu.SMEM),
                  pl.BlockSpec((N_OFFS,), lambda i: (0,), memory_space=pltpu.SMEM),
                  pl.BlockSpec((tt, 8, LANES), lambda i: (i, 0, 0)),
                  pl.BlockSpec(tab.shape, lambda i: (0, 0), pipeline_mode=pl.Buffered(1)),
                  full(diag), full(fold), full(gsum)],
        out_specs=pl.BlockSpec((tt, SLOTS), lambda i: (i, 0)),
        out_shape=jax.ShapeDtypeStruct((t, SLOTS), _f32),
        scratch_shapes=[pltpu.VMEM((tt, 8, STAGE_COLS), _f32)],
        compiler_params=_cparams(("arbitrary",)),
        name="peer_up",
    )(rows_flat, jnp.arange(N_OFFS, dtype=jnp.int32), x8, tab, diag, fold, gsum)


def _peer_down_kernel(row_ref, off_ref, h_ref, gate_ref, tab_ref, ex_ref, diag_ref, y_ref, wexp_ref):
    offs = [off_ref[k] for k in range(N_OFFS)]
    w = _gelu_tanh(h_ref[...]) * gate_ref[...]
    hi, lo = _split_hi_lo(w)
    wexp_ref[...] = _dot(hi, ex_ref[...]) + _dot(lo, ex_ref[...])

    def token(t, _):
        a = jnp.broadcast_to(wexp_ref[pl.ds(t, 1), :], diag_ref.shape) * diag_ref[...]
        ahi = a.astype(_bf16).astype(_f32)
        a2 = jnp.concatenate([ahi, a - ahi], axis=0).astype(_bf16)
        acc = jnp.zeros((2 * 8, LANES), _f32)
        tok_rows = row_ref.at[pl.ds(t * NPICK, NPICK)]
        for c in range(PEER_HEADS):
            rows = _gather_head(tab_ref, tok_rows, c, offs)
            acc = acc + _dot(a2[:, c * CHUNK_ROWS:(c + 1) * CHUNK_ROWS], rows)
        y_ref[t] = acc[:8] + acc[8:]
        return 0

    lax.fori_loop(0, PEER_TT, token, 0, unroll=PEER_UNROLL)


def _peer_down(rows_flat, h2, gate2, tab, expand, diag):
    t = h2.shape[0]
    tt = PEER_TT
    full = lambda a: pl.BlockSpec(a.shape, lambda i: (0,) * a.ndim)
    return pl.pallas_call(
        _peer_down_kernel,
        grid=(t // tt,),
        in_specs=[pl.BlockSpec((tt * NPICK,), lambda i: (i,), memory_space=pltpu.SMEM),
                  pl.BlockSpec((N_OFFS,), lambda i: (0,), memory_space=pltpu.SMEM),
                  pl.BlockSpec((tt, SLOTS), lambda i: (i, 0)),
                  pl.BlockSpec((tt, SLOTS), lambda i: (i, 0)),
                  pl.BlockSpec(tab.shape, lambda i: (0, 0), pipeline_mode=pl.Buffered(1)),
                  full(expand), full(diag)],
        out_specs=pl.BlockSpec((tt, 8, LANES), lambda i: (i, 0, 0)),
        out_shape=jax.ShapeDtypeStruct((t, 8, LANES), _f32),
        scratch_shapes=[pltpu.VMEM((tt, STAGE_COLS), _f32)],
        compiler_params=_cparams(("arbitrary",)),
        name="peer_down",
    )(rows_flat, jnp.arange(N_OFFS, dtype=jnp.int32), h2, gate2, tab, expand, diag)


def _peer_constants():
    col = np.arange(STAGE_COLS)
    p, r = col // PAIR_ROWS, col % PAIR_ROWS
    slot = (p // PEER_TOPK) * 2 * PEER_TOPK + (r % 2) * PEER_TOPK + p % PEER_TOPK
    diag = (r[None, :] // 2 == np.arange(8)[:, None]).astype(np.float32)
    fold = (slot[:, None] == np.arange(SLOTS)[None, :]).astype(np.float32)
    gsum = (np.arange(PEER_TT * 8)[None, :] // 8 == np.arange(PEER_TT)[:, None]).astype(np.float32)
    return (jnp.asarray(diag, _f32), jnp.asarray(fold, _bf16), jnp.asarray(fold.T, _bf16),
            jnp.asarray(gsum, _bf16))


FIN_TM = 512


def _final_norm_kernel(x_ref, y_ref, g_ref, o_ref):
    z = x_ref[...] + y_ref[...]
    o_ref[...] = z * lax.rsqrt(jnp.mean(z * z, axis=-1, keepdims=True) + EPS) * g_ref[...]


def _final_norm(x1, y, g):
    t = x1.shape[0]
    row = pl.BlockSpec((FIN_TM, D_MODEL), lambda i: (i, 0))
    return pl.pallas_call(
        _final_norm_kernel,
        grid=(t // FIN_TM,),
        in_specs=[row, row, pl.BlockSpec((1, D_MODEL), lambda i: (0, 0))],
        out_specs=row,
        out_shape=jax.ShapeDtypeStruct((t, D_MODEL), _f32),
        compiler_params=_cparams(("parallel",)),
        name="final_norm",
    )(x1, y, g.reshape(1, D_MODEL))


def _cmp_to_sel_counts(n_rows, n_sel):
    r = SEL_LEN // CMP_STRIDE
    c = CMP_LEN // CMP_STRIDE
    off = np.arange(n_rows)[:, None] - r * np.arange(n_sel)[None, :] + (c - 1)
    counts = np.zeros((n_rows, LANES), np.float32)
    for n in range(c):
        counts[:, :n_sel] += ((off - n >= 0) & (off - n < r)).astype(np.float32)
    counts[n_rows - 1] = 0.0
    return counts


def _attention(x2d, b, s, w_in, norm_mix_g, rel_bias, lam_rows, subln_g,
               cmp_pos_k, cmp_pos_v, cmp_w1_k, cmp_w2_k, cmp_w1_v, cmp_w2_v):
    dq, dv, nq, kcvc, vs2, vw2, ng, dkT, ksT, kwT = _in_proj(x2d, norm_mix_g, *_arrange_w_in(w_in))
    to3 = lambda a: a.reshape(b, s, a.shape[-1])

    bt_diff = _near_tiles(_bias_tiles(rel_bias[:, :DIFF_HEADS], DIFF_TQ, DIFF_TQ, 2))
    o_diff = _diff_attn(to3(dq), dkT, to3(dv), bt_diff, lam_rows, subln_g)

    n = s // CMP_STRIDE
    a = kcvc.reshape(b, n, CMP_STRIDE, 2, NSA_KV_HEADS, NSA_HD)
    a = jnp.transpose(a, (0, 3, 4, 1, 2, 5)).reshape(b, 2, NSA_KV_HEADS, n, CMP_STRIDE * NSA_HD)
    pos = jnp.stack([cmp_pos_k, cmp_pos_v]).reshape(2, 2, 1, CMP_STRIDE * NSA_HD)
    w1 = jnp.stack([cmp_w1_k, cmp_w1_v]).astype(_bf16)
    w2 = jnp.stack([cmp_w2_k, cmp_w2_v])
    w2dup = jnp.concatenate([w2, w2], axis=-1).astype(_bf16)
    cmp_kv = _compress(a, pos, w1, w2dup)

    n_sel = s // SEL_LEN
    counts = jnp.asarray(_cmp_to_sel_counts(n, n_sel), _bf16)
    nq3 = to3(nq)
    o_cmp, sel = _cmp_attn(nq3, cmp_kv[:, 0], cmp_kv[:, 1], counts, n_sel)

    nsa_bias = rel_bias[:, DIFF_HEADS:]
    tq = NSA_TQ
    def group_tiles(n_tiles, window=None):
        bt = _bias_tiles(nsa_bias, tq, tq, n_tiles, window=window)
        bt = bt.reshape(NSA_KV_HEADS, NSA_GROUP, n_tiles, tq, tq)[:, np.asarray(GROUP_ORDER)]
        return jnp.transpose(bt, (0, 2, 1, 3, 4)).reshape(NSA_KV_HEADS, n_tiles, NSA_GROUP * tq, tq)

    expand = jnp.asarray(np.arange(s)[None, :] // SEL_LEN == np.arange(LANES)[:, None], _bf16)
    o_sel = _sel_attn(nq3, ksT, to3(vs2), sel, expand, _near_tiles(group_tiles(2)))

    bt_win = jnp.transpose(group_tiles(WIN_TILES, window=WINDOW), (0, 2, 1, 3))
    bt_win = bt_win.reshape(NSA_KV_HEADS, NSA_GROUP * tq, WIN_TILES * tq)
    o_win = _win_attn(nq3, kwT, to3(vw2), bt_win)

    flat = lambda a3: a3.reshape(b * s, a3.shape[-1])
    return flat(o_diff), flat(o_cmp), flat(o_sel), flat(o_win), ng


def _gate_expand():
    ge = np.zeros((3, LANES, NSA_WIDTH), np.float32)
    for head in range(NSA_HEADS):
        for br in range(3):
            ge[br, 3 * head + br, head * NSA_HD:(head + 1) * NSA_HD] = 1.0
    return jnp.asarray(ge, _bf16)


def _peer(x1, xn, qp, peer_sub_keys, peer_u, peer_v):
    t = x1.shape[0]
    sk = peer_sub_keys.reshape(2 * PEER_HEADS, PEER_NKEYS, PEER_DKEY // 2).astype(_bf16)
    rows, gate2 = _peer_route(qp, sk)
    rows_flat = rows.reshape(t * NPICK)
    diag, fold, expand, gsum = _peer_constants()
    h2 = _peer_up(rows_flat, xn.reshape(t, 8, LANES), _pair_table(peer_u), diag, fold, gsum)
    y = _peer_down(rows_flat, h2, gate2, _pair_table(peer_v), expand, diag)
    return y.reshape(t, D_MODEL)


def kernel(x, w_in, w_out, norm_mix_g, norm_ffn_g, final_norm_g, rel_bias, diff_lq1, diff_lk1, diff_lq2, diff_lk2, diff_subln_g, cmp_pos_k, cmp_pos_v, cmp_w1_k, cmp_w2_k, cmp_w1_v, cmp_w2_v, peer_w_q, peer_sub_keys, peer_u, peer_v):
    b, s, d = x.shape
    x2d = x.reshape(b * s, d)
    lam_rows = jnp.zeros((8, LANES), _f32).at[0:4, :DIFF_HD].set(
        jnp.stack([diff_lq1[0], diff_lk1[0], diff_lq2[0], diff_lk2[0]]))
    o_diff, o_cmp, o_sel, o_win, ng = _attention(
        x2d, b, s, w_in[0], norm_mix_g[0], rel_bias, lam_rows, diff_subln_g[0],
        cmp_pos_k[0], cmp_pos_v[0], cmp_w1_k[0], cmp_w2_k[0], cmp_w1_v[0], cmp_w2_v[0])
    x1, xn, qp = _mix_out(x2d, o_diff, o_cmp, o_sel, o_win, ng, _gate_expand(),
                          w_out[0].astype(_bf16), norm_ffn_g[0], peer_w_q[0].astype(_bf16))
    y = _peer(x1, xn, qp, peer_sub_keys[0], peer_u[0], peer_v[0])
    return _final_norm(x1, y, final_norm_g).reshape(b, s, d)
```

```python
import functools
import math

import numpy as np
import jax
import jax.numpy as jnp
from jax import lax
from jax.experimental import pallas as pl
from jax.experimental.pallas import tpu as pltpu

D_MODEL = 1024
DIFF_HEADS = 4
DIFF_HD = 64
DIFF_VD = 2 * DIFF_HD
DIFF_WIDTH = DIFF_HEADS * DIFF_VD
NSA_HEADS = 8
NSA_KV_HEADS = 2
NSA_GROUP = NSA_HEADS // NSA_KV_HEADS
NSA_HD = 64
NSA_WIDTH = NSA_HEADS * NSA_HD
CMP_LEN = 32
CMP_STRIDE = 16
CMP_HIDDEN = 256
SEL_LEN = 64
SEL_TOPN = 8
WINDOW = 512
FORCE_BONUS = 1000.0
NEG_INF = -1e30
N_BUCKETS = 32
MAX_DISTANCE = 128
PEER_HEADS = 8
PEER_NKEYS = 128
PEER_EXPERTS = PEER_NKEYS ** 2
PEER_DKEY = 256
PEER_TOPK = 16
EPS = 1e-6
LAMBDA_INIT = 0.8 - 0.6 * math.exp(-0.3 * 0)
LOG2E = math.log2(math.e)

KV_COLS = NSA_KV_HEADS * NSA_HD
SPLIT_SIZES = (DIFF_HEADS * 2 * DIFF_HD, DIFF_HEADS * 2 * DIFF_HD, DIFF_WIDTH, NSA_WIDTH,
               KV_COLS, KV_COLS, KV_COLS, KV_COLS, KV_COLS, KV_COLS, 3 * NSA_HEADS)
SPLIT_OFF = tuple(int(v) for v in np.concatenate([[0], np.cumsum(SPLIT_SIZES)]))

LANES = 128
VMEM_LIMIT = 56 * 1024 * 1024
HALF_EXPERTS = PEER_EXPERTS // 2
PAIR_ROWS = 16
PAIR_WORD_ROWS = PAIR_ROWS // 2

_f32 = jnp.float32
_bf16 = jnp.bfloat16


def _cparams(sem):
    return pltpu.CompilerParams(dimension_semantics=sem, vmem_limit_bytes=VMEM_LIMIT)


def _dot(a, b):
    return jnp.dot(a, b, preferred_element_type=_f32)


def _dot_nt(a, b):
    return lax.dot_general(a, b, (((1,), (1,)), ((), ())), preferred_element_type=_f32)


def _split_hi_lo(x):
    hi = x.astype(_bf16)
    lo = (x - hi.astype(_f32)).astype(_bf16)
    return hi, lo


def _gelu_tanh(x):
    c = math.sqrt(2.0 / math.pi)
    return 0.5 * x * (1.0 + jnp.tanh(c * (x + 0.044715 * (x * x * x))))


IN_TM = 512
_IN_GROUPS = (("dq", 512, _bf16), ("dv", 512, _bf16), ("nq", 512, _bf16), ("kcvc", 256, _f32),
              ("vs2", 256, _bf16), ("vw2", 256, _bf16), ("ng", 128, _f32))
_IN_KEYS_T = (("dkT", 512), ("ksT", 256), ("kwT", 256))


def _in_proj_kernel(x_ref, g_ref, w_ref, wt_ref, *out_refs):
    x = x_ref[...]
    y = x * lax.rsqrt(jnp.mean(x * x, axis=-1, keepdims=True) + EPS)
    h = (y * g_ref[...]).astype(_bf16)
    off = 0
    for (_, width, dt), o_ref in zip(_IN_GROUPS, out_refs):
        o_ref[...] = _dot(h, w_ref[:, off:off + width]).astype(dt)
        off += width
    off = 0
    for (_, width), o_ref in zip(_IN_KEYS_T, out_refs[len(_IN_GROUPS):]):
        o_ref[...] = _dot_nt(wt_ref[off:off + width, :], h).astype(o_ref.dtype)
        off += width


def _arrange_w_in(w_in):
    o = SPLIT_OFF
    sl = lambda i: w_in[:, o[i]:o[i + 1]]
    dup = lambda w: jnp.concatenate([w[:, :64], w[:, :64], w[:, 64:], w[:, 64:]], axis=1)
    ng = jnp.pad(sl(10), ((0, 0), (0, LANES - 3 * NSA_HEADS)))
    cols = [sl(0) * (DIFF_HD ** -0.5 * LOG2E), sl(2), sl(3) * (NSA_HD ** -0.5 * LOG2E),
            sl(4), sl(5), dup(sl(7)), dup(sl(9)), ng]
    keys = [sl(1), dup(sl(6)), dup(sl(8))]
    return (jnp.concatenate(cols, axis=1).astype(_bf16),
            jnp.concatenate(keys, axis=1).T.astype(_bf16))


def _in_proj(x2d, g, w_arr, wt_arr):
    t = x2d.shape[0]
    out_shape = ([jax.ShapeDtypeStruct((t, wd), dt) for _, wd, dt in _IN_GROUPS]
                 + [jax.ShapeDtypeStruct((wd, t), _bf16) for _, wd in _IN_KEYS_T])
    out_specs = ([pl.BlockSpec((IN_TM, wd), lambda i: (i, 0)) for _, wd, _ in _IN_GROUPS]
                 + [pl.BlockSpec((wd, IN_TM), lambda i: (0, i)) for _, wd in _IN_KEYS_T])
    return pl.pallas_call(
        _in_proj_kernel,
        grid=(t // IN_TM,),
        in_specs=[pl.BlockSpec((IN_TM, D_MODEL), lambda i: (i, 0)),
                  pl.BlockSpec((1, D_MODEL), lambda i: (0, 0)),
                  pl.BlockSpec(w_arr.shape, lambda i: (0, 0)),
                  pl.BlockSpec(wt_arr.shape, lambda i: (0, 0))],
        out_specs=out_specs,
        out_shape=out_shape,
        compiler_params=_cparams(("parallel",)),
        name="in_proj",
    )(x2d, g.reshape(1, D_MODEL), w_arr, wt_arr)


def _bucket_table(n):
    rel = np.arange(n)
    max_exact = N_BUCKETS // 2
    nf = np.maximum(rel, 1).astype(np.float32)
    large = max_exact + (np.log(nf / np.float32(max_exact)) / np.float32(math.log(MAX_DISTANCE / max_exact))
                         * np.float32(N_BUCKETS - max_exact)).astype(np.int32)
    large = np.minimum(large, N_BUCKETS - 1)
    return np.where(rel < max_exact, rel, large).astype(np.int32)


def _bias_tiles(rel_bias_heads, tq, tk, n_tiles, window=None):
    tab = (rel_bias_heads.astype(_f32) - rel_bias_heads[N_BUCKETS - 1][None, :].astype(_f32)) * LOG2E
    period = tq + tk
    m = np.arange(period)
    off = np.where(m < tk, -m, period - m)
    rel_w = np.arange(n_tiles)[:, None] * tq + off[None, :]
    bucket = _bucket_table(n_tiles * tq + period)[np.clip(rel_w, 0, None)]
    w = jnp.transpose(tab[bucket], (2, 0, 1))
    tiles = jnp.tile(w, (1, 1, tq))[..., :tq * (period - 1)]
    tiles = tiles.reshape(w.shape[0], n_tiles, tq, period - 1)[..., :tk]
    rel = (np.arange(n_tiles)[:, None, None] * tq + np.arange(tq)[None, :, None]
           - np.arange(tk)[None, None, :])
    ok = rel >= 0
    if window is not None:
        ok &= rel < window
    return jnp.where(jnp.asarray(ok)[None], tiles, NEG_INF)


def _near_tiles(bt):
    first = jnp.concatenate([bt[..., 0, :, :], jnp.full_like(bt[..., 0, :, :], NEG_INF)], axis=-1)
    later = jnp.concatenate([bt[..., 1, :, :], bt[..., 0, :, :]], axis=-1)
    return jnp.stack([first, later], axis=-3)


def _attend_first(s, v_aug):
    m = jnp.max(s, axis=-1, keepdims=True)
    return m, _dot(jnp.exp2(s - m).astype(_bf16), v_aug)


def _attend_step(carry, s, v_aug):
    m, acc = carry
    m_new = jnp.maximum(m, jnp.max(s, axis=-1, keepdims=True))
    acc = jnp.exp2(m - m_new) * acc + _dot(jnp.exp2(s - m_new).astype(_bf16), v_aug)
    return m_new, acc


def _lane_half_mask(shape):
    return lax.broadcasted_iota(jnp.int32, shape, len(shape) - 1) < (LANES // 2)


DIFF_TQ = 256


def _diff_attn_kernel(q_ref, k_ref, v_ref, bt_ref, lam_ref, g_ref, o_ref):
    tq = DIFF_TQ
    qb = pl.program_id(2)
    q = q_ref[...]
    lo = _lane_half_mask(q.shape)
    zero = jnp.zeros_like(q)
    q2 = jnp.concatenate([jnp.where(lo, q, zero), jnp.where(lo, zero, q)], axis=0)

    def kv(start, n):
        start = pl.multiple_of(start, tq)
        v = v_ref[pl.ds(start, n), :]
        return k_ref[:, pl.ds(start, n)], jnp.concatenate([v, jnp.ones_like(v)], axis=1)

    def both(b):
        return jnp.concatenate([b, b], axis=0)

    k, v = kv(jnp.maximum(qb - 1, 0) * tq, 2 * tq)
    carry = _attend_first(_dot(q2, k) + both(bt_ref[jnp.minimum(qb, 1)]), v)

    n_far = jnp.maximum(qb - 1, 0)

    def odd(c):
        k, v = kv((n_far - 1) * tq, tq)
        return _attend_step(c, _dot(q2, k), v)

    carry = lax.cond(n_far % 2 == 1, odd, lambda c: c, carry)

    def pair(i, c):
        k, v = kv(i * 2 * tq, 2 * tq)
        return _attend_step(c, _dot(q2, k), v)

    m, acc = lax.fori_loop(0, n_far // 2, pair, carry)
    o2 = acc[:, :DIFF_VD] / acc[:, DIFF_VD:]
    lv = lam_ref[...]
    lam = (jnp.exp(jnp.sum(lv[0:1] * lv[1:2], axis=-1, keepdims=True))
           - jnp.exp(jnp.sum(lv[2:3] * lv[3:4], axis=-1, keepdims=True)) + LAMBDA_INIT)
    o = o2[:tq] - lam * o2[tq:]
    y = o * lax.rsqrt(jnp.mean(o * o, axis=-1, keepdims=True) + EPS)
    o_ref[...] = (y * g_ref[...] * (1.0 - LAMBDA_INIT)).astype(o_ref.dtype)


def _diff_attn(dq, dkT, dv, bt, lam_rows, subln_g):
    b, s, _ = dq.shape
    tq = DIFF_TQ
    return pl.pallas_call(
        _diff_attn_kernel,
        grid=(b, DIFF_HEADS, s // tq),
        in_specs=[pl.BlockSpec((None, tq, LANES), lambda bi, h, qb: (bi, qb, h)),
                  pl.BlockSpec((LANES, s), lambda bi, h, qb: (h, bi)),
                  pl.BlockSpec((None, s, LANES), lambda bi, h, qb: (bi, 0, h)),
                  pl.BlockSpec((None, 2, tq, 2 * tq), lambda bi, h, qb: (h, 0, 0, 0)),
                  pl.BlockSpec((8, LANES), lambda bi, h, qb: (0, 0)),
                  pl.BlockSpec((1, LANES), lambda bi, h, qb: (0, 0))],
        out_specs=pl.BlockSpec((None, tq, LANES), lambda bi, h, qb: (bi, qb, h)),
        out_shape=jax.ShapeDtypeStruct((b, s, DIFF_WIDTH), _bf16),
        compiler_params=_cparams(("parallel", "parallel", "arbitrary")),
        name="diff_attn",
    )(dq, dkT, dv, bt, lam_rows, subln_g.reshape(1, LANES))


def _compress_kernel(a_ref, pos_ref, w1_ref, w2_ref, o_ref):
    a = a_ref[...]
    half = (CMP_LEN // 2) * NSA_HD
    za = _dot((a + pos_ref[0]).astype(_bf16), w1_ref[:half, :])
    zb = _dot((a + pos_ref[1]).astype(_bf16), w1_ref[half:, :])
    n = a.shape[0]
    hid = za + pltpu.roll(zb, n - 1, 0)
    o_ref[...] = _dot(_gelu_tanh(hid).astype(_bf16), w2_ref[...]).astype(o_ref.dtype)


def _compress(a, pos, w1, w2dup):
    b, _, _, n, width = a.shape
    return pl.pallas_call(
        _compress_kernel,
        grid=(b, 2, NSA_KV_HEADS),
        in_specs=[pl.BlockSpec((None, None, None, n, width), lambda bi, kv, g: (bi, kv, g, 0, 0)),
                  pl.BlockSpec((None, 2, 1, width), lambda bi, kv, g: (kv, 0, 0, 0)),
                  pl.BlockSpec((None, 2 * width, CMP_HIDDEN), lambda bi, kv, g: (kv, 0, 0)),
                  pl.BlockSpec((None, CMP_HIDDEN, LANES), lambda bi, kv, g: (kv, 0, 0))],
        out_specs=pl.BlockSpec((None, None, None, n, LANES), lambda bi, kv, g: (bi, kv, g, 0, 0)),
        out_shape=jax.ShapeDtypeStruct((b, 2, NSA_KV_HEADS, n, LANES), _bf16),
        compiler_params=_cparams(("parallel", "parallel", "parallel")),
        name="compress",
    )(a, pos, w1, w2dup)


NSA_TQ = 128


GROUP_ORDER = (0, 2, 1, 3)


def _stack_group_queries(q):
    rows = []
    for j in GROUP_ORDER:
        blk = q[:, (j // 2) * LANES:(j // 2 + 1) * LANES]
        lo = _lane_half_mask(blk.shape)
        keep = lo if j % 2 == 0 else jnp.logical_not(lo)
        rows.append(jnp.where(keep, blk, jnp.zeros_like(blk)))
    return jnp.concatenate(rows, axis=0)


def _unstack_group_outputs(o4, tq):
    lo = _lane_half_mask((tq, LANES))
    half = NSA_GROUP // 2
    pairs = [jnp.where(lo, o4[i * tq:(i + 1) * tq], o4[(half + i) * tq:(half + i + 1) * tq])
             for i in range(half)]
    return jnp.concatenate(pairs, axis=1)


def _group_values(v):
    lo = _lane_half_mask(v.shape)
    one = jnp.ones_like(v)
    return jnp.where(lo, v, one), jnp.where(lo, one, v)


def _group_pv(p, v_pair):
    n = p.shape[0] // 2
    p = p.astype(_bf16)
    return jnp.concatenate([_dot(p[:n], v_pair[0]), _dot(p[n:], v_pair[1])], axis=0)


def _group_attend_first(s, v_pair):
    m = jnp.max(s, axis=-1, keepdims=True)
    return m, _group_pv(jnp.exp2(s - m), v_pair)


def _group_attend_step(carry, s, v_pair):
    m, acc = carry
    m_new = jnp.maximum(m, jnp.max(s, axis=-1, keepdims=True))
    return m_new, jnp.exp2(m - m_new) * acc + _group_pv(jnp.exp2(s - m_new), v_pair)


def _group_normalise(acc):
    return acc / pltpu.roll(acc, LANES // 2, 1)


def _cmp_attn_kernel(q_ref, kc_ref, vc_ref, cs_ref, o_ref, sel_ref, *, n_sel):
    tq = NSA_TQ
    qb = pl.program_id(2)
    q4 = _stack_group_queries(q_ref[...])
    s = _dot_nt(q4, kc_ref[...])
    row = lax.broadcasted_iota(jnp.int32, s.shape, 0)
    pos4 = qb * tq + (row & (tq - 1))
    c = lax.broadcasted_iota(jnp.int32, s.shape, 1)
    ok = (c * CMP_STRIDE + (CMP_LEN - 1)) <= pos4
    s = jnp.where(ok, s, NEG_INF)
    m = jnp.max(s, axis=-1, keepdims=True)
    e = jnp.where(ok, jnp.exp2(s - m), 0.0)
    l = jnp.sum(e, axis=-1, keepdims=True)
    p = e / jnp.where(l > 0.0, l, 1.0)
    o_ref[...] = _unstack_group_outputs(_dot(p.astype(_bf16), vc_ref[...]), tq).astype(o_ref.dtype)

    psum = p[0:tq] + p[tq:2 * tq] + p[2 * tq:3 * tq] + p[3 * tq:4 * tq]
    hi, lo = _split_hi_lo(psum)
    imp = _dot(hi, cs_ref[...]) + _dot(lo, cs_ref[...])
    blk = lax.broadcasted_iota(jnp.int32, imp.shape, 1)
    pos = qb * tq + lax.broadcasted_iota(jnp.int32, imp.shape, 0)
    cur = pos // SEL_LEN
    forced = (blk == 0) | (blk == cur) | (blk == cur - 1)
    score = jnp.where(blk <= cur, imp + jnp.where(forced, FORCE_BONUS, 0.0), NEG_INF)
    st = score.T[:n_sel]
    blk_t = lax.broadcasted_iota(jnp.int32, st.shape, 0)
    rank = jnp.zeros(st.shape, jnp.int32)
    for i in range(n_sel):
        si = st[i:i + 1]
        beats = (si > st) | ((si == st) & (i < blk_t))
        rank = rank + beats.astype(jnp.int32)
    sel_t = ((rank < SEL_TOPN) & (st > 0.5 * NEG_INF)).astype(_f32)
    sel = jnp.concatenate([sel_t, jnp.zeros((LANES - n_sel, tq), _f32)], axis=0).T
    sel_ref[...] = sel.astype(sel_ref.dtype)


def _cmp_attn(nq, kcmp, vcmp, cmp_sel, n_sel):
    b, s, _ = nq.shape
    tq = NSA_TQ
    n = kcmp.shape[-2]
    gw = NSA_GROUP * NSA_HD
    return pl.pallas_call(
        functools.partial(_cmp_attn_kernel, n_sel=n_sel),
        grid=(b, NSA_KV_HEADS, s // tq),
        in_specs=[pl.BlockSpec((None, tq, gw), lambda bi, g, qb: (bi, qb, g)),
                  pl.BlockSpec((None, None, n, LANES), lambda bi, g, qb: (bi, g, 0, 0)),
                  pl.BlockSpec((None, None, n, LANES), lambda bi, g, qb: (bi, g, 0, 0)),
                  pl.BlockSpec((n, LANES), lambda bi, g, qb: (0, 0))],
        out_specs=[pl.BlockSpec((None, tq, gw), lambda bi, g, qb: (bi, qb, g)),
                   pl.BlockSpec((None, None, tq, LANES), lambda bi, g, qb: (bi, g, qb, 0))],
        out_shape=[jax.ShapeDtypeStruct((b, s, NSA_WIDTH), _bf16),
                   jax.ShapeDtypeStruct((b, NSA_KV_HEADS, s, LANES), _bf16)],
        compiler_params=_cparams(("parallel", "parallel", "arbitrary")),
        name="cmp_attn",
    )(nq, kcmp, vcmp, cmp_sel)


SEL_CHUNK = 4 * NSA_TQ


def _sel_attn_kernel(q_ref, k_ref, v_ref, sel_ref, ex_ref, bt_ref, o_ref):
    tq = NSA_TQ
    qb = pl.program_id(2)
    q4 = _stack_group_queries(q_ref[...])
    sel = sel_ref[...]

    def scores(pen_rows, start, n):
        start = pl.multiple_of(start, tq)
        kx = jnp.concatenate([k_ref[:, pl.ds(start, n)], ex_ref[:, pl.ds(start, n)]], axis=0)
        qx = jnp.concatenate([q4, jnp.concatenate([pen_rows] * NSA_GROUP, axis=0)], axis=1)
        return _dot(qx, kx), _group_values(v_ref[pl.ds(start, n), :])

    def penalty(sel_rows):
        return ((sel_rows.astype(_f32) - 1.0) * (-NEG_INF)).astype(_bf16)

    s, v = scores(penalty(sel), jnp.maximum(qb - 1, 0) * tq, 2 * tq)
    carry = _group_attend_first(s + bt_ref[jnp.minimum(qb, 1)], v)

    far_len = jnp.maximum(qb - 1, 0) * tq
    blk = lax.broadcasted_iota(jnp.int32, sel.shape, 1)
    pen_far = penalty(jnp.where(blk * SEL_LEN < far_len, sel, jnp.zeros_like(sel)))

    def far(i, c):
        s, v = scores(pen_far, i * SEL_CHUNK, SEL_CHUNK)
        return _group_attend_step(c, s, v)

    m, acc = lax.fori_loop(0, (far_len + SEL_CHUNK - 1) // SEL_CHUNK, far, carry)
    o_ref[...] = _unstack_group_outputs(_group_normalise(acc), tq).astype(o_ref.dtype)


def _sel_attn(nq, ks2, vs2, sel, expand, bt):
    b, s, _ = nq.shape
    tq = NSA_TQ
    gw = NSA_GROUP * NSA_HD
    return pl.pallas_call(
        _sel_attn_kernel,
        grid=(b, NSA_KV_HEADS, s // tq),
        in_specs=[pl.BlockSpec((None, tq, gw), lambda bi, g, qb: (bi, qb, g)),
                  pl.BlockSpec((LANES, s), lambda bi, g, qb: (g, bi)),
                  pl.BlockSpec((None, s, LANES), lambda bi, g, qb: (bi, 0, g)),
                  pl.BlockSpec((None, None, tq, LANES), lambda bi, g, qb: (bi, g, qb, 0)),
                  pl.BlockSpec((LANES, s), lambda bi, g, qb: (0, 0)),
                  pl.BlockSpec((None, 2, NSA_GROUP * tq, 2 * tq), lambda bi, g, qb: (g, 0, 0, 0))],
        out_specs=pl.BlockSpec((None, tq, gw), lambda bi, g, qb: (bi, qb, g)),
        out_shape=jax.ShapeDtypeStruct((b, s, NSA_WIDTH), _bf16),
        compiler_params=_cparams(("parallel", "parallel", "arbitrary")),
        name="sel_attn",
    )(nq, ks2, vs2, sel, expand, bt)


WIN_TILES = WINDOW // NSA_TQ + 1


def _win_attn_kernel(q_ref, k_ref, v_ref, bt_ref, o_ref):
    tq = NSA_TQ
    qb = pl.program_id(2)
    q4 = _stack_group_queries(q_ref[...])
    s_blocks, va, vb = [], [], []
    for d in range(WIN_TILES):
        start = pl.multiple_of(jnp.maximum(qb - d, 0) * tq, tq)
        s = _dot(q4, k_ref[:, pl.ds(start, tq)])
        s_blocks.append(s if d == 0 else s + jnp.where(qb >= d, 0.0, NEG_INF))
        a, b = _group_values(v_ref[pl.ds(start, tq), :])
        va.append(a)
        vb.append(b)
    s = jnp.concatenate(s_blocks, axis=1) + bt_ref[...]
    m, acc = _group_attend_first(s, (jnp.concatenate(va, axis=0), jnp.concatenate(vb, axis=0)))
    o_ref[...] = _unstack_group_outputs(_group_normalise(acc), tq).astype(o_ref.dtype)


def _win_attn(nq, kw2, vw2, bt):
    b, s, _ = nq.shape
    tq = NSA_TQ
    gw = NSA_GROUP * NSA_HD
    return pl.pallas_call(
        _win_attn_kernel,
        grid=(b, NSA_KV_HEADS, s // tq),
        in_specs=[pl.BlockSpec((None, tq, gw), lambda bi, g, qb: (bi, qb, g)),
                  pl.BlockSpec((LANES, s), lambda bi, g, qb: (g, bi)),
                  pl.BlockSpec((None, s, LANES), lambda bi, g, qb: (bi, 0, g)),
                  pl.BlockSpec((None, NSA_GROUP * tq, WIN_TILES * tq), lambda bi, g, qb: (g, 0, 0))],
        out_specs=pl.BlockSpec((None, tq, gw), lambda bi, g, qb: (bi, qb, g)),
        out_shape=jax.ShapeDtypeStruct((b, s, NSA_WIDTH), _bf16),
        compiler_params=_cparams(("parallel", "parallel", "arbitrary")),
        name="win_attn",
    )(nq, kw2, vw2, bt)


MIX_TM = 256


def _mix_out_kernel(x_ref, od_ref, oc_ref, os_ref, ow_ref, ng_ref, ge_ref, wo_ref, g_ref, wq_ref,
                    x1_ref, xn_ref, qp_ref):
    sig = jax.nn.sigmoid(ng_ref[...])
    hi, lo = _split_hi_lo(sig)
    o_nsa = jnp.zeros(oc_ref.shape, _f32)
    for br, o_ref in enumerate((oc_ref, os_ref, ow_ref)):
        gate = _dot(hi, ge_ref[br]) + _dot(lo, ge_ref[br])
        o_nsa = o_nsa + gate * o_ref[...].astype(_f32)
    y = _dot(od_ref[...], wo_ref[:DIFF_WIDTH, :]) + _dot(o_nsa.astype(_bf16), wo_ref[DIFF_WIDTH:, :])
    x1 = x_ref[...] + y
    x1_ref[...] = x1
    xn = x1 * lax.rsqrt(jnp.mean(x1 * x1, axis=-1, keepdims=True) + EPS) * g_ref[...]
    xn = xn.astype(_bf16)
    xn_ref[...] = xn
    qp_ref[...] = _dot(xn, wq_ref[...]).astype(qp_ref.dtype)


def _mix_out(x2d, o_diff, o_cmp, o_sel, o_win, ng, gate_expand, w_out, g_ffn, w_q):
    t = x2d.shape[0]
    tm = MIX_TM
    nq = w_q.shape[1]
    row = lambda w: pl.BlockSpec((tm, w), lambda i: (i, 0))
    full = lambda a: pl.BlockSpec(a.shape, lambda i: (0,) * a.ndim)
    g2 = g_ffn.reshape(1, D_MODEL)
    return pl.pallas_call(
        _mix_out_kernel,
        grid=(t // tm,),
        in_specs=[row(D_MODEL), row(DIFF_WIDTH), row(NSA_WIDTH), row(NSA_WIDTH), row(NSA_WIDTH),
                  row(LANES), full(gate_expand), full(w_out), full(g2), full(w_q)],
        out_specs=[row(D_MODEL), row(D_MODEL), row(nq)],
        out_shape=[jax.ShapeDtypeStruct((t, D_MODEL), _f32),
                   jax.ShapeDtypeStruct((t, D_MODEL), _bf16),
                   jax.ShapeDtypeStruct((t, nq), _bf16)],
        compiler_params=_cparams(("parallel",)),
        name="mix_out",
    )(x2d, o_diff, o_cmp, o_sel, o_win, ng, gate_expand, w_out, g2, w_q)


ROUTE_TM = 256
SLOTS = 2 * PEER_HEADS * PEER_TOPK


def _extract_step(st):
    sc, rows, n = st[0], st[1], st[2]
    m = jnp.max(sc, axis=0, keepdims=True)
    idx = jnp.min(jnp.where(sc == m, rows, n), axis=0, keepdims=True)
    st[3].append(m)
    st[4].append(idx)
    st[0] = jnp.where(rows == idx, -jnp.inf, sc)


def _route_head_steps(h, q_ref, sk_ref, flat_c, keep_c, rows_k, rowt_ref, gatet_ref):
    half_d = PEER_DKEY // 2
    col = pl.multiple_of(h * PEER_DKEY, PEER_DKEY)
    box = {}

    def start():
        q0 = q_ref[:, pl.ds(col, half_d)]
        q1 = q_ref[:, pl.ds(col + half_d, half_d)]
        box["a"] = [_dot_nt(sk_ref[2 * h], q0), rows_k, PEER_NKEYS, [], []]
        box["b"] = [_dot_nt(sk_ref[2 * h + 1], q1), rows_k, PEER_NKEYS, [], []]

    def first_stage():
        _extract_step(box["a"])
        _extract_step(box["b"])

    def pairs():
        sv0, si0 = jnp.concatenate(box["a"][3], axis=0), jnp.concatenate(box["a"][4], axis=0)
        sv1, si1 = jnp.concatenate(box["b"][3], axis=0), jnp.concatenate(box["b"][4], axis=0)
        slabs, ids = [], []
        for a, b0 in _PAIR_SLABS:
            if a is None:
                slabs.append(sv0[8:16] + sv1[0:1])
                ids.append(si0[8:16] * PEER_NKEYS + si1[0:1])
            else:
                slabs.append(sv0[a:a + 1] + sv1[b0:b0 + 8])
                ids.append(si0[a:a + 1] * PEER_NKEYS + si1[b0:b0 + 8])
        cand = jnp.where(keep_c, jnp.concatenate(slabs, axis=0), -jnp.inf)
        box["cidx"] = jnp.concatenate(ids, axis=0)
        box["c"] = [cand, flat_c, PEER_TOPK * PEER_TOPK, [], []]

    def second_stage():
        _extract_step(box["c"])

    def finish():
        cv, ci = jnp.concatenate(box["c"][3], axis=0), jnp.concatenate(box["c"][4], axis=0)
        e = jnp.concatenate(
            [jnp.sum(jnp.where(flat_c == ci[k:k + 1], box["cidx"], 0), axis=0, keepdims=True)
             for k in range(PEER_TOPK)], axis=0)
        ex = jnp.exp(cv - cv[0:1])
        gate = ex / jnp.sum(ex, axis=0, keepdims=True)
        upper = e >= HALF_EXPERTS
        base = pl.multiple_of(h * PEER_TOPK, PEER_TOPK)
        rowt_ref[pl.ds(base, PEER_TOPK), :] = (e & (HALF_EXPERTS - 1)) * PAIR_WORD_ROWS
        base2 = pl.multiple_of(h * 2 * PEER_TOPK, 2 * PEER_TOPK)
        gatet_ref[pl.ds(base2, PEER_TOPK), :] = jnp.where(upper, 0.0, gate)
        gatet_ref[pl.ds(base2 + PEER_TOPK, PEER_TOPK), :] = jnp.where(upper, gate, 0.0)

    return [start] + [first_stage] * PEER_TOPK + [pairs] + [second_stage] * PEER_TOPK + [finish]


def _route_head(h, q_ref, sk_ref, flat_c, keep_c, rows_k, rowt_ref, gatet_ref):
    for step in _route_head_steps(h, q_ref, sk_ref, flat_c, keep_c, rows_k, rowt_ref, gatet_ref):
        step()


def _route_consts(flat_ref, tm):
    rows_k = lax.broadcasted_iota(jnp.int32, (PEER_NKEYS, tm), 0)
    flat_c = jnp.broadcast_to(flat_ref[...], (flat_ref.shape[0], tm))
    return rows_k, flat_c, flat_c >= 0


def _peer_route_kernel(q_ref, sk_ref, flat_ref, row_ref, gate_ref, rowt_ref, gatet_ref):
    rows_k, flat_c, keep_c = _route_consts(flat_ref, ROUTE_TM)

    def head(h, _):
        _route_head(h, q_ref, sk_ref, flat_c, keep_c, rows_k, rowt_ref, gatet_ref)
        return 0

    lax.fori_loop(0, PEER_HEADS, head, 0)
    row_ref[...] = rowt_ref[...].T
    gate_ref[...] = gatet_ref[...].T


def _pair_slabs():
    slabs, flat = [(0, 0), (0, 8)] + [(a, 0) for a in range(1, 8)] + [(None, 0)], []
    for a, b0 in slabs:
        for i in range(8):
            aa, bb = (8 + i, 0) if a is None else (a, b0 + i)
            flat.append(aa * PEER_TOPK + bb if (aa + 1) * (bb + 1) <= PEER_TOPK else -1)
    return tuple(slabs), np.asarray(flat, np.int32).reshape(-1, 1)


_PAIR_SLABS, _PAIR_FLAT = _pair_slabs()


def _peer_route(qp, sk, n_tok):
    t = n_tok
    tm = ROUTE_TM
    npick = PEER_HEADS * PEER_TOPK
    return pl.pallas_call(
        _peer_route_kernel,
        grid=(t // tm,),
        in_specs=[pl.BlockSpec((tm, qp.shape[1]), lambda i: (i, 0)),
                  pl.BlockSpec(sk.shape, lambda i: (0, 0, 0)),
                  pl.BlockSpec(_PAIR_FLAT.shape, lambda i: (0, 0))],
        out_specs=[pl.BlockSpec((tm, npick), lambda i: (i, 0)),
                   pl.BlockSpec((tm, SLOTS), lambda i: (i, 0))],
        out_shape=[jax.ShapeDtypeStruct((t, npick), jnp.int32),
                   jax.ShapeDtypeStruct((t, SLOTS), _f32)],
        scratch_shapes=[pltpu.VMEM((npick, tm), jnp.int32), pltpu.VMEM((SLOTS, tm), _f32)],
        compiler_params=_cparams(("parallel",)),
        name="peer_route",
    )(qp, sk, jnp.asarray(_PAIR_FLAT))


PEER_TT = 64
N_OFFS = 8
PEER_UNROLL = 8
NPICK = PEER_HEADS * PEER_TOPK
CHUNK_ROWS = PEER_TOPK * PAIR_ROWS
STAGE_COLS = NPICK * PAIR_ROWS


def _pair_table(w):
    e = w.shape[0]
    w4 = w.astype(_bf16).reshape(2, e // 2, PAIR_WORD_ROWS, LANES)
    words = lax.bitcast_convert_type(jnp.transpose(w4, (1, 2, 3, 0)), jnp.int32)
    return words.reshape(e // 2 * PAIR_WORD_ROWS, LANES)


def _gather_head(tab_ref, tok_rows, c, offs):
    tiles = []
    n_off = len(offs)
    for k in range(PEER_TOPK):
        if k % n_off == 0:
            part_rows = tok_rows.at[pl.ds(c * PEER_TOPK + k, n_off)]
        r = pl.multiple_of(part_rows[offs[k % n_off]], PAIR_WORD_ROWS)
        tiles.append(pltpu.bitcast(tab_ref[pl.ds(r, PAIR_WORD_ROWS), :], _bf16))
    return jnp.concatenate(tiles, axis=0)


def _peer_up_kernel(row_ref, off_ref, x_ref, tab_ref, diag_ref, fold_ref, gsum_ref, h_ref, z_ref):
    offs = [off_ref[k] for k in range(N_OFFS)]

    def token(t, _):
        x8 = x_ref[t]
        tok_rows = row_ref.at[pl.ds(t * NPICK, NPICK)]
        for c in range(PEER_HEADS):
            rows = _gather_head(tab_ref, tok_rows, c, offs)
            z_ref[t, :, c * CHUNK_ROWS:(c + 1) * CHUNK_ROWS] = _dot_nt(x8, rows)
        return 0

    lax.fori_loop(0, PEER_TT, token, 0, unroll=PEER_UNROLL)
    zm = (z_ref[...] * diag_ref[...][None]).reshape(PEER_TT * 8, STAGE_COLS)
    part = _dot(zm.astype(_bf16), fold_ref[...])
    hi, lo = _split_hi_lo(part)
    h_ref[...] = _dot(gsum_ref[...], hi) + _dot(gsum_ref[...], lo)


def _peer_up(rows_flat, x8, tab, diag, fold, gsum, n_tok, x_off):
    t = n_tok
    tt = PEER_TT
    xo = x_off // tt
    full = lambda a: pl.BlockSpec(a.shape, lambda i: (0,) * a.ndim)
    return pl.pallas_call(
        _peer_up_kernel,
        grid=(t // tt,),
        in_specs=[pl.BlockSpec((tt * NPICK,), lambda i: (i,), memory_space=pltpu.SMEM),
                  pl.BlockSpec((N_OFFS,), lambda i: (0,), memory_space=pltpu.SMEM),
                  pl.BlockSpec((tt, 8, LANES), lambda i: (i + xo, 0, 0)),
                  pl.BlockSpec(tab.shape, lambda i: (0, 0), pipeline_mode=pl.Buffered(1)),
                  full(diag), full(fold), full(gsum)],
        out_specs=pl.BlockSpec((tt, SLOTS), lambda i: (i, 0)),
        out_shape=jax.ShapeDtypeStruct((t, SLOTS), _f32),
        scratch_shapes=[pltpu.VMEM((tt, 8, STAGE_COLS), _f32)],
        compiler_params=_cparams(("arbitrary",)),
        name="peer_up",
    )(rows_flat, jnp.arange(N_OFFS, dtype=jnp.int32), x8, tab, diag, fold, gsum)


FUSE_TT = 128
FUSE_INNER = FUSE_TT // PEER_HEADS


def _peer_up_route_kernel(row_ref, off_ref, x_ref, tab_ref, diag_ref, fold_ref, gsum_ref, q_ref, sk_ref, flat_ref,
                          h_ref, row2_ref, gate2_ref, z_ref, rowt_ref, gatet_ref):
    offs = [off_ref[k] for k in range(N_OFFS)]
    rows_k, flat_c, keep_c = _route_consts(flat_ref, FUSE_TT)

    def trip(j, _):
        steps = _route_head_steps(j, q_ref, sk_ref, flat_c, keep_c, rows_k, rowt_ref, gatet_ref)
        per = -(-len(steps) // FUSE_INNER)
        for i in range(FUSE_INNER):
            t = j * FUSE_INNER + i
            x8 = x_ref[t]
            tok_rows = row_ref.at[pl.ds(t * NPICK, NPICK)]
            for c in range(PEER_HEADS):
                rows = _gather_head(tab_ref, tok_rows, c, offs)
                z_ref[t, :, c * CHUNK_ROWS:(c + 1) * CHUNK_ROWS] = _dot_nt(x8, rows)
            for step in steps[i * per:(i + 1) * per]:
                step()
        return 0

    lax.fori_loop(0, PEER_HEADS, trip, 0)
    row2_ref[...] = rowt_ref[...].T
    gate2_ref[...] = gatet_ref[...].T
    zm = (z_ref[...] * diag_ref[...][None]).reshape(FUSE_TT * 8, STAGE_COLS)
    part = _dot(zm.astype(_bf16), fold_ref[...])
    hi, lo = _split_hi_lo(part)
    h_ref[...] = _dot(gsum_ref[...], hi) + _dot(gsum_ref[...], lo)


def _peer_up_route(rows_flat, x8, tab, diag, fold, gsum, qp, sk, n_tok, x_off, q_off):
    tt = FUSE_TT
    xo, qo = x_off // tt, q_off // tt
    full = lambda a: pl.BlockSpec(a.shape, lambda i: (0,) * a.ndim)
    flat = jnp.asarray(_PAIR_FLAT)
    return pl.pallas_call(
        _peer_up_route_kernel,
        grid=(n_tok // tt,),
        in_specs=[pl.BlockSpec((tt * NPICK,), lambda i: (i,), memory_space=pltpu.SMEM),
                  pl.BlockSpec((N_OFFS,), lambda i: (0,), memory_space=pltpu.SMEM),
                  pl.BlockSpec((tt, 8, LANES), lambda i: (i + xo, 0, 0)),
                  pl.BlockSpec(tab.shape, lambda i: (0, 0), pipeline_mode=pl.Buffered(1)),
                  full(diag), full(fold), full(gsum),
                  pl.BlockSpec((tt, qp.shape[1]), lambda i: (i + qo, 0)), full(sk), full(flat)],
        out_specs=[pl.BlockSpec((tt, SLOTS), lambda i: (i, 0)),
                   pl.BlockSpec((tt, NPICK), lambda i: (i, 0)),
                   pl.BlockSpec((tt, SLOTS), lambda i: (i, 0))],
        out_shape=[jax.ShapeDtypeStruct((n_tok, SLOTS), _f32),
                   jax.ShapeDtypeStruct((n_tok, NPICK), jnp.int32),
                   jax.ShapeDtypeStruct((n_tok, SLOTS), _f32)],
        scratch_shapes=[pltpu.VMEM((tt, 8, STAGE_COLS), _f32),
                        pltpu.VMEM((NPICK, tt), jnp.int32), pltpu.VMEM((SLOTS, tt), _f32)],
        compiler_params=_cparams(("arbitrary",)),
        name="peer_up_route",
    )(rows_flat, jnp.arange(N_OFFS, dtype=jnp.int32), x8, tab, diag, fold, gsum, qp, sk, flat)


def _peer_down_kernel(row_ref, off_ref, h_ref, gate_ref, tab_ref, ex_ref, diag_ref, y_ref, wexp_ref):
    offs = [off_ref[k] for k in range(N_OFFS)]
    w = _gelu_tanh(h_ref[...]) * gate_ref[...]
    hi, lo = _split_hi_lo(w)
    wexp_ref[...] = _dot(hi, ex_ref[...]) + _dot(lo, ex_ref[...])

    def token(t, _):
        a = jnp.broadcast_to(wexp_ref[pl.ds(t, 1), :], diag_ref.shape) * diag_ref[...]
        ahi = a.astype(_bf16).astype(_f32)
        a2 = jnp.concatenate([ahi, a - ahi], axis=0).astype(_bf16)
        acc = jnp.zeros((2 * 8, LANES), _f32)
        tok_rows = row_ref.at[pl.ds(t * NPICK, NPICK)]
        for c in range(PEER_HEADS):
            rows = _gather_head(tab_ref, tok_rows, c, offs)
            acc = acc + _dot(a2[:, c * CHUNK_ROWS:(c + 1) * CHUNK_ROWS], rows)
        y_ref[t] = acc[:8] + acc[8:]
        return 0

    lax.fori_loop(0, PEER_TT, token, 0, unroll=PEER_UNROLL)


def _peer_down(rows_flat, h2, gate2, tab, expand, diag):
    t = h2.shape[0]
    tt = PEER_TT
    full = lambda a: pl.BlockSpec(a.shape, lambda i: (0,) * a.ndim)
    return pl.pallas_call(
        _peer_down_kernel,
        grid=(t // tt,),
        in_specs=[pl.BlockSpec((tt * NPICK,), lambda i: (i,), memory_space=pltpu.SMEM),
                  pl.BlockSpec((N_OFFS,), lambda i: (0,), memory_space=pltpu.SMEM),
                  pl.BlockSpec((tt, SLOTS), lambda i: (i, 0)),
                  pl.BlockSpec((tt, SLOTS), lambda i: (i, 0)),
                  pl.BlockSpec(tab.shape, lambda i: (0, 0), pipeline_mode=pl.Buffered(1)),
                  full(expand), full(diag)],
        out_specs=pl.BlockSpec((tt, 8, LANES), lambda i: (i, 0, 0)),
        out_shape=jax.ShapeDtypeStruct((t, 8, LANES), _f32),
        scratch_shapes=[pltpu.VMEM((tt, STAGE_COLS), _f32)],
        compiler_params=_cparams(("arbitrary",)),
        name="peer_down",
    )(rows_flat, jnp.arange(N_OFFS, dtype=jnp.int32), h2, gate2, tab, expand, diag)


def _peer_constants():
    col = np.arange(STAGE_COLS)
    p, r = col // PAIR_ROWS, col % PAIR_ROWS
    slot = (p // PEER_TOPK) * 2 * PEER_TOPK + (r % 2) * PEER_TOPK + p % PEER_TOPK
    diag = (r[None, :] // 2 == np.arange(8)[:, None]).astype(np.float32)
    fold = (slot[:, None] == np.arange(SLOTS)[None, :]).astype(np.float32)
    gsum = lambda tt: jnp.asarray(np.arange(tt * 8)[None, :] // 8 == np.arange(tt)[:, None], _bf16)
    return (jnp.asarray(diag, _f32), jnp.asarray(fold, _bf16), jnp.asarray(fold.T, _bf16),
            gsum(PEER_TT), gsum(FUSE_TT))


FIN_TM = 512


def _final_norm_kernel(x_ref, y_ref, g_ref, o_ref):
    z = x_ref[...] + y_ref[...]
    o_ref[...] = z * lax.rsqrt(jnp.mean(z * z, axis=-1, keepdims=True) + EPS) * g_ref[...]


def _final_norm(x1, y, g):
    t = x1.shape[0]
    row = pl.BlockSpec((FIN_TM, D_MODEL), lambda i: (i, 0))
    return pl.pallas_call(
        _final_norm_kernel,
        grid=(t // FIN_TM,),
        in_specs=[row, row, pl.BlockSpec((1, D_MODEL), lambda i: (0, 0))],
        out_specs=row,
        out_shape=jax.ShapeDtypeStruct((t, D_MODEL), _f32),
        compiler_params=_cparams(("parallel",)),
        name="final_norm",
    )(x1, y, g.reshape(1, D_MODEL))


def _cmp_to_sel_counts(n_rows, n_sel):
    r = SEL_LEN // CMP_STRIDE
    c = CMP_LEN // CMP_STRIDE
    off = np.arange(n_rows)[:, None] - r * np.arange(n_sel)[None, :] + (c - 1)
    counts = np.zeros((n_rows, LANES), np.float32)
    for n in range(c):
        counts[:, :n_sel] += ((off - n >= 0) & (off - n < r)).astype(np.float32)
    counts[n_rows - 1] = 0.0
    return counts


def _attention(x2d, b, s, w_in, norm_mix_g, rel_bias, lam_rows, subln_g,
               cmp_pos_k, cmp_pos_v, cmp_w1_k, cmp_w2_k, cmp_w1_v, cmp_w2_v):
    dq, dv, nq, kcvc, vs2, vw2, ng, dkT, ksT, kwT = _in_proj(x2d, norm_mix_g, *_arrange_w_in(w_in))
    to3 = lambda a: a.reshape(b, s, a.shape[-1])

    bt_diff = _near_tiles(_bias_tiles(rel_bias[:, :DIFF_HEADS], DIFF_TQ, DIFF_TQ, 2))
    o_diff = _diff_attn(to3(dq), dkT, to3(dv), bt_diff, lam_rows, subln_g)

    n = s // CMP_STRIDE
    a = kcvc.reshape(b, n, CMP_STRIDE, 2, NSA_KV_HEADS, NSA_HD)
    a = jnp.transpose(a, (0, 3, 4, 1, 2, 5)).reshape(b, 2, NSA_KV_HEADS, n, CMP_STRIDE * NSA_HD)
    pos = jnp.stack([cmp_pos_k, cmp_pos_v]).reshape(2, 2, 1, CMP_STRIDE * NSA_HD)
    w1 = jnp.stack([cmp_w1_k, cmp_w1_v]).astype(_bf16)
    w2 = jnp.stack([cmp_w2_k, cmp_w2_v])
    w2dup = jnp.concatenate([w2, w2], axis=-1).astype(_bf16)
    cmp_kv = _compress(a, pos, w1, w2dup)

    n_sel = s // SEL_LEN
    counts = jnp.asarray(_cmp_to_sel_counts(n, n_sel), _bf16)
    nq3 = to3(nq)
    o_cmp, sel = _cmp_attn(nq3, cmp_kv[:, 0], cmp_kv[:, 1], counts, n_sel)

    nsa_bias = rel_bias[:, DIFF_HEADS:]
    tq = NSA_TQ
    def group_tiles(n_tiles, window=None):
        bt = _bias_tiles(nsa_bias, tq, tq, n_tiles, window=window)
        bt = bt.reshape(NSA_KV_HEADS, NSA_GROUP, n_tiles, tq, tq)[:, np.asarray(GROUP_ORDER)]
        return jnp.transpose(bt, (0, 2, 1, 3, 4)).reshape(NSA_KV_HEADS, n_tiles, NSA_GROUP * tq, tq)

    expand = jnp.asarray(np.arange(s)[None, :] // SEL_LEN == np.arange(LANES)[:, None], _bf16)
    o_sel = _sel_attn(nq3, ksT, to3(vs2), sel, expand, _near_tiles(group_tiles(2)))

    bt_win = jnp.transpose(group_tiles(WIN_TILES, window=WINDOW), (0, 2, 1, 3))
    bt_win = bt_win.reshape(NSA_KV_HEADS, NSA_GROUP * tq, WIN_TILES * tq)
    o_win = _win_attn(nq3, kwT, to3(vw2), bt_win)

    flat = lambda a3: a3.reshape(b * s, a3.shape[-1])
    return flat(o_diff), flat(o_cmp), flat(o_sel), flat(o_win), ng


def _gate_expand():
    ge = np.zeros((3, LANES, NSA_WIDTH), np.float32)
    for head in range(NSA_HEADS):
        for br in range(3):
            ge[br, 3 * head + br, head * NSA_HD:(head + 1) * NSA_HD] = 1.0
    return jnp.asarray(ge, _bf16)


PEER_PARTS = 4


def _peer(x1, xn, qp, peer_sub_keys, peer_u, peer_v):
    t = x1.shape[0]
    sk = peer_sub_keys.reshape(2 * PEER_HEADS, PEER_NKEYS, PEER_DKEY // 2).astype(_bf16)
    diag, fold, expand, gsum, gsum_fuse = _peer_constants()
    x8 = xn.reshape(t, 8, LANES)
    tab_u = _pair_table(peer_u)
    n = t // PEER_PARTS
    rows, gate = _peer_route(qp, sk, n)
    rows_all, gate_all, h_all = [rows], [gate], []
    for k in range(PEER_PARTS - 1):
        h, rows, gate = _peer_up_route(rows.reshape(n * NPICK), x8, tab_u, diag, fold, gsum_fuse, qp, sk,
                                       n, k * n, (k + 1) * n)
        h_all.append(h)
        rows_all.append(rows)
        gate_all.append(gate)
    h_all.append(_peer_up(rows.reshape(n * NPICK), x8, tab_u, diag, fold, gsum, n, (PEER_PARTS - 1) * n))
    rows_flat = jnp.concatenate(rows_all, axis=0).reshape(t * NPICK)
    y = _peer_down(rows_flat, jnp.concatenate(h_all, axis=0), jnp.concatenate(gate_all, axis=0),
                   _pair_table(peer_v), expand, diag)
    return y.reshape(t, D_MODEL)


def kernel(x, w_in, w_out, norm_mix_g, norm_ffn_g, final_norm_g, rel_bias, diff_lq1, diff_lk1, diff_lq2, diff_lk2, diff_subln_g, cmp_pos_k, cmp_pos_v, cmp_w1_k, cmp_w2_k, cmp_w1_v, cmp_w2_v, peer_w_q, peer_sub_keys, peer_u, peer_v):
    b, s, d = x.shape
    x2d = x.reshape(b * s, d)
    lam_rows = jnp.zeros((8, LANES), _f32).at[0:4, :DIFF_HD].set(
        jnp.stack([diff_lq1[0], diff_lk1[0], diff_lq2[0], diff_lk2[0]]))
    o_diff, o_cmp, o_sel, o_win, ng = _attention(
        x2d, b, s, w_in[0], norm_mix_g[0], rel_bias, lam_rows, diff_subln_g[0],
        cmp_pos_k[0], cmp_pos_v[0], cmp_w1_k[0], cmp_w2_k[0], cmp_w1_v[0], cmp_w2_v[0])
    x1, xn, qp = _mix_out(x2d, o_diff, o_cmp, o_sel, o_win, ng, _gate_expand(),
                          w_out[0].astype(_bf16), norm_ffn_g[0], peer_w_q[0].astype(_bf16))
    y = _peer(x1, xn, qp, peer_sub_keys[0], peer_u[0], peer_v[0])
    return _final_norm(x1, y, final_norm_g).reshape(b, s, d)
```

```python
import functools
import math

import numpy as np
import jax
import jax.numpy as jnp
from jax import lax
from jax.experimental import pallas as pl
from jax.experimental.pallas import tpu as pltpu

D_MODEL = 1024
DIFF_HEADS = 4
DIFF_HD = 64
DIFF_VD = 2 * DIFF_HD
DIFF_WIDTH = DIFF_HEADS * DIFF_VD
NSA_HEADS = 8
NSA_KV_HEADS = 2
NSA_GROUP = NSA_HEADS // NSA_KV_HEADS
NSA_HD = 64
NSA_WIDTH = NSA_HEADS * NSA_HD
CMP_LEN = 32
CMP_STRIDE = 16
CMP_HIDDEN = 256
SEL_LEN = 64
SEL_TOPN = 8
WINDOW = 512
FORCE_BONUS = 1000.0
NEG_INF = -1e30
N_BUCKETS = 32
MAX_DISTANCE = 128
PEER_HEADS = 8
PEER_NKEYS = 128
PEER_EXPERTS = PEER_NKEYS ** 2
PEER_DKEY = 256
PEER_TOPK = 16
EPS = 1e-6
LAMBDA_INIT = 0.8 - 0.6 * math.exp(-0.3 * 0)
LOG2E = math.log2(math.e)

KV_COLS = NSA_KV_HEADS * NSA_HD
SPLIT_SIZES = (DIFF_HEADS * 2 * DIFF_HD, DIFF_HEADS * 2 * DIFF_HD, DIFF_WIDTH, NSA_WIDTH,
               KV_COLS, KV_COLS, KV_COLS, KV_COLS, KV_COLS, KV_COLS, 3 * NSA_HEADS)
SPLIT_OFF = tuple(int(v) for v in np.concatenate([[0], np.cumsum(SPLIT_SIZES)]))

LANES = 128
VMEM_LIMIT = 56 * 1024 * 1024
HALF_EXPERTS = PEER_EXPERTS // 2
PAIR_ROWS = 16
PAIR_WORD_ROWS = PAIR_ROWS // 2

_f32 = jnp.float32
_bf16 = jnp.bfloat16


def _cparams(sem):
    return pltpu.CompilerParams(dimension_semantics=sem, vmem_limit_bytes=VMEM_LIMIT)


def _dot(a, b):
    return jnp.dot(a, b, preferred_element_type=_f32)


def _dot_nt(a, b):
    return lax.dot_general(a, b, (((1,), (1,)), ((), ())), preferred_element_type=_f32)


def _split_hi_lo(x):
    hi = x.astype(_bf16)
    lo = (x - hi.astype(_f32)).astype(_bf16)
    return hi, lo


def _gelu_tanh(x):
    c = math.sqrt(2.0 / math.pi)
    return 0.5 * x * (1.0 + jnp.tanh(c * (x + 0.044715 * (x * x * x))))


IN_TM = 512
_IN_GROUPS = (("dq", 512, _bf16), ("dv", 512, _bf16), ("nq", 512, _bf16), ("kcvc", 256, _f32),
              ("vs2", 256, _bf16), ("vw2", 256, _bf16), ("ng", 128, _f32))
_IN_KEYS_T = (("dkT", 512), ("ksT", 256), ("kwT", 256))


def _in_proj_kernel(x_ref, g_ref, w_ref, wt_ref, *out_refs):
    x = x_ref[...]
    y = x * lax.rsqrt(jnp.mean(x * x, axis=-1, keepdims=True) + EPS)
    h = (y * g_ref[...]).astype(_bf16)
    off = 0
    for (_, width, dt), o_ref in zip(_IN_GROUPS, out_refs):
        o_ref[...] = _dot(h, w_ref[:, off:off + width]).astype(dt)
        off += width
    off = 0
    for (_, width), o_ref in zip(_IN_KEYS_T, out_refs[len(_IN_GROUPS):]):
        o_ref[...] = _dot_nt(wt_ref[off:off + width, :], h).astype(o_ref.dtype)
        off += width


def _arrange_w_in(w_in):
    o = SPLIT_OFF
    sl = lambda i: w_in[:, o[i]:o[i + 1]]
    dup = lambda w: jnp.concatenate([w[:, :64], w[:, :64], w[:, 64:], w[:, 64:]], axis=1)
    ng = jnp.pad(sl(10), ((0, 0), (0, LANES - 3 * NSA_HEADS)))
    cols = [sl(0) * (DIFF_HD ** -0.5 * LOG2E), sl(2), sl(3) * (NSA_HD ** -0.5 * LOG2E),
            sl(4), sl(5), dup(sl(7)), dup(sl(9)), ng]
    keys = [sl(1), dup(sl(6)), dup(sl(8))]
    return (jnp.concatenate(cols, axis=1).astype(_bf16),
            jnp.concatenate(keys, axis=1).T.astype(_bf16))


def _in_proj(x2d, g, w_arr, wt_arr):
    t = x2d.shape[0]
    out_shape = ([jax.ShapeDtypeStruct((t, wd), dt) for _, wd, dt in _IN_GROUPS]
                 + [jax.ShapeDtypeStruct((wd, t), _bf16) for _, wd in _IN_KEYS_T])
    out_specs = ([pl.BlockSpec((IN_TM, wd), lambda i: (i, 0)) for _, wd, _ in _IN_GROUPS]
                 + [pl.BlockSpec((wd, IN_TM), lambda i: (0, i)) for _, wd in _IN_KEYS_T])
    return pl.pallas_call(
        _in_proj_kernel,
        grid=(t // IN_TM,),
        in_specs=[pl.BlockSpec((IN_TM, D_MODEL), lambda i: (i, 0)),
                  pl.BlockSpec((1, D_MODEL), lambda i: (0, 0)),
                  pl.BlockSpec(w_arr.shape, lambda i: (0, 0)),
                  pl.BlockSpec(wt_arr.shape, lambda i: (0, 0))],
        out_specs=out_specs,
        out_shape=out_shape,
        compiler_params=_cparams(("parallel",)),
        name="in_proj",
    )(x2d, g.reshape(1, D_MODEL), w_arr, wt_arr)


def _bucket_table(n):
    rel = np.arange(n)
    max_exact = N_BUCKETS // 2
    nf = np.maximum(rel, 1).astype(np.float32)
    large = max_exact + (np.log(nf / np.float32(max_exact)) / np.float32(math.log(MAX_DISTANCE / max_exact))
                         * np.float32(N_BUCKETS - max_exact)).astype(np.int32)
    large = np.minimum(large, N_BUCKETS - 1)
    return np.where(rel < max_exact, rel, large).astype(np.int32)


def _bias_tiles(rel_bias_heads, tq, tk, n_tiles, window=None):
    tab = (rel_bias_heads.astype(_f32) - rel_bias_heads[N_BUCKETS - 1][None, :].astype(_f32)) * LOG2E
    period = tq + tk
    m = np.arange(period)
    off = np.where(m < tk, -m, period - m)
    rel_w = np.arange(n_tiles)[:, None] * tq + off[None, :]
    bucket = _bucket_table(n_tiles * tq + period)[np.clip(rel_w, 0, None)]
    w = jnp.transpose(tab[bucket], (2, 0, 1))
    tiles = jnp.tile(w, (1, 1, tq))[..., :tq * (period - 1)]
    tiles = tiles.reshape(w.shape[0], n_tiles, tq, period - 1)[..., :tk]
    rel = (np.arange(n_tiles)[:, None, None] * tq + np.arange(tq)[None, :, None]
           - np.arange(tk)[None, None, :])
    ok = rel >= 0
    if window is not None:
        ok &= rel < window
    return jnp.where(jnp.asarray(ok)[None], tiles, NEG_INF)


def _near_tiles(bt):
    first = jnp.concatenate([bt[..., 0, :, :], jnp.full_like(bt[..., 0, :, :], NEG_INF)], axis=-1)
    later = jnp.concatenate([bt[..., 1, :, :], bt[..., 0, :, :]], axis=-1)
    return jnp.stack([first, later], axis=-3)


def _attend_first(s, v_aug):
    m = jnp.max(s, axis=-1, keepdims=True)
    return m, _dot(jnp.exp2(s - m).astype(_bf16), v_aug)


def _attend_step(carry, s, v_aug):
    m, acc = carry
    m_new = jnp.maximum(m, jnp.max(s, axis=-1, keepdims=True))
    acc = jnp.exp2(m - m_new) * acc + _dot(jnp.exp2(s - m_new).astype(_bf16), v_aug)
    return m_new, acc


def _lane_half_mask(shape):
    return lax.broadcasted_iota(jnp.int32, shape, len(shape) - 1) < (LANES // 2)


DIFF_TQ = 256


def _diff_attn_kernel(q_ref, k_ref, v_ref, bt_ref, lam_ref, g_ref, o_ref):
    tq = DIFF_TQ
    qb = pl.program_id(2)
    q = q_ref[...]
    lo = _lane_half_mask(q.shape)
    zero = jnp.zeros_like(q)
    q2 = jnp.concatenate([jnp.where(lo, q, zero), jnp.where(lo, zero, q)], axis=0)

    def kv(start, n):
        start = pl.multiple_of(start, tq)
        v = v_ref[pl.ds(start, n), :]
        return k_ref[:, pl.ds(start, n)], jnp.concatenate([v, jnp.ones_like(v)], axis=1)

    def both(b):
        return jnp.concatenate([b, b], axis=0)

    k, v = kv(jnp.maximum(qb - 1, 0) * tq, 2 * tq)
    carry = _attend_first(_dot(q2, k) + both(bt_ref[jnp.minimum(qb, 1)]), v)

    n_far = jnp.maximum(qb - 1, 0)

    def odd(c):
        k, v = kv((n_far - 1) * tq, tq)
        return _attend_step(c, _dot(q2, k), v)

    carry = lax.cond(n_far % 2 == 1, odd, lambda c: c, carry)

    def pair(i, c):
        k, v = kv(i * 2 * tq, 2 * tq)
        return _attend_step(c, _dot(q2, k), v)

    m, acc = lax.fori_loop(0, n_far // 2, pair, carry)
    o2 = acc[:, :DIFF_VD] / acc[:, DIFF_VD:]
    lv = lam_ref[...]
    lam = (jnp.exp(jnp.sum(lv[0:1] * lv[1:2], axis=-1, keepdims=True))
           - jnp.exp(jnp.sum(lv[2:3] * lv[3:4], axis=-1, keepdims=True)) + LAMBDA_INIT)
    o = o2[:tq] - lam * o2[tq:]
    y = o * lax.rsqrt(jnp.mean(o * o, axis=-1, keepdims=True) + EPS)
    o_ref[...] = (y * g_ref[...] * (1.0 - LAMBDA_INIT)).astype(o_ref.dtype)


def _diff_attn(dq, dkT, dv, bt, lam_rows, subln_g):
    b, s, _ = dq.shape
    tq = DIFF_TQ
    return pl.pallas_call(
        _diff_attn_kernel,
        grid=(b, DIFF_HEADS, s // tq),
        in_specs=[pl.BlockSpec((None, tq, LANES), lambda bi, h, qb: (bi, qb, h)),
                  pl.BlockSpec((LANES, s), lambda bi, h, qb: (h, bi)),
                  pl.BlockSpec((None, s, LANES), lambda bi, h, qb: (bi, 0, h)),
                  pl.BlockSpec((None, 2, tq, 2 * tq), lambda bi, h, qb: (h, 0, 0, 0)),
                  pl.BlockSpec((8, LANES), lambda bi, h, qb: (0, 0)),
                  pl.BlockSpec((1, LANES), lambda bi, h, qb: (0, 0))],
        out_specs=pl.BlockSpec((None, tq, LANES), lambda bi, h, qb: (bi, qb, h)),
        out_shape=jax.ShapeDtypeStruct((b, s, DIFF_WIDTH), _bf16),
        compiler_params=_cparams(("parallel", "parallel", "arbitrary")),
        name="diff_attn",
    )(dq, dkT, dv, bt, lam_rows, subln_g.reshape(1, LANES))


def _compress_kernel(a_ref, pos_ref, w1_ref, w2_ref, o_ref):
    a = a_ref[...]
    half = (CMP_LEN // 2) * NSA_HD
    za = _dot((a + pos_ref[0]).astype(_bf16), w1_ref[:half, :])
    zb = _dot((a + pos_ref[1]).astype(_bf16), w1_ref[half:, :])
    n = a.shape[0]
    hid = za + pltpu.roll(zb, n - 1, 0)
    o_ref[...] = _dot(_gelu_tanh(hid).astype(_bf16), w2_ref[...]).astype(o_ref.dtype)


def _compress(a, pos, w1, w2dup):
    b, _, _, n, width = a.shape
    return pl.pallas_call(
        _compress_kernel,
        grid=(b, 2, NSA_KV_HEADS),
        in_specs=[pl.BlockSpec((None, None, None, n, width), lambda bi, kv, g: (bi, kv, g, 0, 0)),
                  pl.BlockSpec((None, 2, 1, width), lambda bi, kv, g: (kv, 0, 0, 0)),
                  pl.BlockSpec((None, 2 * width, CMP_HIDDEN), lambda bi, kv, g: (kv, 0, 0)),
                  pl.BlockSpec((None, CMP_HIDDEN, LANES), lambda bi, kv, g: (kv, 0, 0))],
        out_specs=pl.BlockSpec((None, None, None, n, LANES), lambda bi, kv, g: (bi, kv, g, 0, 0)),
        out_shape=jax.ShapeDtypeStruct((b, 2, NSA_KV_HEADS, n, LANES), _bf16),
        compiler_params=_cparams(("parallel", "parallel", "parallel")),
        name="compress",
    )(a, pos, w1, w2dup)


NSA_TQ = 256


GROUP_ORDER = (0, 2, 1, 3)


def _stack_group_queries(q):
    rows = []
    for j in GROUP_ORDER:
        blk = q[:, (j // 2) * LANES:(j // 2 + 1) * LANES]
        lo = _lane_half_mask(blk.shape)
        keep = lo if j % 2 == 0 else jnp.logical_not(lo)
        rows.append(jnp.where(keep, blk, jnp.zeros_like(blk)))
    return jnp.concatenate(rows, axis=0)


def _unstack_group_outputs(o4, tq):
    lo = _lane_half_mask((tq, LANES))
    half = NSA_GROUP // 2
    pairs = [jnp.where(lo, o4[i * tq:(i + 1) * tq], o4[(half + i) * tq:(half + i + 1) * tq])
             for i in range(half)]
    return jnp.concatenate(pairs, axis=1)


def _group_values(v):
    lo = _lane_half_mask(v.shape)
    one = jnp.ones_like(v)
    return jnp.where(lo, v, one), jnp.where(lo, one, v)


def _group_pv(p, v_pair):
    n = p.shape[0] // 2
    p = p.astype(_bf16)
    return jnp.concatenate([_dot(p[:n], v_pair[0]), _dot(p[n:], v_pair[1])], axis=0)


def _group_attend_first(s, v_pair):
    m = jnp.max(s, axis=-1, keepdims=True)
    return m, _group_pv(jnp.exp2(s - m), v_pair)


def _group_attend_step(carry, s, v_pair):
    m, acc = carry
    m_new = jnp.maximum(m, jnp.max(s, axis=-1, keepdims=True))
    return m_new, jnp.exp2(m - m_new) * acc + _group_pv(jnp.exp2(s - m_new), v_pair)


def _group_normalise(acc):
    return acc / pltpu.roll(acc, LANES // 2, 1)


def _cmp_attn_kernel(q_ref, kc_ref, vc_ref, cs_ref, o_ref, sel_ref, *, n_sel):
    tq = NSA_TQ
    qb = pl.program_id(2)
    q4 = _stack_group_queries(q_ref[...])
    s = _dot_nt(q4, kc_ref[...])
    row = lax.broadcasted_iota(jnp.int32, s.shape, 0)
    pos4 = qb * tq + (row & (tq - 1))
    c = lax.broadcasted_iota(jnp.int32, s.shape, 1)
    ok = (c * CMP_STRIDE + (CMP_LEN - 1)) <= pos4
    s = jnp.where(ok, s, NEG_INF)
    m = jnp.max(s, axis=-1, keepdims=True)
    e = jnp.where(ok, jnp.exp2(s - m), 0.0)
    l = jnp.sum(e, axis=-1, keepdims=True)
    p = e / jnp.where(l > 0.0, l, 1.0)
    o_ref[...] = _unstack_group_outputs(_dot(p.astype(_bf16), vc_ref[...]), tq).astype(o_ref.dtype)

    psum = p[0:tq] + p[tq:2 * tq] + p[2 * tq:3 * tq] + p[3 * tq:4 * tq]
    hi, lo = _split_hi_lo(psum)
    imp = _dot(hi, cs_ref[...]) + _dot(lo, cs_ref[...])
    blk = lax.broadcasted_iota(jnp.int32, imp.shape, 1)
    pos = qb * tq + lax.broadcasted_iota(jnp.int32, imp.shape, 0)
    cur = pos // SEL_LEN
    forced = (blk == 0) | (blk == cur) | (blk == cur - 1)
    score = jnp.where(blk <= cur, imp + jnp.where(forced, FORCE_BONUS, 0.0), NEG_INF)
    st = score.T[:n_sel]
    blk_t = lax.broadcasted_iota(jnp.int32, st.shape, 0)
    rank = jnp.zeros(st.shape, jnp.int32)
    for i in range(n_sel):
        si = st[i:i + 1]
        beats = (si > st) | ((si == st) & (i < blk_t))
        rank = rank + beats.astype(jnp.int32)
    sel_t = ((rank < SEL_TOPN) & (st > 0.5 * NEG_INF)).astype(_f32)
    sel = jnp.concatenate([sel_t, jnp.zeros((LANES - n_sel, tq), _f32)], axis=0).T
    sel_ref[...] = sel.astype(sel_ref.dtype)


def _cmp_attn(nq, kcmp, vcmp, cmp_sel, n_sel):
    b, s, _ = nq.shape
    tq = NSA_TQ
    n = kcmp.shape[-2]
    gw = NSA_GROUP * NSA_HD
    return pl.pallas_call(
        functools.partial(_cmp_attn_kernel, n_sel=n_sel),
        grid=(b, NSA_KV_HEADS, s // tq),
        in_specs=[pl.BlockSpec((None, tq, gw), lambda bi, g, qb: (bi, qb, g)),
                  pl.BlockSpec((None, None, n, LANES), lambda bi, g, qb: (bi, g, 0, 0)),
                  pl.BlockSpec((None, None, n, LANES), lambda bi, g, qb: (bi, g, 0, 0)),
                  pl.BlockSpec((n, LANES), lambda bi, g, qb: (0, 0))],
        out_specs=[pl.BlockSpec((None, tq, gw), lambda bi, g, qb: (bi, qb, g)),
                   pl.BlockSpec((None, None, tq, LANES), lambda bi, g, qb: (bi, g, qb, 0))],
        out_shape=[jax.ShapeDtypeStruct((b, s, NSA_WIDTH), _bf16),
                   jax.ShapeDtypeStruct((b, NSA_KV_HEADS, s, LANES), _bf16)],
        compiler_params=_cparams(("parallel", "parallel", "arbitrary")),
        name="cmp_attn",
    )(nq, kcmp, vcmp, cmp_sel)


SEL_CHUNK = 4 * NSA_TQ


def _sel_attn_kernel(q_ref, k_ref, v_ref, sel_ref, ex_ref, bt_ref, o_ref):
    tq = NSA_TQ
    qb = pl.program_id(2)
    q4 = _stack_group_queries(q_ref[...])
    sel = sel_ref[...]

    def scores(pen_rows, start, n):
        start = pl.multiple_of(start, tq)
        kx = jnp.concatenate([k_ref[:, pl.ds(start, n)], ex_ref[:, pl.ds(start, n)]], axis=0)
        qx = jnp.concatenate([q4, jnp.concatenate([pen_rows] * NSA_GROUP, axis=0)], axis=1)
        return _dot(qx, kx), _group_values(v_ref[pl.ds(start, n), :])

    def penalty(sel_rows):
        return ((sel_rows.astype(_f32) - 1.0) * (-NEG_INF)).astype(_bf16)

    s, v = scores(penalty(sel), jnp.maximum(qb - 1, 0) * tq, 2 * tq)
    carry = _group_attend_first(s + bt_ref[jnp.minimum(qb, 1)], v)

    far_len = jnp.maximum(qb - 1, 0) * tq
    blk = lax.broadcasted_iota(jnp.int32, sel.shape, 1)
    pen_far = penalty(jnp.where(blk * SEL_LEN < far_len, sel, jnp.zeros_like(sel)))

    def far(i, c):
        s, v = scores(pen_far, i * SEL_CHUNK, SEL_CHUNK)
        return _group_attend_step(c, s, v)

    m, acc = lax.fori_loop(0, (far_len + SEL_CHUNK - 1) // SEL_CHUNK, far, carry)
    o_ref[...] = _unstack_group_outputs(_group_normalise(acc), tq).astype(o_ref.dtype)


def _sel_attn(nq, ks2, vs2, sel, expand, bt):
    b, s, _ = nq.shape
    tq = NSA_TQ
    gw = NSA_GROUP * NSA_HD
    return pl.pallas_call(
        _sel_attn_kernel,
        grid=(b, NSA_KV_HEADS, s // tq),
        in_specs=[pl.BlockSpec((None, tq, gw), lambda bi, g, qb: (bi, qb, g)),
                  pl.BlockSpec((LANES, s), lambda bi, g, qb: (g, bi)),
                  pl.BlockSpec((None, s, LANES), lambda bi, g, qb: (bi, 0, g)),
                  pl.BlockSpec((None, None, tq, LANES), lambda bi, g, qb: (bi, g, qb, 0)),
                  pl.BlockSpec((LANES, s), lambda bi, g, qb: (0, 0)),
                  pl.BlockSpec((None, 2, NSA_GROUP * tq, 2 * tq), lambda bi, g, qb: (g, 0, 0, 0))],
        out_specs=pl.BlockSpec((None, tq, gw), lambda bi, g, qb: (bi, qb, g)),
        out_shape=jax.ShapeDtypeStruct((b, s, NSA_WIDTH), _bf16),
        compiler_params=_cparams(("parallel", "parallel", "arbitrary")),
        name="sel_attn",
    )(nq, ks2, vs2, sel, expand, bt)


WIN_TILES = WINDOW // NSA_TQ + 1


def _win_attn_kernel(q_ref, k_ref, v_ref, bt_ref, o_ref):
    tq = NSA_TQ
    qb = pl.program_id(2)
    q4 = _stack_group_queries(q_ref[...])
    s_blocks, va, vb = [], [], []
    for d in range(WIN_TILES):
        start = pl.multiple_of(jnp.maximum(qb - d, 0) * tq, tq)
        s = _dot(q4, k_ref[:, pl.ds(start, tq)])
        s_blocks.append(s if d == 0 else s + jnp.where(qb >= d, 0.0, NEG_INF))
        a, b = _group_values(v_ref[pl.ds(start, tq), :])
        va.append(a)
        vb.append(b)
    s = jnp.concatenate(s_blocks, axis=1) + bt_ref[...]
    m, acc = _group_attend_first(s, (jnp.concatenate(va, axis=0), jnp.concatenate(vb, axis=0)))
    o_ref[...] = _unstack_group_outputs(_group_normalise(acc), tq).astype(o_ref.dtype)


def _win_attn(nq, kw2, vw2, bt):
    b, s, _ = nq.shape
    tq = NSA_TQ
    gw = NSA_GROUP * NSA_HD
    return pl.pallas_call(
        _win_attn_kernel,
        grid=(b, NSA_KV_HEADS, s // tq),
        in_specs=[pl.BlockSpec((None, tq, gw), lambda bi, g, qb: (bi, qb, g)),
                  pl.BlockSpec((LANES, s), lambda bi, g, qb: (g, bi)),
                  pl.BlockSpec((None, s, LANES), lambda bi, g, qb: (bi, 0, g)),
                  pl.BlockSpec((None, NSA_GROUP * tq, WIN_TILES * tq), lambda bi, g, qb: (g, 0, 0))],
        out_specs=pl.BlockSpec((None, tq, gw), lambda bi, g, qb: (bi, qb, g)),
        out_shape=jax.ShapeDtypeStruct((b, s, NSA_WIDTH), _bf16),
        compiler_params=_cparams(("parallel", "parallel", "arbitrary")),
        name="win_attn",
    )(nq, kw2, vw2, bt)


MIX_TM = 256


def _mix_out_kernel(x_ref, od_ref, oc_ref, os_ref, ow_ref, ng_ref, ge_ref, wo_ref, g_ref, wq_ref,
                    x1_ref, xn_ref, qp_ref):
    sig = jax.nn.sigmoid(ng_ref[...])
    hi, lo = _split_hi_lo(sig)
    o_nsa = jnp.zeros(oc_ref.shape, _f32)
    for br, o_ref in enumerate((oc_ref, os_ref, ow_ref)):
        gate = _dot(hi, ge_ref[br]) + _dot(lo, ge_ref[br])
        o_nsa = o_nsa + gate * o_ref[...].astype(_f32)
    y = _dot(od_ref[...], wo_ref[:DIFF_WIDTH, :]) + _dot(o_nsa.astype(_bf16), wo_ref[DIFF_WIDTH:, :])
    x1 = x_ref[...] + y
    x1_ref[...] = x1
    xn = x1 * lax.rsqrt(jnp.mean(x1 * x1, axis=-1, keepdims=True) + EPS) * g_ref[...]
    xn = xn.astype(_bf16)
    xn_ref[...] = xn
    qp_ref[...] = _dot(xn, wq_ref[...]).astype(qp_ref.dtype)


def _mix_out(x2d, o_diff, o_cmp, o_sel, o_win, ng, gate_expand, w_out, g_ffn, w_q):
    t = x2d.shape[0]
    tm = MIX_TM
    nq = w_q.shape[1]
    row = lambda w: pl.BlockSpec((tm, w), lambda i: (i, 0))
    full = lambda a: pl.BlockSpec(a.shape, lambda i: (0,) * a.ndim)
    g2 = g_ffn.reshape(1, D_MODEL)
    return pl.pallas_call(
        _mix_out_kernel,
        grid=(t // tm,),
        in_specs=[row(D_MODEL), row(DIFF_WIDTH), row(NSA_WIDTH), row(NSA_WIDTH), row(NSA_WIDTH),
                  row(LANES), full(gate_expand), full(w_out), full(g2), full(w_q)],
        out_specs=[row(D_MODEL), row(D_MODEL), row(nq)],
        out_shape=[jax.ShapeDtypeStruct((t, D_MODEL), _f32),
                   jax.ShapeDtypeStruct((t, D_MODEL), _bf16),
                   jax.ShapeDtypeStruct((t, nq), _bf16)],
        compiler_params=_cparams(("parallel",)),
        name="mix_out",
    )(x2d, o_diff, o_cmp, o_sel, o_win, ng, gate_expand, w_out, g2, w_q)


ROUTE_TM = 256
SLOTS = 2 * PEER_HEADS * PEER_TOPK


def _extract_step(st):
    sc, rows, n = st[0], st[1], st[2]
    m = jnp.max(sc, axis=0, keepdims=True)
    idx = jnp.min(jnp.where(sc == m, rows, n), axis=0, keepdims=True)
    st[3].append(m)
    st[4].append(idx)
    st[0] = jnp.where(rows == idx, -jnp.inf, sc)


def _route_head_steps(h, q_ref, sk_ref, flat_c, keep_c, rows_k, rowt_ref, gatet_ref):
    half_d = PEER_DKEY // 2
    col = pl.multiple_of(h * PEER_DKEY, PEER_DKEY)
    box = {}

    def start():
        q0 = q_ref[:, pl.ds(col, half_d)]
        q1 = q_ref[:, pl.ds(col + half_d, half_d)]
        box["a"] = [_dot_nt(sk_ref[2 * h], q0), rows_k, PEER_NKEYS, [], []]
        box["b"] = [_dot_nt(sk_ref[2 * h + 1], q1), rows_k, PEER_NKEYS, [], []]

    def first_stage():
        _extract_step(box["a"])
        _extract_step(box["b"])

    def pairs():
        sv0, si0 = jnp.concatenate(box["a"][3], axis=0), jnp.concatenate(box["a"][4], axis=0)
        sv1, si1 = jnp.concatenate(box["b"][3], axis=0), jnp.concatenate(box["b"][4], axis=0)
        slabs, ids = [], []
        for a, b0 in _PAIR_SLABS:
            if a is None:
                slabs.append(sv0[8:16] + sv1[0:1])
                ids.append(si0[8:16] * PEER_NKEYS + si1[0:1])
            else:
                slabs.append(sv0[a:a + 1] + sv1[b0:b0 + 8])
                ids.append(si0[a:a + 1] * PEER_NKEYS + si1[b0:b0 + 8])
        cand = jnp.where(keep_c, jnp.concatenate(slabs, axis=0), -jnp.inf)
        box["cidx"] = jnp.concatenate(ids, axis=0)
        box["c"] = [cand, flat_c, PEER_TOPK * PEER_TOPK, [], []]

    def second_stage():
        _extract_step(box["c"])

    def finish():
        cv, ci = jnp.concatenate(box["c"][3], axis=0), jnp.concatenate(box["c"][4], axis=0)
        e = jnp.concatenate(
            [jnp.sum(jnp.where(flat_c == ci[k:k + 1], box["cidx"], 0), axis=0, keepdims=True)
             for k in range(PEER_TOPK)], axis=0)
        ex = jnp.exp(cv - cv[0:1])
        gate = ex / jnp.sum(ex, axis=0, keepdims=True)
        upper = e >= HALF_EXPERTS
        base = pl.multiple_of(h * PEER_TOPK, PEER_TOPK)
        rowt_ref[pl.ds(base, PEER_TOPK), :] = (e & (HALF_EXPERTS - 1)) * PAIR_WORD_ROWS
        base2 = pl.multiple_of(h * 2 * PEER_TOPK, 2 * PEER_TOPK)
        gatet_ref[pl.ds(base2, PEER_TOPK), :] = jnp.where(upper, 0.0, gate)
        gatet_ref[pl.ds(base2 + PEER_TOPK, PEER_TOPK), :] = jnp.where(upper, gate, 0.0)

    return [start] + [first_stage] * PEER_TOPK + [pairs] + [second_stage] * PEER_TOPK + [finish]


def _route_head(h, q_ref, sk_ref, flat_c, keep_c, rows_k, rowt_ref, gatet_ref):
    for step in _route_head_steps(h, q_ref, sk_ref, flat_c, keep_c, rows_k, rowt_ref, gatet_ref):
        step()


def _route_consts(flat_ref, tm):
    rows_k = lax.broadcasted_iota(jnp.int32, (PEER_NKEYS, tm), 0)
    flat_c = jnp.broadcast_to(flat_ref[...], (flat_ref.shape[0], tm))
    return rows_k, flat_c, flat_c >= 0


def _peer_route_kernel(q_ref, sk_ref, flat_ref, row_ref, gate_ref, rowt_ref, gatet_ref):
    rows_k, flat_c, keep_c = _route_consts(flat_ref, ROUTE_TM)

    def head(h, _):
        _route_head(h, q_ref, sk_ref, flat_c, keep_c, rows_k, rowt_ref, gatet_ref)
        return 0

    lax.fori_loop(0, PEER_HEADS, head, 0)
    row_ref[...] = rowt_ref[...].T
    gate_ref[...] = gatet_ref[...].T


def _pair_slabs():
    slabs, flat = [(0, 0), (0, 8)] + [(a, 0) for a in range(1, 8)] + [(None, 0)], []
    for a, b0 in slabs:
        for i in range(8):
            aa, bb = (8 + i, 0) if a is None else (a, b0 + i)
            flat.append(aa * PEER_TOPK + bb if (aa + 1) * (bb + 1) <= PEER_TOPK else -1)
    return tuple(slabs), np.asarray(flat, np.int32).reshape(-1, 1)


_PAIR_SLABS, _PAIR_FLAT = _pair_slabs()


def _peer_route(qp, sk, n_tok):
    t = n_tok
    tm = ROUTE_TM
    npick = PEER_HEADS * PEER_TOPK
    return pl.pallas_call(
        _peer_route_kernel,
        grid=(t // tm,),
        in_specs=[pl.BlockSpec((tm, qp.shape[1]), lambda i: (i, 0)),
                  pl.BlockSpec(sk.shape, lambda i: (0, 0, 0)),
                  pl.BlockSpec(_PAIR_FLAT.shape, lambda i: (0, 0))],
        out_specs=[pl.BlockSpec((tm, npick), lambda i: (i, 0)),
                   pl.BlockSpec((tm, SLOTS), lambda i: (i, 0))],
        out_shape=[jax.ShapeDtypeStruct((t, npick), jnp.int32),
                   jax.ShapeDtypeStruct((t, SLOTS), _f32)],
        scratch_shapes=[pltpu.VMEM((npick, tm), jnp.int32), pltpu.VMEM((SLOTS, tm), _f32)],
        compiler_params=_cparams(("parallel",)),
        name="peer_route",
    )(qp, sk, jnp.asarray(_PAIR_FLAT))


PEER_TT = 64
N_OFFS = 8
PEER_UNROLL = 8
NPICK = PEER_HEADS * PEER_TOPK
CHUNK_ROWS = PEER_TOPK * PAIR_ROWS
STAGE_COLS = NPICK * PAIR_ROWS


def _pair_table(w):
    e = w.shape[0]
    w4 = w.astype(_bf16).reshape(2, e // 2, PAIR_WORD_ROWS, LANES)
    words = lax.bitcast_convert_type(jnp.transpose(w4, (1, 2, 3, 0)), jnp.int32)
    return words.reshape(e // 2 * PAIR_WORD_ROWS, LANES)


def _gather_head(tab_ref, tok_rows, c, offs):
    tiles = []
    n_off = len(offs)
    for k in range(PEER_TOPK):
        if k % n_off == 0:
            part_rows = tok_rows.at[pl.ds(c * PEER_TOPK + k, n_off)]
        r = pl.multiple_of(part_rows[offs[k % n_off]], PAIR_WORD_ROWS)
        tiles.append(pltpu.bitcast(tab_ref[pl.ds(r, PAIR_WORD_ROWS), :], _bf16))
    return jnp.concatenate(tiles, axis=0)


def _peer_up_kernel(row_ref, off_ref, x_ref, tab_ref, diag_ref, fold_ref, gsum_ref, h_ref, z_ref):
    offs = [off_ref[k] for k in range(N_OFFS)]

    def token(t, _):
        x8 = x_ref[t]
        tok_rows = row_ref.at[pl.ds(t * NPICK, NPICK)]
        for c in range(PEER_HEADS):
            rows = _gather_head(tab_ref, tok_rows, c, offs)
            z_ref[t, :, c * CHUNK_ROWS:(c + 1) * CHUNK_ROWS] = _dot_nt(x8, rows)
        return 0

    lax.fori_loop(0, PEER_TT, token, 0, unroll=PEER_UNROLL)
    zm = (z_ref[...] * diag_ref[...][None]).reshape(PEER_TT * 8, STAGE_COLS)
    part = _dot(zm.astype(_bf16), fold_ref[...])
    hi, lo = _split_hi_lo(part)
    h_ref[...] = _dot(gsum_ref[...], hi) + _dot(gsum_ref[...], lo)


def _peer_up(rows_flat, x8, tab, diag, fold, gsum, n_tok, x_off):
    t = n_tok
    tt = PEER_TT
    xo = x_off // tt
    full = lambda a: pl.BlockSpec(a.shape, lambda i: (0,) * a.ndim)
    return pl.pallas_call(
        _peer_up_kernel,
        grid=(t // tt,),
        in_specs=[pl.BlockSpec((tt * NPICK,), lambda i: (i,), memory_space=pltpu.SMEM),
                  pl.BlockSpec((N_OFFS,), lambda i: (0,), memory_space=pltpu.SMEM),
                  pl.BlockSpec((tt, 8, LANES), lambda i: (i + xo, 0, 0)),
                  pl.BlockSpec(tab.shape, lambda i: (0, 0), pipeline_mode=pl.Buffered(1)),
                  full(diag), full(fold), full(gsum)],
        out_specs=pl.BlockSpec((tt, SLOTS), lambda i: (i, 0)),
        out_shape=jax.ShapeDtypeStruct((t, SLOTS), _f32),
        scratch_shapes=[pltpu.VMEM((tt, 8, STAGE_COLS), _f32)],
        compiler_params=_cparams(("arbitrary",)),
        name="peer_up",
    )(rows_flat, jnp.arange(N_OFFS, dtype=jnp.int32), x8, tab, diag, fold, gsum)


FUSE_TT = 128
FUSE_INNER = FUSE_TT // PEER_HEADS


def _peer_up_route_kernel(row_ref, off_ref, x_ref, tab_ref, diag_ref, fold_ref, gsum_ref, q_ref, sk_ref, flat_ref,
                          h_ref, row2_ref, gate2_ref, z_ref, rowt_ref, gatet_ref):
    offs = [off_ref[k] for k in range(N_OFFS)]
    rows_k, flat_c, keep_c = _route_consts(flat_ref, FUSE_TT)

    def trip(j, _):
        steps = _route_head_steps(j, q_ref, sk_ref, flat_c, keep_c, rows_k, rowt_ref, gatet_ref)
        per = -(-len(steps) // FUSE_INNER)
        for i in range(FUSE_INNER):
            t = j * FUSE_INNER + i
            x8 = x_ref[t]
            tok_rows = row_ref.at[pl.ds(t * NPICK, NPICK)]
            for c in range(PEER_HEADS):
                rows = _gather_head(tab_ref, tok_rows, c, offs)
                z_ref[t, :, c * CHUNK_ROWS:(c + 1) * CHUNK_ROWS] = _dot_nt(x8, rows)
            for step in steps[i * per:(i + 1) * per]:
                step()
        return 0

    lax.fori_loop(0, PEER_HEADS, trip, 0)
    row2_ref[...] = rowt_ref[...].T
    gate2_ref[...] = gatet_ref[...].T
    zm = (z_ref[...] * diag_ref[...][None]).reshape(FUSE_TT * 8, STAGE_COLS)
    part = _dot(zm.astype(_bf16), fold_ref[...])
    hi, lo = _split_hi_lo(part)
    h_ref[...] = _dot(gsum_ref[...], hi) + _dot(gsum_ref[...], lo)


def _peer_up_route(rows_flat, x8, tab, diag, fold, gsum, qp, sk, n_tok, x_off, q_off):
    tt = FUSE_TT
    xo, qo = x_off // tt, q_off // tt
    full = lambda a: pl.BlockSpec(a.shape, lambda i: (0,) * a.ndim)
    flat = jnp.asarray(_PAIR_FLAT)
    return pl.pallas_call(
        _peer_up_route_kernel,
        grid=(n_tok // tt,),
        in_specs=[pl.BlockSpec((tt * NPICK,), lambda i: (i,), memory_space=pltpu.SMEM),
                  pl.BlockSpec((N_OFFS,), lambda i: (0,), memory_space=pltpu.SMEM),
                  pl.BlockSpec((tt, 8, LANES), lambda i: (i + xo, 0, 0)),
                  pl.BlockSpec(tab.shape, lambda i: (0, 0), pipeline_mode=pl.Buffered(1)),
                  full(diag), full(fold), full(gsum),
                  pl.BlockSpec((tt, qp.shape[1]), lambda i: (i + qo, 0)), full(sk), full(flat)],
        out_specs=[pl.BlockSpec((tt, SLOTS), lambda i: (i, 0)),
                   pl.BlockSpec((tt, NPICK), lambda i: (i, 0)),
                   pl.BlockSpec((tt, SLOTS), lambda i: (i, 0))],
        out_shape=[jax.ShapeDtypeStruct((n_tok, SLOTS), _f32),
                   jax.ShapeDtypeStruct((n_tok, NPICK), jnp.int32),
                   jax.ShapeDtypeStruct((n_tok, SLOTS), _f32)],
        scratch_shapes=[pltpu.VMEM((tt, 8, STAGE_COLS), _f32),
                        pltpu.VMEM((NPICK, tt), jnp.int32), pltpu.VMEM((SLOTS, tt), _f32)],
        compiler_params=_cparams(("arbitrary",)),
        name="peer_up_route",
    )(rows_flat, jnp.arange(N_OFFS, dtype=jnp.int32), x8, tab, diag, fold, gsum, qp, sk, flat)


def _peer_down_kernel(row_ref, off_ref, h_ref, gate_ref, tab_ref, ex_ref, diag_ref, y_ref, wexp_ref):
    offs = [off_ref[k] for k in range(N_OFFS)]
    w = _gelu_tanh(h_ref[...]) * gate_ref[...]
    hi, lo = _split_hi_lo(w)
    wexp_ref[...] = _dot(hi, ex_ref[...]) + _dot(lo, ex_ref[...])

    def token(t, _):
        a = jnp.broadcast_to(wexp_ref[pl.ds(t, 1), :], diag_ref.shape) * diag_ref[...]
        ahi = a.astype(_bf16).astype(_f32)
        a2 = jnp.concatenate([ahi, a - ahi], axis=0).astype(_bf16)
        acc = jnp.zeros((2 * 8, LANES), _f32)
        tok_rows = row_ref.at[pl.ds(t * NPICK, NPICK)]
        for c in range(PEER_HEADS):
            rows = _gather_head(tab_ref, tok_rows, c, offs)
            acc = acc + _dot(a2[:, c * CHUNK_ROWS:(c + 1) * CHUNK_ROWS], rows)
        y_ref[t] = acc[:8] + acc[8:]
        return 0

    lax.fori_loop(0, PEER_TT, token, 0, unroll=PEER_UNROLL)


def _peer_down(rows_flat, h2, gate2, tab, expand, diag):
    t = h2.shape[0]
    tt = PEER_TT
    full = lambda a: pl.BlockSpec(a.shape, lambda i: (0,) * a.ndim)
    return pl.pallas_call(
        _peer_down_kernel,
        grid=(t // tt,),
        in_specs=[pl.BlockSpec((tt * NPICK,), lambda i: (i,), memory_space=pltpu.SMEM),
                  pl.BlockSpec((N_OFFS,), lambda i: (0,), memory_space=pltpu.SMEM),
                  pl.BlockSpec((tt, SLOTS), lambda i: (i, 0)),
                  pl.BlockSpec((tt, SLOTS), lambda i: (i, 0)),
                  pl.BlockSpec(tab.shape, lambda i: (0, 0), pipeline_mode=pl.Buffered(1)),
                  full(expand), full(diag)],
        out_specs=pl.BlockSpec((tt, 8, LANES), lambda i: (i, 0, 0)),
        out_shape=jax.ShapeDtypeStruct((t, 8, LANES), _f32),
        scratch_shapes=[pltpu.VMEM((tt, STAGE_COLS), _f32)],
        compiler_params=_cparams(("arbitrary",)),
        name="peer_down",
    )(rows_flat, jnp.arange(N_OFFS, dtype=jnp.int32), h2, gate2, tab, expand, diag)


def _peer_constants():
    col = np.arange(STAGE_COLS)
    p, r = col // PAIR_ROWS, col % PAIR_ROWS
    slot = (p // PEER_TOPK) * 2 * PEER_TOPK + (r % 2) * PEER_TOPK + p % PEER_TOPK
    diag = (r[None, :] // 2 == np.arange(8)[:, None]).astype(np.float32)
    fold = (slot[:, None] == np.arange(SLOTS)[None, :]).astype(np.float32)
    gsum = lambda tt: jnp.asarray(np.arange(tt * 8)[None, :] // 8 == np.arange(tt)[:, None], _bf16)
    return (jnp.asarray(diag, _f32), jnp.asarray(fold, _bf16), jnp.asarray(fold.T, _bf16),
            gsum(PEER_TT), gsum(FUSE_TT))


FIN_TM = 512


def _final_norm_kernel(x_ref, y_ref, g_ref, o_ref):
    z = x_ref[...] + y_ref[...]
    o_ref[...] = z * lax.rsqrt(jnp.mean(z * z, axis=-1, keepdims=True) + EPS) * g_ref[...]


def _final_norm(x1, y, g):
    t = x1.shape[0]
    row = pl.BlockSpec((FIN_TM, D_MODEL), lambda i: (i, 0))
    return pl.pallas_call(
        _final_norm_kernel,
        grid=(t // FIN_TM,),
        in_specs=[row, row, pl.BlockSpec((1, D_MODEL), lambda i: (0, 0))],
        out_specs=row,
        out_shape=jax.ShapeDtypeStruct((t, D_MODEL), _f32),
        compiler_params=_cparams(("parallel",)),
        name="final_norm",
    )(x1, y, g.reshape(1, D_MODEL))


def _cmp_to_sel_counts(n_rows, n_sel):
    r = SEL_LEN // CMP_STRIDE
    c = CMP_LEN // CMP_STRIDE
    off = np.arange(n_rows)[:, None] - r * np.arange(n_sel)[None, :] + (c - 1)
    counts = np.zeros((n_rows, LANES), np.float32)
    for n in range(c):
        counts[:, :n_sel] += ((off - n >= 0) & (off - n < r)).astype(np.float32)
    counts[n_rows - 1] = 0.0
    return counts


def _attention(x2d, b, s, w_in, norm_mix_g, rel_bias, lam_rows, subln_g,
               cmp_pos_k, cmp_pos_v, cmp_w1_k, cmp_w2_k, cmp_w1_v, cmp_w2_v):
    dq, dv, nq, kcvc, vs2, vw2, ng, dkT, ksT, kwT = _in_proj(x2d, norm_mix_g, *_arrange_w_in(w_in))
    to3 = lambda a: a.reshape(b, s, a.shape[-1])

    bt_diff = _near_tiles(_bias_tiles(rel_bias[:, :DIFF_HEADS], DIFF_TQ, DIFF_TQ, 2))
    o_diff = _diff_attn(to3(dq), dkT, to3(dv), bt_diff, lam_rows, subln_g)

    n = s // CMP_STRIDE
    a = kcvc.reshape(b, n, CMP_STRIDE, 2, NSA_KV_HEADS, NSA_HD)
    a = jnp.transpose(a, (0, 3, 4, 1, 2, 5)).reshape(b, 2, NSA_KV_HEADS, n, CMP_STRIDE * NSA_HD)
    pos = jnp.stack([cmp_pos_k, cmp_pos_v]).reshape(2, 2, 1, CMP_STRIDE * NSA_HD)
    w1 = jnp.stack([cmp_w1_k, cmp_w1_v]).astype(_bf16)
    w2 = jnp.stack([cmp_w2_k, cmp_w2_v])
    w2dup = jnp.concatenate([w2, w2], axis=-1).astype(_bf16)
    cmp_kv = _compress(a, pos, w1, w2dup)

    n_sel = s // SEL_LEN
    counts = jnp.asarray(_cmp_to_sel_counts(n, n_sel), _bf16)
    nq3 = to3(nq)
    o_cmp, sel = _cmp_attn(nq3, cmp_kv[:, 0], cmp_kv[:, 1], counts, n_sel)

    nsa_bias = rel_bias[:, DIFF_HEADS:]
    tq = NSA_TQ
    def group_tiles(n_tiles, window=None):
        bt = _bias_tiles(nsa_bias, tq, tq, n_tiles, window=window)
        bt = bt.reshape(NSA_KV_HEADS, NSA_GROUP, n_tiles, tq, tq)[:, np.asarray(GROUP_ORDER)]
        return jnp.transpose(bt, (0, 2, 1, 3, 4)).reshape(NSA_KV_HEADS, n_tiles, NSA_GROUP * tq, tq)

    expand = jnp.asarray(np.arange(s)[None, :] // SEL_LEN == np.arange(LANES)[:, None], _bf16)
    o_sel = _sel_attn(nq3, ksT, to3(vs2), sel, expand, _near_tiles(group_tiles(2)))

    bt_win = jnp.transpose(group_tiles(WIN_TILES, window=WINDOW), (0, 2, 1, 3))
    bt_win = bt_win.reshape(NSA_KV_HEADS, NSA_GROUP * tq, WIN_TILES * tq)
    o_win = _win_attn(nq3, kwT, to3(vw2), bt_win)

    flat = lambda a3: a3.reshape(b * s, a3.shape[-1])
    return flat(o_diff), flat(o_cmp), flat(o_sel), flat(o_win), ng


def _gate_expand():
    ge = np.zeros((3, LANES, NSA_WIDTH), np.float32)
    for head in range(NSA_HEADS):
        for br in range(3):
            ge[br, 3 * head + br, head * NSA_HD:(head + 1) * NSA_HD] = 1.0
    return jnp.asarray(ge, _bf16)


PEER_PARTS = 8


def _peer(x1, xn, qp, peer_sub_keys, peer_u, peer_v):
    t = x1.shape[0]
    sk = peer_sub_keys.reshape(2 * PEER_HEADS, PEER_NKEYS, PEER_DKEY // 2).astype(_bf16)
    diag, fold, expand, gsum, gsum_fuse = _peer_constants()
    x8 = xn.reshape(t, 8, LANES)
    tab_u = _pair_table(peer_u)
    n = t // PEER_PARTS
    rows, gate = _peer_route(qp, sk, n)
    rows_all, gate_all, h_all = [rows], [gate], []
    for k in range(PEER_PARTS - 1):
        h, rows, gate = _peer_up_route(rows.reshape(n * NPICK), x8, tab_u, diag, fold, gsum_fuse, qp, sk,
                                       n, k * n, (k + 1) * n)
        h_all.append(h)
        rows_all.append(rows)
        gate_all.append(gate)
    h_all.append(_peer_up(rows.reshape(n * NPICK), x8, tab_u, diag, fold, gsum, n, (PEER_PARTS - 1) * n))
    rows_flat = jnp.concatenate(rows_all, axis=0).reshape(t * NPICK)
    y = _peer_down(rows_flat, jnp.concatenate(h_all, axis=0), jnp.concatenate(gate_all, axis=0),
                   _pair_table(peer_v), expand, diag)
    return y.reshape(t, D_MODEL)


def kernel(x, w_in, w_out, norm_mix_g, norm_ffn_g, final_norm_g, rel_bias, diff_lq1, diff_lk1, diff_lq2, diff_lk2, diff_subln_g, cmp_pos_k, cmp_pos_v, cmp_w1_k, cmp_w2_k, cmp_w1_v, cmp_w2_v, peer_w_q, peer_sub_keys, peer_u, peer_v):
    b, s, d = x.shape
    x2d = x.reshape(b * s, d)
    lam_rows = jnp.zeros((8, LANES), _f32).at[0:4, :DIFF_HD].set(
        jnp.stack([diff_lq1[0], diff_lk1[0], diff_lq2[0], diff_lk2[0]]))
    o_diff, o_cmp, o_sel, o_win, ng = _attention(
        x2d, b, s, w_in[0], norm_mix_g[0], rel_bias, lam_rows, diff_subln_g[0],
        cmp_pos_k[0], cmp_pos_v[0], cmp_w1_k[0], cmp_w2_k[0], cmp_w1_v[0], cmp_w2_v[0])
    x1, xn, qp = _mix_out(x2d, o_diff, o_cmp, o_sel, o_win, ng, _gate_expand(),
                          w_out[0].astype(_bf16), norm_ffn_g[0], peer_w_q[0].astype(_bf16))
    y = _peer(x1, xn, qp, peer_sub_keys[0], peer_u[0], peer_v[0])
    return _final_norm(x1, y, final_norm_g).reshape(b, s, d)
```

```python
import functools
import math

import numpy as np
import jax
import jax.numpy as jnp
from jax import lax
from jax.experimental import pallas as pl
from jax.experimental.pallas import tpu as pltpu

D_MODEL = 1024
DIFF_HEADS = 4
DIFF_HD = 64
DIFF_VD = 2 * DIFF_HD
DIFF_WIDTH = DIFF_HEADS * DIFF_VD
NSA_HEADS = 8
NSA_KV_HEADS = 2
NSA_GROUP = NSA_HEADS // NSA_KV_HEADS
NSA_HD = 64
NSA_WIDTH = NSA_HEADS * NSA_HD
CMP_LEN = 32
CMP_STRIDE = 16
CMP_HIDDEN = 256
SEL_LEN = 64
SEL_TOPN = 8
WINDOW = 512
FORCE_BONUS = 1000.0
NEG_INF = -1e30
N_BUCKETS = 32
MAX_DISTANCE = 128
PEER_HEADS = 8
PEER_NKEYS = 128
PEER_EXPERTS = PEER_NKEYS ** 2
PEER_DKEY = 256
PEER_TOPK = 16
EPS = 1e-6
LAMBDA_INIT = 0.8 - 0.6 * math.exp(-0.3 * 0)
LOG2E = math.log2(math.e)

KV_COLS = NSA_KV_HEADS * NSA_HD
SPLIT_SIZES = (DIFF_HEADS * 2 * DIFF_HD, DIFF_HEADS * 2 * DIFF_HD, DIFF_WIDTH, NSA_WIDTH,
               KV_COLS, KV_COLS, KV_COLS, KV_COLS, KV_COLS, KV_COLS, 3 * NSA_HEADS)
SPLIT_OFF = tuple(int(v) for v in np.concatenate([[0], np.cumsum(SPLIT_SIZES)]))

LANES = 128
VMEM_LIMIT = 56 * 1024 * 1024
HALF_EXPERTS = PEER_EXPERTS // 2
PAIR_ROWS = 16
PAIR_WORD_ROWS = PAIR_ROWS // 2

_f32 = jnp.float32
_bf16 = jnp.bfloat16


def _cparams(sem):
    return pltpu.CompilerParams(dimension_semantics=sem, vmem_limit_bytes=VMEM_LIMIT)


def _dot(a, b):
    return jnp.dot(a, b, preferred_element_type=_f32)


def _dot_nt(a, b):
    return lax.dot_general(a, b, (((1,), (1,)), ((), ())), preferred_element_type=_f32)


def _split_hi_lo(x):
    hi = x.astype(_bf16)
    lo = (x - hi.astype(_f32)).astype(_bf16)
    return hi, lo


def _gelu_tanh(x):
    c = math.sqrt(2.0 / math.pi)
    return 0.5 * x * (1.0 + jnp.tanh(c * (x + 0.044715 * (x * x * x))))


IN_TM = 512
_IN_GROUPS = (("dq", 512, _bf16), ("dv", 512, _bf16), ("nq", 512, _bf16), ("kcvc", 256, _f32),
              ("vs2", 256, _bf16), ("vw2", 256, _bf16), ("ng", 128, _f32))
_IN_KEYS_T = (("dkT", 512), ("ksT", 256), ("kwT", 256))


def _in_proj_kernel(x_ref, g_ref, w_ref, wt_ref, *out_refs):
    x = x_ref[...]
    y = x * lax.rsqrt(jnp.mean(x * x, axis=-1, keepdims=True) + EPS)
    h = (y * g_ref[...]).astype(_bf16)
    off = 0
    for (_, width, dt), o_ref in zip(_IN_GROUPS, out_refs):
        o_ref[...] = _dot(h, w_ref[:, off:off + width]).astype(dt)
        off += width
    off = 0
    for (_, width), o_ref in zip(_IN_KEYS_T, out_refs[len(_IN_GROUPS):]):
        o_ref[...] = _dot_nt(wt_ref[off:off + width, :], h).astype(o_ref.dtype)
        off += width


def _arrange_w_in(w_in):
    o = SPLIT_OFF
    sl = lambda i: w_in[:, o[i]:o[i + 1]]
    dup = lambda w: jnp.concatenate([w[:, :64], w[:, :64], w[:, 64:], w[:, 64:]], axis=1)
    ng = jnp.pad(sl(10), ((0, 0), (0, LANES - 3 * NSA_HEADS)))
    cols = [sl(0) * (DIFF_HD ** -0.5 * LOG2E), sl(2), sl(3) * (NSA_HD ** -0.5 * LOG2E),
            sl(4), sl(5), dup(sl(7)), dup(sl(9)), ng]
    keys = [sl(1), dup(sl(6)), dup(sl(8))]
    return (jnp.concatenate(cols, axis=1).astype(_bf16),
            jnp.concatenate(keys, axis=1).T.astype(_bf16))


def _in_proj(x2d, g, w_arr, wt_arr):
    t = x2d.shape[0]
    out_shape = ([jax.ShapeDtypeStruct((t, wd), dt) for _, wd, dt in _IN_GROUPS]
                 + [jax.ShapeDtypeStruct((wd, t), _bf16) for _, wd in _IN_KEYS_T])
    out_specs = ([pl.BlockSpec((IN_TM, wd), lambda i: (i, 0)) for _, wd, _ in _IN_GROUPS]
                 + [pl.BlockSpec((wd, IN_TM), lambda i: (0, i)) for _, wd in _IN_KEYS_T])
    return pl.pallas_call(
        _in_proj_kernel,
        grid=(t // IN_TM,),
        in_specs=[pl.BlockSpec((IN_TM, D_MODEL), lambda i: (i, 0)),
                  pl.BlockSpec((1, D_MODEL), lambda i: (0, 0)),
                  pl.BlockSpec(w_arr.shape, lambda i: (0, 0)),
                  pl.BlockSpec(wt_arr.shape, lambda i: (0, 0))],
        out_specs=out_specs,
        out_shape=out_shape,
        compiler_params=_cparams(("parallel",)),
        name="in_proj",
    )(x2d, g.reshape(1, D_MODEL), w_arr, wt_arr)


def _bucket_table(n):
    rel = np.arange(n)
    max_exact = N_BUCKETS // 2
    nf = np.maximum(rel, 1).astype(np.float32)
    large = max_exact + (np.log(nf / np.float32(max_exact)) / np.float32(math.log(MAX_DISTANCE / max_exact))
                         * np.float32(N_BUCKETS - max_exact)).astype(np.int32)
    large = np.minimum(large, N_BUCKETS - 1)
    return np.where(rel < max_exact, rel, large).astype(np.int32)


def _bias_tiles(rel_bias_heads, tq, tk, n_tiles, window=None):
    tab = (rel_bias_heads.astype(_f32) - rel_bias_heads[N_BUCKETS - 1][None, :].astype(_f32)) * LOG2E
    period = tq + tk
    m = np.arange(period)
    off = np.where(m < tk, -m, period - m)
    rel_w = np.arange(n_tiles)[:, None] * tq + off[None, :]
    bucket = _bucket_table(n_tiles * tq + period)[np.clip(rel_w, 0, None)]
    w = jnp.transpose(tab[bucket], (2, 0, 1))
    tiles = jnp.tile(w, (1, 1, tq))[..., :tq * (period - 1)]
    tiles = tiles.reshape(w.shape[0], n_tiles, tq, period - 1)[..., :tk]
    rel = (np.arange(n_tiles)[:, None, None] * tq + np.arange(tq)[None, :, None]
           - np.arange(tk)[None, None, :])
    ok = rel >= 0
    if window is not None:
        ok &= rel < window
    return jnp.where(jnp.asarray(ok)[None], tiles, NEG_INF)


def _near_tiles(bt):
    first = jnp.concatenate([bt[..., 0, :, :], jnp.full_like(bt[..., 0, :, :], NEG_INF)], axis=-1)
    later = jnp.concatenate([bt[..., 1, :, :], bt[..., 0, :, :]], axis=-1)
    return jnp.stack([first, later], axis=-3)


def _attend_first(s, v_aug):
    m = jnp.max(s, axis=-1, keepdims=True)
    return m, _dot(jnp.exp2(s - m).astype(_bf16), v_aug)


def _attend_step(carry, s, v_aug):
    m, acc = carry
    m_new = jnp.maximum(m, jnp.max(s, axis=-1, keepdims=True))
    acc = jnp.exp2(m - m_new) * acc + _dot(jnp.exp2(s - m_new).astype(_bf16), v_aug)
    return m_new, acc


def _lane_half_mask(shape):
    return lax.broadcasted_iota(jnp.int32, shape, len(shape) - 1) < (LANES // 2)


DIFF_TQ = 256


def _diff_attn_kernel(q_ref, k_ref, v_ref, bt_ref, lam_ref, g_ref, o_ref):
    tq = DIFF_TQ
    qb = pl.program_id(2)
    q = q_ref[...]
    lo = _lane_half_mask(q.shape)
    zero = jnp.zeros_like(q)
    q2 = jnp.concatenate([jnp.where(lo, q, zero), jnp.where(lo, zero, q)], axis=0)

    def kv(start, n):
        start = pl.multiple_of(start, tq)
        v = v_ref[pl.ds(start, n), :]
        return k_ref[:, pl.ds(start, n)], jnp.concatenate([v, jnp.ones_like(v)], axis=1)

    def both(b):
        return jnp.concatenate([b, b], axis=0)

    k, v = kv(jnp.maximum(qb - 1, 0) * tq, 2 * tq)
    carry = _attend_first(_dot(q2, k) + both(bt_ref[jnp.minimum(qb, 1)]), v)

    n_far = jnp.maximum(qb - 1, 0)

    def odd(c):
        k, v = kv((n_far - 1) * tq, tq)
        return _attend_step(c, _dot(q2, k), v)

    carry = lax.cond(n_far % 2 == 1, odd, lambda c: c, carry)

    def pair(i, c):
        k, v = kv(i * 2 * tq, 2 * tq)
        return _attend_step(c, _dot(q2, k), v)

    m, acc = lax.fori_loop(0, n_far // 2, pair, carry)
    o2 = acc[:, :DIFF_VD] / acc[:, DIFF_VD:]
    lv = lam_ref[...]
    lam = (jnp.exp(jnp.sum(lv[0:1] * lv[1:2], axis=-1, keepdims=True))
           - jnp.exp(jnp.sum(lv[2:3] * lv[3:4], axis=-1, keepdims=True)) + LAMBDA_INIT)
    o = o2[:tq] - lam * o2[tq:]
    y = o * lax.rsqrt(jnp.mean(o * o, axis=-1, keepdims=True) + EPS)
    o_ref[...] = (y * g_ref[...] * (1.0 - LAMBDA_INIT)).astype(o_ref.dtype)


def _diff_attn(dq, dkT, dv, bt, lam_rows, subln_g):
    b, s, _ = dq.shape
    tq = DIFF_TQ
    return pl.pallas_call(
        _diff_attn_kernel,
        grid=(b, DIFF_HEADS, s // tq),
        in_specs=[pl.BlockSpec((None, tq, LANES), lambda bi, h, qb: (bi, qb, h)),
                  pl.BlockSpec((LANES, s), lambda bi, h, qb: (h, bi)),
                  pl.BlockSpec((None, s, LANES), lambda bi, h, qb: (bi, 0, h)),
                  pl.BlockSpec((None, 2, tq, 2 * tq), lambda bi, h, qb: (h, 0, 0, 0)),
                  pl.BlockSpec((8, LANES), lambda bi, h, qb: (0, 0)),
                  pl.BlockSpec((1, LANES), lambda bi, h, qb: (0, 0))],
        out_specs=pl.BlockSpec((None, tq, LANES), lambda bi, h, qb: (bi, qb, h)),
        out_shape=jax.ShapeDtypeStruct((b, s, DIFF_WIDTH), _bf16),
        compiler_params=_cparams(("parallel", "parallel", "arbitrary")),
        name="diff_attn",
    )(dq, dkT, dv, bt, lam_rows, subln_g.reshape(1, LANES))


def _compress_kernel(a_ref, pos_ref, w1_ref, w2_ref, o_ref):
    a = a_ref[...]
    half = (CMP_LEN // 2) * NSA_HD
    za = _dot((a + pos_ref[0]).astype(_bf16), w1_ref[:half, :])
    zb = _dot((a + pos_ref[1]).astype(_bf16), w1_ref[half:, :])
    n = a.shape[0]
    hid = za + pltpu.roll(zb, n - 1, 0)
    o_ref[...] = _dot(_gelu_tanh(hid).astype(_bf16), w2_ref[...]).astype(o_ref.dtype)


def _compress(a, pos, w1, w2dup):
    b, _, _, n, width = a.shape
    return pl.pallas_call(
        _compress_kernel,
        grid=(b, 2, NSA_KV_HEADS),
        in_specs=[pl.BlockSpec((None, None, None, n, width), lambda bi, kv, g: (bi, kv, g, 0, 0)),
                  pl.BlockSpec((None, 2, 1, width), lambda bi, kv, g: (kv, 0, 0, 0)),
                  pl.BlockSpec((None, 2 * width, CMP_HIDDEN), lambda bi, kv, g: (kv, 0, 0)),
                  pl.BlockSpec((None, CMP_HIDDEN, LANES), lambda bi, kv, g: (kv, 0, 0))],
        out_specs=pl.BlockSpec((None, None, None, n, LANES), lambda bi, kv, g: (bi, kv, g, 0, 0)),
        out_shape=jax.ShapeDtypeStruct((b, 2, NSA_KV_HEADS, n, LANES), _bf16),
        compiler_params=_cparams(("parallel", "parallel", "parallel")),
        name="compress",
    )(a, pos, w1, w2dup)


NSA_TQ = 256


GROUP_ORDER = (0, 2, 1, 3)


def _stack_group_queries(q):
    rows = []
    for j in GROUP_ORDER:
        blk = q[:, (j // 2) * LANES:(j // 2 + 1) * LANES]
        lo = _lane_half_mask(blk.shape)
        keep = lo if j % 2 == 0 else jnp.logical_not(lo)
        rows.append(jnp.where(keep, blk, jnp.zeros_like(blk)))
    return jnp.concatenate(rows, axis=0)


def _unstack_group_outputs(o4, tq):
    lo = _lane_half_mask((tq, LANES))
    half = NSA_GROUP // 2
    pairs = [jnp.where(lo, o4[i * tq:(i + 1) * tq], o4[(half + i) * tq:(half + i + 1) * tq])
             for i in range(half)]
    return jnp.concatenate(pairs, axis=1)


def _group_values(v):
    lo = _lane_half_mask(v.shape)
    one = jnp.ones_like(v)
    return jnp.where(lo, v, one), jnp.where(lo, one, v)


def _group_pv(p, v_pair):
    n = p.shape[0] // 2
    p = p.astype(_bf16)
    return jnp.concatenate([_dot(p[:n], v_pair[0]), _dot(p[n:], v_pair[1])], axis=0)


def _group_attend_first(s, v_pair):
    m = jnp.max(s, axis=-1, keepdims=True)
    return m, _group_pv(jnp.exp2(s - m), v_pair)


def _group_attend_step(carry, s, v_pair):
    m, acc = carry
    m_new = jnp.maximum(m, jnp.max(s, axis=-1, keepdims=True))
    return m_new, jnp.exp2(m - m_new) * acc + _group_pv(jnp.exp2(s - m_new), v_pair)


def _group_normalise(acc):
    return acc / pltpu.roll(acc, LANES // 2, 1)


def _cmp_attn_kernel(q_ref, kc_ref, vc_ref, cs_ref, o_ref, sel_ref, *, n_sel):
    tq = NSA_TQ
    qb = pl.program_id(2)
    q4 = _stack_group_queries(q_ref[...])
    s = _dot_nt(q4, kc_ref[...])
    row = lax.broadcasted_iota(jnp.int32, s.shape, 0)
    pos4 = qb * tq + (row & (tq - 1))
    c = lax.broadcasted_iota(jnp.int32, s.shape, 1)
    ok = (c * CMP_STRIDE + (CMP_LEN - 1)) <= pos4
    s = jnp.where(ok, s, NEG_INF)
    m = jnp.max(s, axis=-1, keepdims=True)
    e = jnp.where(ok, jnp.exp2(s - m), 0.0)
    l = jnp.sum(e, axis=-1, keepdims=True)
    p = e / jnp.where(l > 0.0, l, 1.0)
    o_ref[...] = _unstack_group_outputs(_dot(p.astype(_bf16), vc_ref[...]), tq).astype(o_ref.dtype)

    psum = p[0:tq] + p[tq:2 * tq] + p[2 * tq:3 * tq] + p[3 * tq:4 * tq]
    hi, lo = _split_hi_lo(psum)
    imp = _dot(hi, cs_ref[...]) + _dot(lo, cs_ref[...])
    blk = lax.broadcasted_iota(jnp.int32, imp.shape, 1)
    pos = qb * tq + lax.broadcasted_iota(jnp.int32, imp.shape, 0)
    cur = pos // SEL_LEN
    forced = (blk == 0) | (blk == cur) | (blk == cur - 1)
    score = jnp.where(blk <= cur, imp + jnp.where(forced, FORCE_BONUS, 0.0), NEG_INF)
    st = score.T[:n_sel]
    blk_t = lax.broadcasted_iota(jnp.int32, st.shape, 0)
    rank = jnp.zeros(st.shape, jnp.int32)
    for i in range(n_sel):
        si = st[i:i + 1]
        beats = (si > st) | ((si == st) & (i < blk_t))
        rank = rank + beats.astype(jnp.int32)
    sel_t = ((rank < SEL_TOPN) & (st > 0.5 * NEG_INF)).astype(_f32)
    sel = jnp.concatenate([sel_t, jnp.zeros((LANES - n_sel, tq), _f32)], axis=0).T
    sel_ref[...] = sel.astype(sel_ref.dtype)


def _cmp_attn(nq, kcmp, vcmp, cmp_sel, n_sel):
    b, s, _ = nq.shape
    tq = NSA_TQ
    n = kcmp.shape[-2]
    gw = NSA_GROUP * NSA_HD
    return pl.pallas_call(
        functools.partial(_cmp_attn_kernel, n_sel=n_sel),
        grid=(b, NSA_KV_HEADS, s // tq),
        in_specs=[pl.BlockSpec((None, tq, gw), lambda bi, g, qb: (bi, qb, g)),
                  pl.BlockSpec((None, None, n, LANES), lambda bi, g, qb: (bi, g, 0, 0)),
                  pl.BlockSpec((None, None, n, LANES), lambda bi, g, qb: (bi, g, 0, 0)),
                  pl.BlockSpec((n, LANES), lambda bi, g, qb: (0, 0))],
        out_specs=[pl.BlockSpec((None, tq, gw), lambda bi, g, qb: (bi, qb, g)),
                   pl.BlockSpec((None, None, tq, LANES), lambda bi, g, qb: (bi, g, qb, 0))],
        out_shape=[jax.ShapeDtypeStruct((b, s, NSA_WIDTH), _bf16),
                   jax.ShapeDtypeStruct((b, NSA_KV_HEADS, s, LANES), _bf16)],
        compiler_params=_cparams(("parallel", "parallel", "arbitrary")),
        name="cmp_attn",
    )(nq, kcmp, vcmp, cmp_sel)


SEL_CHUNK = 4 * NSA_TQ


def _sel_attn_kernel(q_ref, k_ref, v_ref, sel_ref, ex_ref, bt_ref, o_ref):
    tq = NSA_TQ
    qb = pl.program_id(2)
    q4 = _stack_group_queries(q_ref[...])
    sel = sel_ref[...]

    def scores(pen_rows, start, n):
        start = pl.multiple_of(start, tq)
        kx = jnp.concatenate([k_ref[:, pl.ds(start, n)], ex_ref[:, pl.ds(start, n)]], axis=0)
        qx = jnp.concatenate([q4, jnp.concatenate([pen_rows] * NSA_GROUP, axis=0)], axis=1)
        return _dot(qx, kx), _group_values(v_ref[pl.ds(start, n), :])

    def penalty(sel_rows):
        return ((sel_rows.astype(_f32) - 1.0) * (-NEG_INF)).astype(_bf16)

    s, v = scores(penalty(sel), jnp.maximum(qb - 1, 0) * tq, 2 * tq)
    carry = _group_attend_first(s + bt_ref[jnp.minimum(qb, 1)], v)

    far_len = jnp.maximum(qb - 1, 0) * tq
    blk = lax.broadcasted_iota(jnp.int32, sel.shape, 1)
    pen_far = penalty(jnp.where(blk * SEL_LEN < far_len, sel, jnp.zeros_like(sel)))

    def far(i, c):
        s, v = scores(pen_far, i * SEL_CHUNK, SEL_CHUNK)
        return _group_attend_step(c, s, v)

    m, acc = lax.fori_loop(0, (far_len + SEL_CHUNK - 1) // SEL_CHUNK, far, carry)
    o_ref[...] = _unstack_group_outputs(_group_normalise(acc), tq).astype(o_ref.dtype)


def _sel_attn(nq, ks2, vs2, sel, expand, bt):
    b, s, _ = nq.shape
    tq = NSA_TQ
    gw = NSA_GROUP * NSA_HD
    return pl.pallas_call(
        _sel_attn_kernel,
        grid=(b, NSA_KV_HEADS, s // tq),
        in_specs=[pl.BlockSpec((None, tq, gw), lambda bi, g, qb: (bi, qb, g)),
                  pl.BlockSpec((LANES, s), lambda bi, g, qb: (g, bi)),
                  pl.BlockSpec((None, s, LANES), lambda bi, g, qb: (bi, 0, g)),
                  pl.BlockSpec((None, None, tq, LANES), lambda bi, g, qb: (bi, g, qb, 0)),
                  pl.BlockSpec((LANES, s), lambda bi, g, qb: (0, 0)),
                  pl.BlockSpec((None, 2, NSA_GROUP * tq, 2 * tq), lambda bi, g, qb: (g, 0, 0, 0))],
        out_specs=pl.BlockSpec((None, tq, gw), lambda bi, g, qb: (bi, qb, g)),
        out_shape=jax.ShapeDtypeStruct((b, s, NSA_WIDTH), _bf16),
        compiler_params=_cparams(("parallel", "parallel", "arbitrary")),
        name="sel_attn",
    )(nq, ks2, vs2, sel, expand, bt)


WIN_TILES = WINDOW // NSA_TQ + 1


def _win_attn_kernel(q_ref, k_ref, v_ref, bt_ref, o_ref):
    tq = NSA_TQ
    qb = pl.program_id(2)
    q4 = _stack_group_queries(q_ref[...])
    s_blocks, va, vb = [], [], []
    for d in range(WIN_TILES):
        start = pl.multiple_of(jnp.maximum(qb - d, 0) * tq, tq)
        s = _dot(q4, k_ref[:, pl.ds(start, tq)])
        s_blocks.append(s if d == 0 else s + jnp.where(qb >= d, 0.0, NEG_INF))
        a, b = _group_values(v_ref[pl.ds(start, tq), :])
        va.append(a)
        vb.append(b)
    s = jnp.concatenate(s_blocks, axis=1) + bt_ref[...]
    m, acc = _group_attend_first(s, (jnp.concatenate(va, axis=0), jnp.concatenate(vb, axis=0)))
    o_ref[...] = _unstack_group_outputs(_group_normalise(acc), tq).astype(o_ref.dtype)


def _win_attn(nq, kw2, vw2, bt):
    b, s, _ = nq.shape
    tq = NSA_TQ
    gw = NSA_GROUP * NSA_HD
    return pl.pallas_call(
        _win_attn_kernel,
        grid=(b, NSA_KV_HEADS, s // tq),
        in_specs=[pl.BlockSpec((None, tq, gw), lambda bi, g, qb: (bi, qb, g)),
                  pl.BlockSpec((LANES, s), lambda bi, g, qb: (g, bi)),
                  pl.BlockSpec((None, s, LANES), lambda bi, g, qb: (bi, 0, g)),
                  pl.BlockSpec((None, NSA_GROUP * tq, WIN_TILES * tq), lambda bi, g, qb: (g, 0, 0))],
        out_specs=pl.BlockSpec((None, tq, gw), lambda bi, g, qb: (bi, qb, g)),
        out_shape=jax.ShapeDtypeStruct((b, s, NSA_WIDTH), _bf16),
        compiler_params=_cparams(("parallel", "parallel", "arbitrary")),
        name="win_attn",
    )(nq, kw2, vw2, bt)


MIX_TM = 512


def _mix_out_kernel(x_ref, od_ref, oc_ref, os_ref, ow_ref, ng_ref, ge_ref, wo_ref, g_ref, wq_ref,
                    x1_ref, xn_ref, qp_ref):
    sig = jax.nn.sigmoid(ng_ref[...])
    hi, lo = _split_hi_lo(sig)
    o_nsa = jnp.zeros(oc_ref.shape, _f32)
    for br, o_ref in enumerate((oc_ref, os_ref, ow_ref)):
        gate = _dot(hi, ge_ref[br]) + _dot(lo, ge_ref[br])
        o_nsa = o_nsa + gate * o_ref[...].astype(_f32)
    y = _dot(od_ref[...], wo_ref[:DIFF_WIDTH, :]) + _dot(o_nsa.astype(_bf16), wo_ref[DIFF_WIDTH:, :])
    x1 = x_ref[...] + y
    x1_ref[...] = x1
    xn = x1 * lax.rsqrt(jnp.mean(x1 * x1, axis=-1, keepdims=True) + EPS) * g_ref[...]
    xn = xn.astype(_bf16)
    xn_ref[...] = xn
    qp_ref[...] = _dot(xn, wq_ref[...]).astype(qp_ref.dtype)


def _mix_out(x2d, o_diff, o_cmp, o_sel, o_win, ng, gate_expand, w_out, g_ffn, w_q):
    t = x2d.shape[0]
    tm = MIX_TM
    nq = w_q.shape[1]
    row = lambda w: pl.BlockSpec((tm, w), lambda i: (i, 0))
    full = lambda a: pl.BlockSpec(a.shape, lambda i: (0,) * a.ndim)
    g2 = g_ffn.reshape(1, D_MODEL)
    return pl.pallas_call(
        _mix_out_kernel,
        grid=(t // tm,),
        in_specs=[row(D_MODEL), row(DIFF_WIDTH), row(NSA_WIDTH), row(NSA_WIDTH), row(NSA_WIDTH),
                  row(LANES), full(gate_expand), full(w_out), full(g2), full(w_q)],
        out_specs=[row(D_MODEL), row(D_MODEL), row(nq)],
        out_shape=[jax.ShapeDtypeStruct((t, D_MODEL), _f32),
                   jax.ShapeDtypeStruct((t, D_MODEL), _bf16),
                   jax.ShapeDtypeStruct((t, nq), _bf16)],
        compiler_params=_cparams(("parallel",)),
        name="mix_out",
    )(x2d, o_diff, o_cmp, o_sel, o_win, ng, gate_expand, w_out, g2, w_q)


ROUTE_TM = 256
SLOTS = 2 * PEER_HEADS * PEER_TOPK


def _extract_step(st):
    sc, rows, n = st[0], st[1], st[2]
    m = jnp.max(sc, axis=0, keepdims=True)
    idx = jnp.min(jnp.where(sc == m, rows, n), axis=0, keepdims=True)
    st[3].append(m)
    st[4].append(idx)
    st[0] = jnp.where(rows == idx, -jnp.inf, sc)


def _route_head_steps(h, q_ref, sk_ref, flat_c, keep_c, rows_k, rowt_ref, gatet_ref):
    half_d = PEER_DKEY // 2
    col = pl.multiple_of(h * PEER_DKEY, PEER_DKEY)
    box = {}

    def start():
        q0 = q_ref[:, pl.ds(col, half_d)]
        q1 = q_ref[:, pl.ds(col + half_d, half_d)]
        box["a"] = [_dot_nt(sk_ref[2 * h], q0), rows_k, PEER_NKEYS, [], []]
        box["b"] = [_dot_nt(sk_ref[2 * h + 1], q1), rows_k, PEER_NKEYS, [], []]

    def first_stage():
        _extract_step(box["a"])
        _extract_step(box["b"])

    def pairs():
        sv0, si0 = jnp.concatenate(box["a"][3], axis=0), jnp.concatenate(box["a"][4], axis=0)
        sv1, si1 = jnp.concatenate(box["b"][3], axis=0), jnp.concatenate(box["b"][4], axis=0)
        slabs, ids = [], []
        for a, b0 in _PAIR_SLABS:
            if a is None:
                slabs.append(sv0[8:16] + sv1[0:1])
                ids.append(si0[8:16] * PEER_NKEYS + si1[0:1])
            else:
                slabs.append(sv0[a:a + 1] + sv1[b0:b0 + 8])
                ids.append(si0[a:a + 1] * PEER_NKEYS + si1[b0:b0 + 8])
        cand = jnp.where(keep_c, jnp.concatenate(slabs, axis=0), -jnp.inf)
        box["cidx"] = jnp.concatenate(ids, axis=0)
        box["c"] = [cand, flat_c, PEER_TOPK * PEER_TOPK, [], []]

    def second_stage():
        _extract_step(box["c"])

    def finish():
        cv, ci = jnp.concatenate(box["c"][3], axis=0), jnp.concatenate(box["c"][4], axis=0)
        e = jnp.concatenate(
            [jnp.sum(jnp.where(flat_c == ci[k:k + 1], box["cidx"], 0), axis=0, keepdims=True)
             for k in range(PEER_TOPK)], axis=0)
        ex = jnp.exp(cv - cv[0:1])
        gate = ex / jnp.sum(ex, axis=0, keepdims=True)
        upper = e >= HALF_EXPERTS
        base = pl.multiple_of(h * PEER_TOPK, PEER_TOPK)
        rowt_ref[pl.ds(base, PEER_TOPK), :] = (e & (HALF_EXPERTS - 1)) * PAIR_WORD_ROWS
        base2 = pl.multiple_of(h * 2 * PEER_TOPK, 2 * PEER_TOPK)
        gatet_ref[pl.ds(base2, PEER_TOPK), :] = jnp.where(upper, 0.0, gate)
        gatet_ref[pl.ds(base2 + PEER_TOPK, PEER_TOPK), :] = jnp.where(upper, gate, 0.0)

    return [start] + [first_stage] * PEER_TOPK + [pairs] + [second_stage] * PEER_TOPK + [finish]


def _route_head(h, q_ref, sk_ref, flat_c, keep_c, rows_k, rowt_ref, gatet_ref):
    for step in _route_head_steps(h, q_ref, sk_ref, flat_c, keep_c, rows_k, rowt_ref, gatet_ref):
        step()


def _route_consts(flat_ref, tm):
    rows_k = lax.broadcasted_iota(jnp.int32, (PEER_NKEYS, tm), 0)
    flat_c = jnp.broadcast_to(flat_ref[...], (flat_ref.shape[0], tm))
    return rows_k, flat_c, flat_c >= 0


def _peer_route_kernel(q_ref, sk_ref, flat_ref, row_ref, gate_ref, rowt_ref, gatet_ref):
    rows_k, flat_c, keep_c = _route_consts(flat_ref, ROUTE_TM)

    def head(h, _):
        _route_head(h, q_ref, sk_ref, flat_c, keep_c, rows_k, rowt_ref, gatet_ref)
        return 0

    lax.fori_loop(0, PEER_HEADS, head, 0)
    row_ref[...] = rowt_ref[...].T
    gate_ref[...] = gatet_ref[...].T


def _pair_slabs():
    slabs, flat = [(0, 0), (0, 8)] + [(a, 0) for a in range(1, 8)] + [(None, 0)], []
    for a, b0 in slabs:
        for i in range(8):
            aa, bb = (8 + i, 0) if a is None else (a, b0 + i)
            flat.append(aa * PEER_TOPK + bb if (aa + 1) * (bb + 1) <= PEER_TOPK else -1)
    return tuple(slabs), np.asarray(flat, np.int32).reshape(-1, 1)


_PAIR_SLABS, _PAIR_FLAT = _pair_slabs()


def _peer_route(qp, sk, n_tok):
    t = n_tok
    tm = ROUTE_TM
    npick = PEER_HEADS * PEER_TOPK
    return pl.pallas_call(
        _peer_route_kernel,
        grid=(t // tm,),
        in_specs=[pl.BlockSpec((tm, qp.shape[1]), lambda i: (i, 0)),
                  pl.BlockSpec(sk.shape, lambda i: (0, 0, 0)),
                  pl.BlockSpec(_PAIR_FLAT.shape, lambda i: (0, 0))],
        out_specs=[pl.BlockSpec((tm, npick), lambda i: (i, 0)),
                   pl.BlockSpec((tm, SLOTS), lambda i: (i, 0))],
        out_shape=[jax.ShapeDtypeStruct((t, npick), jnp.int32),
                   jax.ShapeDtypeStruct((t, SLOTS), _f32)],
        scratch_shapes=[pltpu.VMEM((npick, tm), jnp.int32), pltpu.VMEM((SLOTS, tm), _f32)],
        compiler_params=_cparams(("parallel",)),
        name="peer_route",
    )(qp, sk, jnp.asarray(_PAIR_FLAT))


PEER_TT = 64
N_OFFS = 8
PEER_UNROLL = 8
PEER_DOWN_UNROLL = 16
NPICK = PEER_HEADS * PEER_TOPK
CHUNK_ROWS = PEER_TOPK * PAIR_ROWS
STAGE_COLS = NPICK * PAIR_ROWS


def _pair_table(w):
    e = w.shape[0]
    w4 = w.astype(_bf16).reshape(2, e // 2, PAIR_WORD_ROWS, LANES)
    words = lax.bitcast_convert_type(jnp.transpose(w4, (1, 2, 3, 0)), jnp.int32)
    return words.reshape(e // 2 * PAIR_WORD_ROWS, LANES)


def _gather_head(tab_ref, tok_rows, c, offs):
    tiles = []
    n_off = len(offs)
    for k in range(PEER_TOPK):
        if k % n_off == 0:
            part_rows = tok_rows.at[pl.ds(c * PEER_TOPK + k, n_off)]
        r = pl.multiple_of(part_rows[offs[k % n_off]], PAIR_WORD_ROWS)
        tiles.append(pltpu.bitcast(tab_ref[pl.ds(r, PAIR_WORD_ROWS), :], _bf16))
    return jnp.concatenate(tiles, axis=0)


def _peer_up_kernel(row_ref, off_ref, x_ref, tab_ref, diag_ref, fold_ref, gsum_ref, h_ref, z_ref):
    offs = [off_ref[k] for k in range(N_OFFS)]

    def token(t, _):
        x8 = x_ref[t]
        tok_rows = row_ref.at[pl.ds(t * NPICK, NPICK)]
        for c in range(PEER_HEADS):
            rows = _gather_head(tab_ref, tok_rows, c, offs)
            z_ref[t, :, c * CHUNK_ROWS:(c + 1) * CHUNK_ROWS] = _dot_nt(x8, rows)
        return 0

    lax.fori_loop(0, PEER_TT, token, 0, unroll=PEER_UNROLL)
    zm = (z_ref[...] * diag_ref[...][None]).reshape(PEER_TT * 8, STAGE_COLS)
    part = _dot(zm.astype(_bf16), fold_ref[...])
    hi, lo = _split_hi_lo(part)
    h_ref[...] = _dot(gsum_ref[...], hi) + _dot(gsum_ref[...], lo)


def _peer_up(rows_flat, x8, tab, diag, fold, gsum, n_tok, x_off):
    t = n_tok
    tt = PEER_TT
    xo = x_off // tt
    full = lambda a: pl.BlockSpec(a.shape, lambda i: (0,) * a.ndim)
    return pl.pallas_call(
        _peer_up_kernel,
        grid=(t // tt,),
        in_specs=[pl.BlockSpec((tt * NPICK,), lambda i: (i,), memory_space=pltpu.SMEM),
                  pl.BlockSpec((N_OFFS,), lambda i: (0,), memory_space=pltpu.SMEM),
                  pl.BlockSpec((tt, 8, LANES), lambda i: (i + xo, 0, 0)),
                  pl.BlockSpec(tab.shape, lambda i: (0, 0), pipeline_mode=pl.Buffered(1)),
                  full(diag), full(fold), full(gsum)],
        out_specs=pl.BlockSpec((tt, SLOTS), lambda i: (i, 0)),
        out_shape=jax.ShapeDtypeStruct((t, SLOTS), _f32),
        scratch_shapes=[pltpu.VMEM((tt, 8, STAGE_COLS), _f32)],
        compiler_params=_cparams(("arbitrary",)),
        name="peer_up",
    )(rows_flat, jnp.arange(N_OFFS, dtype=jnp.int32), x8, tab, diag, fold, gsum)


FUSE_TT = 128
FUSE_INNER = FUSE_TT // PEER_HEADS


def _peer_up_route_kernel(row_ref, off_ref, x_ref, tab_ref, diag_ref, fold_ref, gsum_ref, q_ref, sk_ref, flat_ref,
                          h_ref, row2_ref, gate2_ref, z_ref, rowt_ref, gatet_ref):
    offs = [off_ref[k] for k in range(N_OFFS)]
    rows_k, flat_c, keep_c = _route_consts(flat_ref, FUSE_TT)

    def trip(j, _):
        steps = _route_head_steps(j, q_ref, sk_ref, flat_c, keep_c, rows_k, rowt_ref, gatet_ref)
        per = -(-len(steps) // FUSE_INNER)
        for i in range(FUSE_INNER):
            t = j * FUSE_INNER + i
            x8 = x_ref[t]
            tok_rows = row_ref.at[pl.ds(t * NPICK, NPICK)]
            for c in range(PEER_HEADS):
                rows = _gather_head(tab_ref, tok_rows, c, offs)
                z_ref[t, :, c * CHUNK_ROWS:(c + 1) * CHUNK_ROWS] = _dot_nt(x8, rows)
            for step in steps[i * per:(i + 1) * per]:
                step()
        return 0

    lax.fori_loop(0, PEER_HEADS, trip, 0)
    row2_ref[...] = rowt_ref[...].T
    gate2_ref[...] = gatet_ref[...].T
    zm = (z_ref[...] * diag_ref[...][None]).reshape(FUSE_TT * 8, STAGE_COLS)
    part = _dot(zm.astype(_bf16), fold_ref[...])
    hi, lo = _split_hi_lo(part)
    h_ref[...] = _dot(gsum_ref[...], hi) + _dot(gsum_ref[...], lo)


def _peer_up_route(rows_flat, x8, tab, diag, fold, gsum, qp, sk, n_tok, x_off, q_off):
    tt = FUSE_TT
    xo, qo = x_off // tt, q_off // tt
    full = lambda a: pl.BlockSpec(a.shape, lambda i: (0,) * a.ndim)
    flat = jnp.asarray(_PAIR_FLAT)
    return pl.pallas_call(
        _peer_up_route_kernel,
        grid=(n_tok // tt,),
        in_specs=[pl.BlockSpec((tt * NPICK,), lambda i: (i,), memory_space=pltpu.SMEM),
                  pl.BlockSpec((N_OFFS,), lambda i: (0,), memory_space=pltpu.SMEM),
                  pl.BlockSpec((tt, 8, LANES), lambda i: (i + xo, 0, 0)),
                  pl.BlockSpec(tab.shape, lambda i: (0, 0), pipeline_mode=pl.Buffered(1)),
                  full(diag), full(fold), full(gsum),
                  pl.BlockSpec((tt, qp.shape[1]), lambda i: (i + qo, 0)), full(sk), full(flat)],
        out_specs=[pl.BlockSpec((tt, SLOTS), lambda i: (i, 0)),
                   pl.BlockSpec((tt, NPICK), lambda i: (i, 0)),
                   pl.BlockSpec((tt, SLOTS), lambda i: (i, 0))],
        out_shape=[jax.ShapeDtypeStruct((n_tok, SLOTS), _f32),
                   jax.ShapeDtypeStruct((n_tok, NPICK), jnp.int32),
                   jax.ShapeDtypeStruct((n_tok, SLOTS), _f32)],
        scratch_shapes=[pltpu.VMEM((tt, 8, STAGE_COLS), _f32),
                        pltpu.VMEM((NPICK, tt), jnp.int32), pltpu.VMEM((SLOTS, tt), _f32)],
        compiler_params=_cparams(("arbitrary",)),
        name="peer_up_route",
    )(rows_flat, jnp.arange(N_OFFS, dtype=jnp.int32), x8, tab, diag, fold, gsum, qp, sk, flat)


def _peer_down_kernel(row_ref, off_ref, h_ref, gate_ref, tab_ref, ex_ref, diag_ref, y_ref, wexp_ref):
    offs = [off_ref[k] for k in range(N_OFFS)]
    w = _gelu_tanh(h_ref[...]) * gate_ref[...]
    hi, lo = _split_hi_lo(w)
    wexp_ref[...] = _dot(hi, ex_ref[...]) + _dot(lo, ex_ref[...])

    def token(t, _):
        a = jnp.broadcast_to(wexp_ref[pl.ds(t, 1), :], diag_ref.shape) * diag_ref[...]
        ahi = a.astype(_bf16).astype(_f32)
        a2 = jnp.concatenate([ahi, a - ahi], axis=0).astype(_bf16)
        acc = jnp.zeros((2 * 8, LANES), _f32)
        tok_rows = row_ref.at[pl.ds(t * NPICK, NPICK)]
        for c in range(PEER_HEADS):
            rows = _gather_head(tab_ref, tok_rows, c, offs)
            acc = acc + _dot(a2[:, c * CHUNK_ROWS:(c + 1) * CHUNK_ROWS], rows)
        y_ref[t] = acc[:8] + acc[8:]
        return 0

    lax.fori_loop(0, PEER_TT, token, 0, unroll=PEER_DOWN_UNROLL)


def _peer_down(rows_flat, h2, gate2, tab, expand, diag):
    t = h2.shape[0]
    tt = PEER_TT
    full = lambda a: pl.BlockSpec(a.shape, lambda i: (0,) * a.ndim)
    return pl.pallas_call(
        _peer_down_kernel,
        grid=(t // tt,),
        in_specs=[pl.BlockSpec((tt * NPICK,), lambda i: (i,), memory_space=pltpu.SMEM),
                  pl.BlockSpec((N_OFFS,), lambda i: (0,), memory_space=pltpu.SMEM),
                  pl.BlockSpec((tt, SLOTS), lambda i: (i, 0)),
                  pl.BlockSpec((tt, SLOTS), lambda i: (i, 0)),
                  pl.BlockSpec(tab.shape, lambda i: (0, 0), pipeline_mode=pl.Buffered(1)),
                  full(expand), full(diag)],
        out_specs=pl.BlockSpec((tt, 8, LANES), lambda i: (i, 0, 0)),
        out_shape=jax.ShapeDtypeStruct((t, 8, LANES), _f32),
        scratch_shapes=[pltpu.VMEM((tt, STAGE_COLS), _f32)],
        compiler_params=_cparams(("arbitrary",)),
        name="peer_down",
    )(rows_flat, jnp.arange(N_OFFS, dtype=jnp.int32), h2, gate2, tab, expand, diag)


def _peer_constants():
    col = np.arange(STAGE_COLS)
    p, r = col // PAIR_ROWS, col % PAIR_ROWS
    slot = (p // PEER_TOPK) * 2 * PEER_TOPK + (r % 2) * PEER_TOPK + p % PEER_TOPK
    diag = (r[None, :] // 2 == np.arange(8)[:, None]).astype(np.float32)
    fold = (slot[:, None] == np.arange(SLOTS)[None, :]).astype(np.float32)
    gsum = lambda tt: jnp.asarray(np.arange(tt * 8)[None, :] // 8 == np.arange(tt)[:, None], _bf16)
    return (jnp.asarray(diag, _f32), jnp.asarray(fold, _bf16), jnp.asarray(fold.T, _bf16),
            gsum(PEER_TT), gsum(FUSE_TT))


FIN_TM = 512


def _final_norm_kernel(x_ref, y_ref, g_ref, o_ref):
    z = x_ref[...] + y_ref[...]
    o_ref[...] = z * lax.rsqrt(jnp.mean(z * z, axis=-1, keepdims=True) + EPS) * g_ref[...]


def _final_norm(x1, y, g):
    t = x1.shape[0]
    row = pl.BlockSpec((FIN_TM, D_MODEL), lambda i: (i, 0))
    return pl.pallas_call(
        _final_norm_kernel,
        grid=(t // FIN_TM,),
        in_specs=[row, row, pl.BlockSpec((1, D_MODEL), lambda i: (0, 0))],
        out_specs=row,
        out_shape=jax.ShapeDtypeStruct((t, D_MODEL), _f32),
        compiler_params=_cparams(("parallel",)),
        name="final_norm",
    )(x1, y, g.reshape(1, D_MODEL))


def _cmp_to_sel_counts(n_rows, n_sel):
    r = SEL_LEN // CMP_STRIDE
    c = CMP_LEN // CMP_STRIDE
    off = np.arange(n_rows)[:, None] - r * np.arange(n_sel)[None, :] + (c - 1)
    counts = np.zeros((n_rows, LANES), np.float32)
    for n in range(c):
        counts[:, :n_sel] += ((off - n >= 0) & (off - n < r)).astype(np.float32)
    counts[n_rows - 1] = 0.0
    return counts


def _attention(x2d, b, s, w_in, norm_mix_g, rel_bias, lam_rows, subln_g,
               cmp_pos_k, cmp_pos_v, cmp_w1_k, cmp_w2_k, cmp_w1_v, cmp_w2_v):
    dq, dv, nq, kcvc, vs2, vw2, ng, dkT, ksT, kwT = _in_proj(x2d, norm_mix_g, *_arrange_w_in(w_in))
    to3 = lambda a: a.reshape(b, s, a.shape[-1])

    bt_diff = _near_tiles(_bias_tiles(rel_bias[:, :DIFF_HEADS], DIFF_TQ, DIFF_TQ, 2))
    o_diff = _diff_attn(to3(dq), dkT, to3(dv), bt_diff, lam_rows, subln_g)

    n = s // CMP_STRIDE
    a = kcvc.reshape(b, n, CMP_STRIDE, 2, NSA_KV_HEADS, NSA_HD)
    a = jnp.transpose(a, (0, 3, 4, 1, 2, 5)).reshape(b, 2, NSA_KV_HEADS, n, CMP_STRIDE * NSA_HD)
    pos = jnp.stack([cmp_pos_k, cmp_pos_v]).reshape(2, 2, 1, CMP_STRIDE * NSA_HD)
    w1 = jnp.stack([cmp_w1_k, cmp_w1_v]).astype(_bf16)
    w2 = jnp.stack([cmp_w2_k, cmp_w2_v])
    w2dup = jnp.concatenate([w2, w2], axis=-1).astype(_bf16)
    cmp_kv = _compress(a, pos, w1, w2dup)

    n_sel = s // SEL_LEN
    counts = jnp.asarray(_cmp_to_sel_counts(n, n_sel), _bf16)
    nq3 = to3(nq)
    o_cmp, sel = _cmp_attn(nq3, cmp_kv[:, 0], cmp_kv[:, 1], counts, n_sel)

    nsa_bias = rel_bias[:, DIFF_HEADS:]
    tq = NSA_TQ
    def group_tiles(n_tiles, window=None):
        bt = _bias_tiles(nsa_bias, tq, tq, n_tiles, window=window)
        bt = bt.reshape(NSA_KV_HEADS, NSA_GROUP, n_tiles, tq, tq)[:, np.asarray(GROUP_ORDER)]
        return jnp.transpose(bt, (0, 2, 1, 3, 4)).reshape(NSA_KV_HEADS, n_tiles, NSA_GROUP * tq, tq)

    expand = jnp.asarray(np.arange(s)[None, :] // SEL_LEN == np.arange(LANES)[:, None], _bf16)
    o_sel = _sel_attn(nq3, ksT, to3(vs2), sel, expand, _near_tiles(group_tiles(2)))

    bt_win = jnp.transpose(group_tiles(WIN_TILES, window=WINDOW), (0, 2, 1, 3))
    bt_win = bt_win.reshape(NSA_KV_HEADS, NSA_GROUP * tq, WIN_TILES * tq)
    o_win = _win_attn(nq3, kwT, to3(vw2), bt_win)

    flat = lambda a3: a3.reshape(b * s, a3.shape[-1])
    return flat(o_diff), flat(o_cmp), flat(o_sel), flat(o_win), ng


def _gate_expand():
    ge = np.zeros((3, LANES, NSA_WIDTH), np.float32)
    for head in range(NSA_HEADS):
        for br in range(3):
            ge[br, 3 * head + br, head * NSA_HD:(head + 1) * NSA_HD] = 1.0
    return jnp.asarray(ge, _bf16)


PEER_PARTS = 8


def _peer(x1, xn, qp, peer_sub_keys, peer_u, peer_v):
    t = x1.shape[0]
    sk = peer_sub_keys.reshape(2 * PEER_HEADS, PEER_NKEYS, PEER_DKEY // 2).astype(_bf16)
    diag, fold, expand, gsum, gsum_fuse = _peer_constants()
    x8 = xn.reshape(t, 8, LANES)
    tab_u = _pair_table(peer_u)
    n = t // PEER_PARTS
    rows, gate = _peer_route(qp, sk, n)
    rows_all, gate_all, h_all = [rows], [gate], []
    for k in range(PEER_PARTS - 1):
        h, rows, gate = _peer_up_route(rows.reshape(n * NPICK), x8, tab_u, diag, fold, gsum_fuse, qp, sk,
                                       n, k * n, (k + 1) * n)
        h_all.append(h)
        rows_all.append(rows)
        gate_all.append(gate)
    h_all.append(_peer_up(rows.reshape(n * NPICK), x8, tab_u, diag, fold, gsum, n, (PEER_PARTS - 1) * n))
    rows_flat = jnp.concatenate(rows_all, axis=0).reshape(t * NPICK)
    y = _peer_down(rows_flat, jnp.concatenate(h_all, axis=0), jnp.concatenate(gate_all, axis=0),
                   _pair_table(peer_v), expand, diag)
    return y.reshape(t, D_MODEL)


def kernel(x, w_in, w_out, norm_mix_g, norm_ffn_g, final_norm_g, rel_bias, diff_lq1, diff_lk1, diff_lq2, diff_lk2, diff_subln_g, cmp_pos_k, cmp_pos_v, cmp_w1_k, cmp_w2_k, cmp_w1_v, cmp_w2_v, peer_w_q, peer_sub_keys, peer_u, peer_v):
    b, s, d = x.shape
    x2d = x.reshape(b * s, d)
    lam_rows = jnp.zeros((8, LANES), _f32).at[0:4, :DIFF_HD].set(
        jnp.stack([diff_lq1[0], diff_lk1[0], diff_lq2[0], diff_lk2[0]]))
    o_diff, o_cmp, o_sel, o_win, ng = _attention(
        x2d, b, s, w_in[0], norm_mix_g[0], rel_bias, lam_rows, diff_subln_g[0],
        cmp_pos_k[0], cmp_pos_v[0], cmp_w1_k[0], cmp_w2_k[0], cmp_w1_v[0], cmp_w2_v[0])
    x1, xn, qp = _mix_out(x2d, o_diff, o_cmp, o_sel, o_win, ng, _gate_expand(),
                          w_out[0].astype(_bf16), norm_ffn_g[0], peer_w_q[0].astype(_bf16))
    y = _peer(x1, xn, qp, peer_sub_keys[0], peer_u[0], peer_v[0])
    return _final_norm(x1, y, final_norm_g).reshape(b, s, d)
```

```python
import functools
import math

import numpy as np
import jax
import jax.numpy as jnp
from jax import lax
from jax.experimental import pallas as pl
from jax.experimental.pallas import tpu as pltpu

D_MODEL = 1024
DIFF_HEADS = 4
DIFF_HD = 64
DIFF_VD = 2 * DIFF_HD
DIFF_WIDTH = DIFF_HEADS * DIFF_VD
NSA_HEADS = 8
NSA_KV_HEADS = 2
NSA_GROUP = NSA_HEADS // NSA_KV_HEADS
NSA_HD = 64
NSA_WIDTH = NSA_HEADS * NSA_HD
CMP_LEN = 32
CMP_STRIDE = 16
CMP_HIDDEN = 256
SEL_LEN = 64
SEL_TOPN = 8
WINDOW = 512
FORCE_BONUS = 1000.0
NEG_INF = -1e30
N_BUCKETS = 32
MAX_DISTANCE = 128
PEER_HEADS = 8
PEER_NKEYS = 128
PEER_EXPERTS = PEER_NKEYS ** 2
PEER_DKEY = 256
PEER_TOPK = 16
EPS = 1e-6
LAMBDA_INIT = 0.8 - 0.6 * math.exp(-0.3 * 0)
LOG2E = math.log2(math.e)

KV_COLS = NSA_KV_HEADS * NSA_HD
SPLIT_SIZES = (DIFF_HEADS * 2 * DIFF_HD, DIFF_HEADS * 2 * DIFF_HD, DIFF_WIDTH, NSA_WIDTH,
               KV_COLS, KV_COLS, KV_COLS, KV_COLS, KV_COLS, KV_COLS, 3 * NSA_HEADS)
SPLIT_OFF = tuple(int(v) for v in np.concatenate([[0], np.cumsum(SPLIT_SIZES)]))

LANES = 128
VMEM_LIMIT = 56 * 1024 * 1024
HALF_EXPERTS = PEER_EXPERTS // 2
PAIR_ROWS = 16
PAIR_WORD_ROWS = PAIR_ROWS // 2

_f32 = jnp.float32
_bf16 = jnp.bfloat16


def _cparams(sem):
    return pltpu.CompilerParams(dimension_semantics=sem, vmem_limit_bytes=VMEM_LIMIT)


def _dot(a, b):
    return jnp.dot(a, b, preferred_element_type=_f32)


def _dot_nt(a, b):
    return lax.dot_general(a, b, (((1,), (1,)), ((), ())), preferred_element_type=_f32)


def _split_hi_lo(x):
    hi = x.astype(_bf16)
    lo = (x - hi.astype(_f32)).astype(_bf16)
    return hi, lo


def _gelu_tanh(x):
    c = math.sqrt(2.0 / math.pi)
    return 0.5 * x * (1.0 + jnp.tanh(c * (x + 0.044715 * (x * x * x))))


IN_TM = 512
_IN_GROUPS = (("dq", 512, _bf16), ("dv", 512, _bf16), ("nq", 512, _bf16), ("kcvc", 256, _f32),
              ("vs2", 256, _bf16), ("vw2", 256, _bf16), ("ng", 128, _f32))
_IN_KEYS_T = (("dkT", 512), ("ksT", 256), ("kwT", 256))


def _in_proj_kernel(x_ref, g_ref, w_ref, wt_ref, *out_refs):
    x = x_ref[...]
    y = x * lax.rsqrt(jnp.mean(x * x, axis=-1, keepdims=True) + EPS)
    h = (y * g_ref[...]).astype(_bf16)
    off = 0
    for (_, width, dt), o_ref in zip(_IN_GROUPS, out_refs):
        o_ref[...] = _dot(h, w_ref[:, off:off + width]).astype(dt)
        off += width
    off = 0
    for (_, width), o_ref in zip(_IN_KEYS_T, out_refs[len(_IN_GROUPS):]):
        o_ref[...] = _dot_nt(wt_ref[off:off + width, :], h).astype(o_ref.dtype)
        off += width


def _arrange_w_in(w_in):
    o = SPLIT_OFF
    sl = lambda i: w_in[:, o[i]:o[i + 1]]
    dup = lambda w: jnp.concatenate([w[:, :64], w[:, :64], w[:, 64:], w[:, 64:]], axis=1)
    ng = jnp.pad(sl(10), ((0, 0), (0, LANES - 3 * NSA_HEADS)))
    cols = [sl(0) * (DIFF_HD ** -0.5 * LOG2E), sl(2), sl(3) * (NSA_HD ** -0.5 * LOG2E),
            sl(4), sl(5), dup(sl(7)), dup(sl(9)), ng]
    keys = [sl(1), dup(sl(6)), dup(sl(8))]
    return (jnp.concatenate(cols, axis=1).astype(_bf16),
            jnp.concatenate(keys, axis=1).T.astype(_bf16))


def _in_proj(x2d, g, w_arr, wt_arr):
    t = x2d.shape[0]
    out_shape = ([jax.ShapeDtypeStruct((t, wd), dt) for _, wd, dt in _IN_GROUPS]
                 + [jax.ShapeDtypeStruct((wd, t), _bf16) for _, wd in _IN_KEYS_T])
    out_specs = ([pl.BlockSpec((IN_TM, wd), lambda i: (i, 0)) for _, wd, _ in _IN_GROUPS]
                 + [pl.BlockSpec((wd, IN_TM), lambda i: (0, i)) for _, wd in _IN_KEYS_T])
    return pl.pallas_call(
        _in_proj_kernel,
        grid=(t // IN_TM,),
        in_specs=[pl.BlockSpec((IN_TM, D_MODEL), lambda i: (i, 0)),
                  pl.BlockSpec((1, D_MODEL), lambda i: (0, 0)),
                  pl.BlockSpec(w_arr.shape, lambda i: (0, 0)),
                  pl.BlockSpec(wt_arr.shape, lambda i: (0, 0))],
        out_specs=out_specs,
        out_shape=out_shape,
        compiler_params=_cparams(("parallel",)),
        name="in_proj",
    )(x2d, g.reshape(1, D_MODEL), w_arr, wt_arr)


def _bucket_table(n):
    rel = np.arange(n)
    max_exact = N_BUCKETS // 2
    nf = np.maximum(rel, 1).astype(np.float32)
    large = max_exact + (np.log(nf / np.float32(max_exact)) / np.float32(math.log(MAX_DISTANCE / max_exact))
                         * np.float32(N_BUCKETS - max_exact)).astype(np.int32)
    large = np.minimum(large, N_BUCKETS - 1)
    return np.where(rel < max_exact, rel, large).astype(np.int32)


def _bias_tiles(rel_bias_heads, tq, tk, n_tiles, window=None):
    tab = (rel_bias_heads.astype(_f32) - rel_bias_heads[N_BUCKETS - 1][None, :].astype(_f32)) * LOG2E
    period = tq + tk
    m = np.arange(period)
    off = np.where(m < tk, -m, period - m)
    rel_w = np.arange(n_tiles)[:, None] * tq + off[None, :]
    bucket = _bucket_table(n_tiles * tq + period)[np.clip(rel_w, 0, None)]
    w = jnp.transpose(tab[bucket], (2, 0, 1))
    tiles = jnp.tile(w, (1, 1, tq))[..., :tq * (period - 1)]
    tiles = tiles.reshape(w.shape[0], n_tiles, tq, period - 1)[..., :tk]
    rel = (np.arange(n_tiles)[:, None, None] * tq + np.arange(tq)[None, :, None]
           - np.arange(tk)[None, None, :])
    ok = rel >= 0
    if window is not None:
        ok &= rel < window
    return jnp.where(jnp.asarray(ok)[None], tiles, NEG_INF)


def _near_tiles(bt):
    first = jnp.concatenate([bt[..., 0, :, :], jnp.full_like(bt[..., 0, :, :], NEG_INF)], axis=-1)
    later = jnp.concatenate([bt[..., 1, :, :], bt[..., 0, :, :]], axis=-1)
    return jnp.stack([first, later], axis=-3)


def _attend_first(s, v_aug):
    m = jnp.max(s, axis=-1, keepdims=True)
    return m, _dot(jnp.exp2(s - m).astype(_bf16), v_aug)


def _attend_step(carry, s, v_aug):
    m, acc = carry
    m_new = jnp.maximum(m, jnp.max(s, axis=-1, keepdims=True))
    acc = jnp.exp2(m - m_new) * acc + _dot(jnp.exp2(s - m_new).astype(_bf16), v_aug)
    return m_new, acc


def _lane_half_mask(shape):
    return lax.broadcasted_iota(jnp.int32, shape, len(shape) - 1) < (LANES // 2)


DIFF_TQ = 256


def _diff_attn_kernel(q_ref, k_ref, v_ref, bt_ref, lam_ref, g_ref, o_ref):
    tq = DIFF_TQ
    qb = pl.program_id(2)
    q = q_ref[...]
    lo = _lane_half_mask(q.shape)
    zero = jnp.zeros_like(q)
    q2 = jnp.concatenate([jnp.where(lo, q, zero), jnp.where(lo, zero, q)], axis=0)

    def kv(start, n):
        start = pl.multiple_of(start, tq)
        v = v_ref[pl.ds(start, n), :]
        return k_ref[:, pl.ds(start, n)], jnp.concatenate([v, jnp.ones_like(v)], axis=1)

    def both(b):
        return jnp.concatenate([b, b], axis=0)

    k, v = kv(jnp.maximum(qb - 1, 0) * tq, 2 * tq)
    carry = _attend_first(_dot(q2, k) + both(bt_ref[jnp.minimum(qb, 1)]), v)

    n_far = jnp.maximum(qb - 1, 0)

    def odd(c):
        k, v = kv((n_far - 1) * tq, tq)
        return _attend_step(c, _dot(q2, k), v)

    carry = lax.cond(n_far % 2 == 1, odd, lambda c: c, carry)

    def pair(i, c):
        k, v = kv(i * 2 * tq, 2 * tq)
        return _attend_step(c, _dot(q2, k), v)

    m, acc = lax.fori_loop(0, n_far // 2, pair, carry)
    o2 = acc[:, :DIFF_VD] / acc[:, DIFF_VD:]
    lv = lam_ref[...]
    lam = (jnp.exp(jnp.sum(lv[0:1] * lv[1:2], axis=-1, keepdims=True))
           - jnp.exp(jnp.sum(lv[2:3] * lv[3:4], axis=-1, keepdims=True)) + LAMBDA_INIT)
    o = o2[:tq] - lam * o2[tq:]
    y = o * lax.rsqrt(jnp.mean(o * o, axis=-1, keepdims=True) + EPS)
    o_ref[...] = (y * g_ref[...] * (1.0 - LAMBDA_INIT)).astype(o_ref.dtype)


def _diff_attn(dq, dkT, dv, bt, lam_rows, subln_g):
    b, s, _ = dq.shape
    tq = DIFF_TQ
    return pl.pallas_call(
        _diff_attn_kernel,
        grid=(b, DIFF_HEADS, s // tq),
        in_specs=[pl.BlockSpec((None, tq, LANES), lambda bi, h, qb: (bi, qb, h)),
                  pl.BlockSpec((LANES, s), lambda bi, h, qb: (h, bi)),
                  pl.BlockSpec((None, s, LANES), lambda bi, h, qb: (bi, 0, h)),
                  pl.BlockSpec((None, 2, tq, 2 * tq), lambda bi, h, qb: (h, 0, 0, 0)),
                  pl.BlockSpec((8, LANES), lambda bi, h, qb: (0, 0)),
                  pl.BlockSpec((1, LANES), lambda bi, h, qb: (0, 0))],
        out_specs=pl.BlockSpec((None, tq, LANES), lambda bi, h, qb: (bi, qb, h)),
        out_shape=jax.ShapeDtypeStruct((b, s, DIFF_WIDTH), _bf16),
        compiler_params=_cparams(("parallel", "parallel", "arbitrary")),
        name="diff_attn",
    )(dq, dkT, dv, bt, lam_rows, subln_g.reshape(1, LANES))


def _compress_kernel(a_ref, pos_ref, w1_ref, w2_ref, o_ref):
    a = a_ref[...]
    half = (CMP_LEN // 2) * NSA_HD
    za = _dot((a + pos_ref[0]).astype(_bf16), w1_ref[:half, :])
    zb = _dot((a + pos_ref[1]).astype(_bf16), w1_ref[half:, :])
    n = a.shape[0]
    hid = za + pltpu.roll(zb, n - 1, 0)
    o_ref[...] = _dot(_gelu_tanh(hid).astype(_bf16), w2_ref[...]).astype(o_ref.dtype)


def _compress(a, pos, w1, w2dup):
    b, _, _, n, width = a.shape
    return pl.pallas_call(
        _compress_kernel,
        grid=(b, 2, NSA_KV_HEADS),
        in_specs=[pl.BlockSpec((None, None, None, n, width), lambda bi, kv, g: (bi, kv, g, 0, 0)),
                  pl.BlockSpec((None, 2, 1, width), lambda bi, kv, g: (kv, 0, 0, 0)),
                  pl.BlockSpec((None, 2 * width, CMP_HIDDEN), lambda bi, kv, g: (kv, 0, 0)),
                  pl.BlockSpec((None, CMP_HIDDEN, LANES), lambda bi, kv, g: (kv, 0, 0))],
        out_specs=pl.BlockSpec((None, None, None, n, LANES), lambda bi, kv, g: (bi, kv, g, 0, 0)),
        out_shape=jax.ShapeDtypeStruct((b, 2, NSA_KV_HEADS, n, LANES), _bf16),
        compiler_params=_cparams(("parallel", "parallel", "parallel")),
        name="compress",
    )(a, pos, w1, w2dup)


NSA_TQ = 256


GROUP_ORDER = (0, 2, 1, 3)


def _stack_group_queries(q):
    rows = []
    for j in GROUP_ORDER:
        blk = q[:, (j // 2) * LANES:(j // 2 + 1) * LANES]
        lo = _lane_half_mask(blk.shape)
        keep = lo if j % 2 == 0 else jnp.logical_not(lo)
        rows.append(jnp.where(keep, blk, jnp.zeros_like(blk)))
    return jnp.concatenate(rows, axis=0)


def _unstack_group_outputs(o4, tq):
    lo = _lane_half_mask((tq, LANES))
    half = NSA_GROUP // 2
    pairs = [jnp.where(lo, o4[i * tq:(i + 1) * tq], o4[(half + i) * tq:(half + i + 1) * tq])
             for i in range(half)]
    return jnp.concatenate(pairs, axis=1)


def _group_values(v):
    lo = _lane_half_mask(v.shape)
    one = jnp.ones_like(v)
    return jnp.where(lo, v, one), jnp.where(lo, one, v)


def _group_pv(p, v_pair):
    n = p.shape[0] // 2
    p = p.astype(_bf16)
    return jnp.concatenate([_dot(p[:n], v_pair[0]), _dot(p[n:], v_pair[1])], axis=0)


def _group_attend_first(s, v_pair):
    m = jnp.max(s, axis=-1, keepdims=True)
    return m, _group_pv(jnp.exp2(s - m), v_pair)


def _group_attend_step(carry, s, v_pair):
    m, acc = carry
    m_new = jnp.maximum(m, jnp.max(s, axis=-1, keepdims=True))
    return m_new, jnp.exp2(m - m_new) * acc + _group_pv(jnp.exp2(s - m_new), v_pair)


def _group_normalise(acc):
    return acc / pltpu.roll(acc, LANES // 2, 1)


def _cmp_attn_kernel(q_ref, kc_ref, vc_ref, cs_ref, o_ref, sel_ref, *, n_sel):
    tq = NSA_TQ
    qb = pl.program_id(2)
    q4 = _stack_group_queries(q_ref[...])
    s = _dot_nt(q4, kc_ref[...])
    row = lax.broadcasted_iota(jnp.int32, s.shape, 0)
    pos4 = qb * tq + (row & (tq - 1))
    c = lax.broadcasted_iota(jnp.int32, s.shape, 1)
    ok = (c * CMP_STRIDE + (CMP_LEN - 1)) <= pos4
    s = jnp.where(ok, s, NEG_INF)
    m = jnp.max(s, axis=-1, keepdims=True)
    e = jnp.where(ok, jnp.exp2(s - m), 0.0)
    l = jnp.sum(e, axis=-1, keepdims=True)
    p = e / jnp.where(l > 0.0, l, 1.0)
    o_ref[...] = _unstack_group_outputs(_dot(p.astype(_bf16), vc_ref[...]), tq).astype(o_ref.dtype)

    psum = p[0:tq] + p[tq:2 * tq] + p[2 * tq:3 * tq] + p[3 * tq:4 * tq]
    hi, lo = _split_hi_lo(psum)
    imp = _dot(hi, cs_ref[...]) + _dot(lo, cs_ref[...])
    blk = lax.broadcasted_iota(jnp.int32, imp.shape, 1)
    pos = qb * tq + lax.broadcasted_iota(jnp.int32, imp.shape, 0)
    cur = pos // SEL_LEN
    forced = (blk == 0) | (blk == cur) | (blk == cur - 1)
    score = jnp.where(blk <= cur, imp + jnp.where(forced, FORCE_BONUS, 0.0), NEG_INF)
    st = score.T[:n_sel]
    blk_t = lax.broadcasted_iota(jnp.int32, st.shape, 0)
    rank = jnp.zeros(st.shape, jnp.int32)
    for i in range(n_sel):
        si = st[i:i + 1]
        beats = (si > st) | ((si == st) & (i < blk_t))
        rank = rank + beats.astype(jnp.int32)
    sel_t = ((rank < SEL_TOPN) & (st > 0.5 * NEG_INF)).astype(_f32)
    sel = jnp.concatenate([sel_t, jnp.zeros((LANES - n_sel, tq), _f32)], axis=0).T
    sel_ref[...] = sel.astype(sel_ref.dtype)


def _cmp_attn(nq, kcmp, vcmp, cmp_sel, n_sel):
    b, s, _ = nq.shape
    tq = NSA_TQ
    n = kcmp.shape[-2]
    gw = NSA_GROUP * NSA_HD
    return pl.pallas_call(
        functools.partial(_cmp_attn_kernel, n_sel=n_sel),
        grid=(b, NSA_KV_HEADS, s // tq),
        in_specs=[pl.BlockSpec((None, tq, gw), lambda bi, g, qb: (bi, qb, g)),
                  pl.BlockSpec((None, None, n, LANES), lambda bi, g, qb: (bi, g, 0, 0)),
                  pl.BlockSpec((None, None, n, LANES), lambda bi, g, qb: (bi, g, 0, 0)),
                  pl.BlockSpec((n, LANES), lambda bi, g, qb: (0, 0))],
        out_specs=[pl.BlockSpec((None, tq, gw), lambda bi, g, qb: (bi, qb, g)),
                   pl.BlockSpec((None, None, tq, LANES), lambda bi, g, qb: (bi, g, qb, 0))],
        out_shape=[jax.ShapeDtypeStruct((b, s, NSA_WIDTH), _bf16),
                   jax.ShapeDtypeStruct((b, NSA_KV_HEADS, s, LANES), _bf16)],
        compiler_params=_cparams(("parallel", "parallel", "arbitrary")),
        name="cmp_attn",
    )(nq, kcmp, vcmp, cmp_sel)


SEL_CHUNK = 4 * NSA_TQ


def _sel_attn_kernel(q_ref, k_ref, v_ref, sel_ref, ex_ref, bt_ref, o_ref):
    tq = NSA_TQ
    qb = pl.program_id(2)
    q4 = _stack_group_queries(q_ref[...])
    sel = sel_ref[...]

    def scores(pen_rows, start, n):
        start = pl.multiple_of(start, tq)
        kx = jnp.concatenate([k_ref[:, pl.ds(start, n)], ex_ref[:, pl.ds(start, n)]], axis=0)
        qx = jnp.concatenate([q4, jnp.concatenate([pen_rows] * NSA_GROUP, axis=0)], axis=1)
        return _dot(qx, kx), _group_values(v_ref[pl.ds(start, n), :])

    def penalty(sel_rows):
        return ((sel_rows.astype(_f32) - 1.0) * (-NEG_INF)).astype(_bf16)

    s, v = scores(penalty(sel), jnp.maximum(qb - 1, 0) * tq, 2 * tq)
    carry = _group_attend_first(s + bt_ref[jnp.minimum(qb, 1)], v)

    far_len = jnp.maximum(qb - 1, 0) * tq
    blk = lax.broadcasted_iota(jnp.int32, sel.shape, 1)
    pen_far = penalty(jnp.where(blk * SEL_LEN < far_len, sel, jnp.zeros_like(sel)))

    def far(i, c):
        s, v = scores(pen_far, i * SEL_CHUNK, SEL_CHUNK)
        return _group_attend_step(c, s, v)

    m, acc = lax.fori_loop(0, (far_len + SEL_CHUNK - 1) // SEL_CHUNK, far, carry)
    o_ref[...] = _unstack_group_outputs(_group_normalise(acc), tq).astype(o_ref.dtype)


def _sel_attn(nq, ks2, vs2, sel, expand, bt):
    b, s, _ = nq.shape
    tq = NSA_TQ
    gw = NSA_GROUP * NSA_HD
    return pl.pallas_call(
        _sel_attn_kernel,
        grid=(b, NSA_KV_HEADS, s // tq),
        in_specs=[pl.BlockSpec((None, tq, gw), lambda bi, g, qb: (bi, qb, g)),
                  pl.BlockSpec((LANES, s), lambda bi, g, qb: (g, bi)),
                  pl.BlockSpec((None, s, LANES), lambda bi, g, qb: (bi, 0, g)),
                  pl.BlockSpec((None, None, tq, LANES), lambda bi, g, qb: (bi, g, qb, 0)),
                  pl.BlockSpec((LANES, s), lambda bi, g, qb: (0, 0)),
                  pl.BlockSpec((None, 2, NSA_GROUP * tq, 2 * tq), lambda bi, g, qb: (g, 0, 0, 0))],
        out_specs=pl.BlockSpec((None, tq, gw), lambda bi, g, qb: (bi, qb, g)),
        out_shape=jax.ShapeDtypeStruct((b, s, NSA_WIDTH), _bf16),
        compiler_params=_cparams(("parallel", "parallel", "arbitrary")),
        name="sel_attn",
    )(nq, ks2, vs2, sel, expand, bt)


WIN_TILES = WINDOW // NSA_TQ + 1


def _win_attn_kernel(q_ref, k_ref, v_ref, bt_ref, o_ref):
    tq = NSA_TQ
    qb = pl.program_id(2)
    q4 = _stack_group_queries(q_ref[...])
    s_blocks, va, vb = [], [], []
    for d in range(WIN_TILES):
        start = pl.multiple_of(jnp.maximum(qb - d, 0) * tq, tq)
        s = _dot(q4, k_ref[:, pl.ds(start, tq)])
        s_blocks.append(s if d == 0 else s + jnp.where(qb >= d, 0.0, NEG_INF))
        a, b = _group_values(v_ref[pl.ds(start, tq), :])
        va.append(a)
        vb.append(b)
    s = jnp.concatenate(s_blocks, axis=1) + bt_ref[...]
    m, acc = _group_attend_first(s, (jnp.concatenate(va, axis=0), jnp.concatenate(vb, axis=0)))
    o_ref[...] = _unstack_group_outputs(_group_normalise(acc), tq).astype(o_ref.dtype)


def _win_attn(nq, kw2, vw2, bt):
    b, s, _ = nq.shape
    tq = NSA_TQ
    gw = NSA_GROUP * NSA_HD
    return pl.pallas_call(
        _win_attn_kernel,
        grid=(b, NSA_KV_HEADS, s // tq),
        in_specs=[pl.BlockSpec((None, tq, gw), lambda bi, g, qb: (bi, qb, g)),
                  pl.BlockSpec((LANES, s), lambda bi, g, qb: (g, bi)),
                  pl.BlockSpec((None, s, LANES), lambda bi, g, qb: (bi, 0, g)),
                  pl.BlockSpec((None, NSA_GROUP * tq, WIN_TILES * tq), lambda bi, g, qb: (g, 0, 0))],
        out_specs=pl.BlockSpec((None, tq, gw), lambda bi, g, qb: (bi, qb, g)),
        out_shape=jax.ShapeDtypeStruct((b, s, NSA_WIDTH), _bf16),
        compiler_params=_cparams(("parallel", "parallel", "arbitrary")),
        name="win_attn",
    )(nq, kw2, vw2, bt)


MIX_TM = 512


def _mix_out_kernel(x_ref, od_ref, oc_ref, os_ref, ow_ref, ng_ref, ge_ref, wo_ref, g_ref, wq_ref,
                    x1_ref, xn_ref, qp_ref):
    sig = jax.nn.sigmoid(ng_ref[...])
    hi, lo = _split_hi_lo(sig)
    o_nsa = jnp.zeros(oc_ref.shape, _f32)
    for br, o_ref in enumerate((oc_ref, os_ref, ow_ref)):
        gate = _dot(hi, ge_ref[br]) + _dot(lo, ge_ref[br])
        o_nsa = o_nsa + gate * o_ref[...].astype(_f32)
    y = _dot(od_ref[...], wo_ref[:DIFF_WIDTH, :]) + _dot(o_nsa.astype(_bf16), wo_ref[DIFF_WIDTH:, :])
    x1 = x_ref[...] + y
    x1_ref[...] = x1
    xn = x1 * lax.rsqrt(jnp.mean(x1 * x1, axis=-1, keepdims=True) + EPS) * g_ref[...]
    xn = xn.astype(_bf16)
    xn_ref[...] = xn
    qp_ref[...] = _dot(xn, wq_ref[...]).astype(qp_ref.dtype)


def _mix_out(x2d, o_diff, o_cmp, o_sel, o_win, ng, gate_expand, w_out, g_ffn, w_q):
    t = x2d.shape[0]
    tm = MIX_TM
    nq = w_q.shape[1]
    row = lambda w: pl.BlockSpec((tm, w), lambda i: (i, 0))
    full = lambda a: pl.BlockSpec(a.shape, lambda i: (0,) * a.ndim)
    g2 = g_ffn.reshape(1, D_MODEL)
    return pl.pallas_call(
        _mix_out_kernel,
        grid=(t // tm,),
        in_specs=[row(D_MODEL), row(DIFF_WIDTH), row(NSA_WIDTH), row(NSA_WIDTH), row(NSA_WIDTH),
                  row(LANES), full(gate_expand), full(w_out), full(g2), full(w_q)],
        out_specs=[row(D_MODEL), row(D_MODEL), row(nq)],
        out_shape=[jax.ShapeDtypeStruct((t, D_MODEL), _f32),
                   jax.ShapeDtypeStruct((t, D_MODEL), _bf16),
                   jax.ShapeDtypeStruct((t, nq), _bf16)],
        compiler_params=_cparams(("parallel",)),
        name="mix_out",
    )(x2d, o_diff, o_cmp, o_sel, o_win, ng, gate_expand, w_out, g2, w_q)


ROUTE_TM = 256
SLOTS = 2 * PEER_HEADS * PEER_TOPK


def _extract_step(st):
    sc, rows, n = st[0], st[1], st[2]
    m = jnp.max(sc, axis=0, keepdims=True)
    idx = jnp.min(jnp.where(sc == m, rows, n), axis=0, keepdims=True)
    st[3].append(m)
    st[4].append(idx)
    st[0] = jnp.where(rows == idx, -jnp.inf, sc)


def _route_head_steps(h, q_ref, sk_ref, flat_c, keep_c, rows_k, rowt_ref, gatet_ref):
    half_d = PEER_DKEY // 2
    col = pl.multiple_of(h * PEER_DKEY, PEER_DKEY)
    box = {}

    def start():
        q0 = q_ref[:, pl.ds(col, half_d)]
        q1 = q_ref[:, pl.ds(col + half_d, half_d)]
        box["a"] = [_dot_nt(sk_ref[2 * h], q0), rows_k, PEER_NKEYS, [], []]
        box["b"] = [_dot_nt(sk_ref[2 * h + 1], q1), rows_k, PEER_NKEYS, [], []]

    def first_stage():
        _extract_step(box["a"])
        _extract_step(box["b"])

    def pairs():
        sv0, si0 = jnp.concatenate(box["a"][3], axis=0), jnp.concatenate(box["a"][4], axis=0)
        sv1, si1 = jnp.concatenate(box["b"][3], axis=0), jnp.concatenate(box["b"][4], axis=0)
        slabs, ids = [], []
        for a, b0 in _PAIR_SLABS:
            if a is None:
                slabs.append(sv0[8:16] + sv1[0:1])
                ids.append(si0[8:16] * PEER_NKEYS + si1[0:1])
            else:
                slabs.append(sv0[a:a + 1] + sv1[b0:b0 + 8])
                ids.append(si0[a:a + 1] * PEER_NKEYS + si1[b0:b0 + 8])
        cand = jnp.where(keep_c, jnp.concatenate(slabs, axis=0), -jnp.inf)
        box["cidx"] = jnp.concatenate(ids, axis=0)
        box["c"] = [cand, flat_c, PEER_TOPK * PEER_TOPK, [], []]

    def second_stage():
        _extract_step(box["c"])

    def finish():
        cv, ci = jnp.concatenate(box["c"][3], axis=0), jnp.concatenate(box["c"][4], axis=0)
        e = jnp.concatenate(
            [jnp.sum(jnp.where(flat_c == ci[k:k + 1], box["cidx"], 0), axis=0, keepdims=True)
             for k in range(PEER_TOPK)], axis=0)
        ex = jnp.exp(cv - cv[0:1])
        gate = ex / jnp.sum(ex, axis=0, keepdims=True)
        upper = e >= HALF_EXPERTS
        base = pl.multiple_of(h * PEER_TOPK, PEER_TOPK)
        rowt_ref[pl.ds(base, PEER_TOPK), :] = (e & (HALF_EXPERTS - 1)) * PAIR_WORD_ROWS
        base2 = pl.multiple_of(h * 2 * PEER_TOPK, 2 * PEER_TOPK)
        gatet_ref[pl.ds(base2, PEER_TOPK), :] = jnp.where(upper, 0.0, gate)
        gatet_ref[pl.ds(base2 + PEER_TOPK, PEER_TOPK), :] = jnp.where(upper, gate, 0.0)

    return [start] + [first_stage] * PEER_TOPK + [pairs] + [second_stage] * PEER_TOPK + [finish]


def _route_head(h, q_ref, sk_ref, flat_c, keep_c, rows_k, rowt_ref, gatet_ref):
    for step in _route_head_steps(h, q_ref, sk_ref, flat_c, keep_c, rows_k, rowt_ref, gatet_ref):
        step()


def _route_consts(flat_ref, tm):
    rows_k = lax.broadcasted_iota(jnp.int32, (PEER_NKEYS, tm), 0)
    flat_c = jnp.broadcast_to(flat_ref[...], (flat_ref.shape[0], tm))
    return rows_k, flat_c, flat_c >= 0


def _peer_route_kernel(q_ref, sk_ref, flat_ref, row_ref, gate_ref, rowt_ref, gatet_ref):
    rows_k, flat_c, keep_c = _route_consts(flat_ref, ROUTE_TM)

    def head(h, _):
        _route_head(h, q_ref, sk_ref, flat_c, keep_c, rows_k, rowt_ref, gatet_ref)
        return 0

    lax.fori_loop(0, PEER_HEADS, head, 0)
    row_ref[...] = rowt_ref[...].T
    gate_ref[...] = gatet_ref[...].T


def _pair_slabs():
    slabs, flat = [(0, 0), (0, 8)] + [(a, 0) for a in range(1, 8)] + [(None, 0)], []
    for a, b0 in slabs:
        for i in range(8):
            aa, bb = (8 + i, 0) if a is None else (a, b0 + i)
            flat.append(aa * PEER_TOPK + bb if (aa + 1) * (bb + 1) <= PEER_TOPK else -1)
    return tuple(slabs), np.asarray(flat, np.int32).reshape(-1, 1)


_PAIR_SLABS, _PAIR_FLAT = _pair_slabs()


def _peer_route(qp, sk, n_tok):
    t = n_tok
    tm = ROUTE_TM
    npick = PEER_HEADS * PEER_TOPK
    return pl.pallas_call(
        _peer_route_kernel,
        grid=(t // tm,),
        in_specs=[pl.BlockSpec((tm, qp.shape[1]), lambda i: (i, 0)),
                  pl.BlockSpec(sk.shape, lambda i: (0, 0, 0)),
                  pl.BlockSpec(_PAIR_FLAT.shape, lambda i: (0, 0))],
        out_specs=[pl.BlockSpec((tm, npick), lambda i: (i, 0)),
                   pl.BlockSpec((tm, SLOTS), lambda i: (i, 0))],
        out_shape=[jax.ShapeDtypeStruct((t, npick), jnp.int32),
                   jax.ShapeDtypeStruct((t, SLOTS), _f32)],
        scratch_shapes=[pltpu.VMEM((npick, tm), jnp.int32), pltpu.VMEM((SLOTS, tm), _f32)],
        compiler_params=_cparams(("parallel",)),
        name="peer_route",
    )(qp, sk, jnp.asarray(_PAIR_FLAT))


PEER_TT = 64
N_OFFS = 8
PEER_UNROLL = 8
PEER_DOWN_UNROLL = 16
NPICK = PEER_HEADS * PEER_TOPK
CHUNK_ROWS = PEER_TOPK * PAIR_ROWS
STAGE_COLS = NPICK * PAIR_ROWS


def _pair_table(w):
    e = w.shape[0]
    w4 = w.astype(_bf16).reshape(2, e // 2, PAIR_WORD_ROWS, LANES)
    words = lax.bitcast_convert_type(jnp.transpose(w4, (1, 2, 3, 0)), jnp.int32)
    return words.reshape(e // 2 * PAIR_WORD_ROWS, LANES)


def _gather_head(tab_ref, tok_rows, c, offs):
    tiles = []
    n_off = len(offs)
    for k in range(PEER_TOPK):
        if k % n_off == 0:
            part_rows = tok_rows.at[pl.ds(c * PEER_TOPK + k, n_off)]
        r = pl.multiple_of(part_rows[offs[k % n_off]], PAIR_WORD_ROWS)
        tiles.append(pltpu.bitcast(tab_ref[pl.ds(r, PAIR_WORD_ROWS), :], _bf16))
    return jnp.concatenate(tiles, axis=0)


def _peer_up_kernel(row_ref, off_ref, x_ref, tab_ref, diag_ref, fold_ref, gsum_ref, h_ref, z_ref):
    offs = [off_ref[k] for k in range(N_OFFS)]

    def token(t, _):
        x8 = x_ref[t]
        tok_rows = row_ref.at[pl.ds(t * NPICK, NPICK)]
        for c in range(PEER_HEADS):
            rows = _gather_head(tab_ref, tok_rows, c, offs)
            z_ref[t, :, c * CHUNK_ROWS:(c + 1) * CHUNK_ROWS] = _dot_nt(x8, rows)
        return 0

    lax.fori_loop(0, PEER_TT, token, 0, unroll=PEER_UNROLL)
    zm = (z_ref[...] * diag_ref[...][None]).reshape(PEER_TT * 8, STAGE_COLS)
    part = _dot(zm.astype(_bf16), fold_ref[...])
    hi, lo = _split_hi_lo(part)
    h_ref[...] = _dot(gsum_ref[...], hi) + _dot(gsum_ref[...], lo)


def _peer_up(rows_flat, x8, tab, diag, fold, gsum, n_tok, x_off):
    t = n_tok
    tt = PEER_TT
    xo = x_off // tt
    full = lambda a: pl.BlockSpec(a.shape, lambda i: (0,) * a.ndim)
    return pl.pallas_call(
        _peer_up_kernel,
        grid=(t // tt,),
        in_specs=[pl.BlockSpec((tt * NPICK,), lambda i: (i,), memory_space=pltpu.SMEM),
                  pl.BlockSpec((N_OFFS,), lambda i: (0,), memory_space=pltpu.SMEM),
                  pl.BlockSpec((tt, 8, LANES), lambda i: (i + xo, 0, 0)),
                  pl.BlockSpec(tab.shape, lambda i: (0, 0), pipeline_mode=pl.Buffered(1)),
                  full(diag), full(fold), full(gsum)],
        out_specs=pl.BlockSpec((tt, SLOTS), lambda i: (i, 0)),
        out_shape=jax.ShapeDtypeStruct((t, SLOTS), _f32),
        scratch_shapes=[pltpu.VMEM((tt, 8, STAGE_COLS), _f32)],
        compiler_params=_cparams(("arbitrary",)),
        name="peer_up",
    )(rows_flat, jnp.arange(N_OFFS, dtype=jnp.int32), x8, tab, diag, fold, gsum)


FUSE_TT = 128
FUSE_INNER = FUSE_TT // PEER_HEADS


def _peer_up_route_kernel(row_ref, off_ref, x_ref, tab_ref, diag_ref, fold_ref, gsum_ref, q_ref, sk_ref, flat_ref,
                          h_ref, row2_ref, gate2_ref, z_ref, rowt_ref, gatet_ref):
    offs = [off_ref[k] for k in range(N_OFFS)]
    rows_k, flat_c, keep_c = _route_consts(flat_ref, FUSE_TT)

    def trip(j, _):
        steps = _route_head_steps(j, q_ref, sk_ref, flat_c, keep_c, rows_k, rowt_ref, gatet_ref)
        per = -(-len(steps) // FUSE_INNER)
        for i in range(FUSE_INNER):
            t = j * FUSE_INNER + i
            x8 = x_ref[t]
            tok_rows = row_ref.at[pl.ds(t * NPICK, NPICK)]
            for c in range(PEER_HEADS):
                rows = _gather_head(tab_ref, tok_rows, c, offs)
                z_ref[t, :, c * CHUNK_ROWS:(c + 1) * CHUNK_ROWS] = _dot_nt(x8, rows)
            for step in steps[i * per:(i + 1) * per]:
                step()
        return 0

    lax.fori_loop(0, PEER_HEADS, trip, 0)
    row2_ref[...] = rowt_ref[...].T
    gate2_ref[...] = gatet_ref[...].T
    zm = (z_ref[...] * diag_ref[...][None]).reshape(FUSE_TT * 8, STAGE_COLS)
    part = _dot(zm.astype(_bf16), fold_ref[...])
    hi, lo = _split_hi_lo(part)
    h_ref[...] = _dot(gsum_ref[...], hi) + _dot(gsum_ref[...], lo)


def _peer_up_route(rows_flat, x8, tab, diag, fold, gsum, qp, sk, n_tok, x_off, q_off):
    tt = FUSE_TT
    xo, qo = x_off // tt, q_off // tt
    full = lambda a: pl.BlockSpec(a.shape, lambda i: (0,) * a.ndim)
    flat = jnp.asarray(_PAIR_FLAT)
    return pl.pallas_call(
        _peer_up_route_kernel,
        grid=(n_tok // tt,),
        in_specs=[pl.BlockSpec((tt * NPICK,), lambda i: (i,), memory_space=pltpu.SMEM),
                  pl.BlockSpec((N_OFFS,), lambda i: (0,), memory_space=pltpu.SMEM),
                  pl.BlockSpec((tt, 8, LANES), lambda i: (i + xo, 0, 0)),
                  pl.BlockSpec(tab.shape, lambda i: (0, 0), pipeline_mode=pl.Buffered(1)),
                  full(diag), full(fold), full(gsum),
                  pl.BlockSpec((tt, qp.shape[1]), lambda i: (i + qo, 0)), full(sk), full(flat)],
        out_specs=[pl.BlockSpec((tt, SLOTS), lambda i: (i, 0)),
                   pl.BlockSpec((tt, NPICK), lambda i: (i, 0)),
                   pl.BlockSpec((tt, SLOTS), lambda i: (i, 0))],
        out_shape=[jax.ShapeDtypeStruct((n_tok, SLOTS), _f32),
                   jax.ShapeDtypeStruct((n_tok, NPICK), jnp.int32),
                   jax.ShapeDtypeStruct((n_tok, SLOTS), _f32)],
        scratch_shapes=[pltpu.VMEM((tt, 8, STAGE_COLS), _f32),
                        pltpu.VMEM((NPICK, tt), jnp.int32), pltpu.VMEM((SLOTS, tt), _f32)],
        compiler_params=_cparams(("arbitrary",)),
        name="peer_up_route",
    )(rows_flat, jnp.arange(N_OFFS, dtype=jnp.int32), x8, tab, diag, fold, gsum, qp, sk, flat)


def _peer_down_kernel(row_ref, off_ref, h_ref, gate_ref, tab_ref, ex_ref, diag_ref, y_ref, wexp_ref):
    offs = [off_ref[k] for k in range(N_OFFS)]
    w = _gelu_tanh(h_ref[...]) * gate_ref[...]
    hi, lo = _split_hi_lo(w)
    wexp_ref[...] = _dot(hi, ex_ref[...]) + _dot(lo, ex_ref[...])

    def token(t, _):
        a = jnp.broadcast_to(wexp_ref[pl.ds(t, 1), :], diag_ref.shape) * diag_ref[...]
        ahi = a.astype(_bf16).astype(_f32)
        a2 = jnp.concatenate([ahi, a - ahi], axis=0).astype(_bf16)
        acc = jnp.zeros((2 * 8, LANES), _f32)
        tok_rows = row_ref.at[pl.ds(t * NPICK, NPICK)]
        for c in range(PEER_HEADS):
            rows = _gather_head(tab_ref, tok_rows, c, offs)
            acc = acc + _dot(a2[:, c * CHUNK_ROWS:(c + 1) * CHUNK_ROWS], rows)
        y_ref[t] = acc[:8] + acc[8:]
        return 0

    lax.fori_loop(0, PEER_TT, token, 0, unroll=PEER_DOWN_UNROLL)


def _peer_down(rows_flat, h2, gate2, tab, expand, diag):
    t = h2.shape[0]
    tt = PEER_TT
    full = lambda a: pl.BlockSpec(a.shape, lambda i: (0,) * a.ndim)
    return pl.pallas_call(
        _peer_down_kernel,
        grid=(t // tt,),
        in_specs=[pl.BlockSpec((tt * NPICK,), lambda i: (i,), memory_space=pltpu.SMEM),
                  pl.BlockSpec((N_OFFS,), lambda i: (0,), memory_space=pltpu.SMEM),
                  pl.BlockSpec((tt, SLOTS), lambda i: (i, 0)),
                  pl.BlockSpec((tt, SLOTS), lambda i: (i, 0)),
                  pl.BlockSpec(tab.shape, lambda i: (0, 0), pipeline_mode=pl.Buffered(1)),
                  full(expand), full(diag)],
        out_specs=pl.BlockSpec((tt, 8, LANES), lambda i: (i, 0, 0)),
        out_shape=jax.ShapeDtypeStruct((t, 8, LANES), _f32),
        scratch_shapes=[pltpu.VMEM((tt, STAGE_COLS), _f32)],
        compiler_params=_cparams(("arbitrary",)),
        name="peer_down",
    )(rows_flat, jnp.arange(N_OFFS, dtype=jnp.int32), h2, gate2, tab, expand, diag)


def _peer_constants():
    col = np.arange(STAGE_COLS)
    p, r = col // PAIR_ROWS, col % PAIR_ROWS
    slot = (p // PEER_TOPK) * 2 * PEER_TOPK + (r % 2) * PEER_TOPK + p % PEER_TOPK
    diag = (r[None, :] // 2 == np.arange(8)[:, None]).astype(np.float32)
    fold = (slot[:, None] == np.arange(SLOTS)[None, :]).astype(np.float32)
    gsum = lambda tt: jnp.asarray(np.arange(tt * 8)[None, :] // 8 == np.arange(tt)[:, None], _bf16)
    return (jnp.asarray(diag, _f32), jnp.asarray(fold, _bf16), jnp.asarray(fold.T, _bf16),
            gsum(PEER_TT), gsum(FUSE_TT))


FIN_TM = 512


def _final_norm_kernel(x_ref, y_ref, g_ref, o_ref):
    z = x_ref[...] + y_ref[...]
    o_ref[...] = z * lax.rsqrt(jnp.mean(z * z, axis=-1, keepdims=True) + EPS) * g_ref[...]


def _final_norm(x1, y, g):
    t = x1.shape[0]
    row = pl.BlockSpec((FIN_TM, D_MODEL), lambda i: (i, 0))
    return pl.pallas_call(
        _final_norm_kernel,
        grid=(t // FIN_TM,),
        in_specs=[row, row, pl.BlockSpec((1, D_MODEL), lambda i: (0, 0))],
        out_specs=row,
        out_shape=jax.ShapeDtypeStruct((t, D_MODEL), _f32),
        compiler_params=_cparams(("parallel",)),
        name="final_norm",
    )(x1, y, g.reshape(1, D_MODEL))


def _cmp_to_sel_counts(n_rows, n_sel):
    r = SEL_LEN // CMP_STRIDE
    c = CMP_LEN // CMP_STRIDE
    off = np.arange(n_rows)[:, None] - r * np.arange(n_sel)[None, :] + (c - 1)
    counts = np.zeros((n_rows, LANES), np.float32)
    for n in range(c):
        counts[:, :n_sel] += ((off - n >= 0) & (off - n < r)).astype(np.float32)
    counts[n_rows - 1] = 0.0
    return counts


def _attention(x2d, b, s, w_in, norm_mix_g, rel_bias, lam_rows, subln_g,
               cmp_pos_k, cmp_pos_v, cmp_w1_k, cmp_w2_k, cmp_w1_v, cmp_w2_v):
    dq, dv, nq, kcvc, vs2, vw2, ng, dkT, ksT, kwT = _in_proj(x2d, norm_mix_g, *_arrange_w_in(w_in))
    to3 = lambda a: a.reshape(b, s, a.shape[-1])

    bt_diff = _near_tiles(_bias_tiles(rel_bias[:, :DIFF_HEADS], DIFF_TQ, DIFF_TQ, 2))
    o_diff = _diff_attn(to3(dq), dkT, to3(dv), bt_diff, lam_rows, subln_g)

    n = s // CMP_STRIDE
    a = kcvc.reshape(b, n, CMP_STRIDE, 2, NSA_KV_HEADS, NSA_HD)
    a = jnp.transpose(a, (0, 3, 4, 1, 2, 5)).reshape(b, 2, NSA_KV_HEADS, n, CMP_STRIDE * NSA_HD)
    pos = jnp.stack([cmp_pos_k, cmp_pos_v]).reshape(2, 2, 1, CMP_STRIDE * NSA_HD)
    w1 = jnp.stack([cmp_w1_k, cmp_w1_v]).astype(_bf16)
    w2 = jnp.stack([cmp_w2_k, cmp_w2_v])
    w2dup = jnp.concatenate([w2, w2], axis=-1).astype(_bf16)
    cmp_kv = _compress(a, pos, w1, w2dup)

    n_sel = s // SEL_LEN
    counts = jnp.asarray(_cmp_to_sel_counts(n, n_sel), _bf16)
    nq3 = to3(nq)
    o_cmp, sel = _cmp_attn(nq3, cmp_kv[:, 0], cmp_kv[:, 1], counts, n_sel)

    nsa_bias = rel_bias[:, DIFF_HEADS:]
    tq = NSA_TQ
    def group_tiles(n_tiles, window=None):
        bt = _bias_tiles(nsa_bias, tq, tq, n_tiles, window=window)
        bt = bt.reshape(NSA_KV_HEADS, NSA_GROUP, n_tiles, tq, tq)[:, np.asarray(GROUP_ORDER)]
        return jnp.transpose(bt, (0, 2, 1, 3, 4)).reshape(NSA_KV_HEADS, n_tiles, NSA_GROUP * tq, tq)

    expand = jnp.asarray(np.arange(s)[None, :] // SEL_LEN == np.arange(LANES)[:, None], _bf16)
    o_sel = _sel_attn(nq3, ksT, to3(vs2), sel, expand, _near_tiles(group_tiles(2)))

    bt_win = jnp.transpose(group_tiles(WIN_TILES, window=WINDOW), (0, 2, 1, 3))
    bt_win = bt_win.reshape(NSA_KV_HEADS, NSA_GROUP * tq, WIN_TILES * tq)
    o_win = _win_attn(nq3, kwT, to3(vw2), bt_win)

    flat = lambda a3: a3.reshape(b * s, a3.shape[-1])
    return flat(o_diff), flat(o_cmp), flat(o_sel), flat(o_win), ng


def _gate_expand():
    ge = np.zeros((3, LANES, NSA_WIDTH), np.float32)
    for head in range(NSA_HEADS):
        for br in range(3):
            ge[br, 3 * head + br, head * NSA_HD:(head + 1) * NSA_HD] = 1.0
    return jnp.asarray(ge, _bf16)


PEER_PARTS = 16


def _peer(x1, xn, qp, peer_sub_keys, peer_u, peer_v):
    t = x1.shape[0]
    sk = peer_sub_keys.reshape(2 * PEER_HEADS, PEER_NKEYS, PEER_DKEY // 2).astype(_bf16)
    diag, fold, expand, gsum, gsum_fuse = _peer_constants()
    x8 = xn.reshape(t, 8, LANES)
    tab_u = _pair_table(peer_u)
    n = t // PEER_PARTS
    rows, gate = _peer_route(qp, sk, n)
    rows_all, gate_all, h_all = [rows], [gate], []
    for k in range(PEER_PARTS - 1):
        h, rows, gate = _peer_up_route(rows.reshape(n * NPICK), x8, tab_u, diag, fold, gsum_fuse, qp, sk,
                                       n, k * n, (k + 1) * n)
        h_all.append(h)
        rows_all.append(rows)
        gate_all.append(gate)
    h_all.append(_peer_up(rows.reshape(n * NPICK), x8, tab_u, diag, fold, gsum, n, (PEER_PARTS - 1) * n))
    rows_flat = jnp.concatenate(rows_all, axis=0).reshape(t * NPICK)
    y = _peer_down(rows_flat, jnp.concatenate(h_all, axis=0), jnp.concatenate(gate_all, axis=0),
                   _pair_table(peer_v), expand, diag)
    return y.reshape(t, D_MODEL)


def kernel(x, w_in, w_out, norm_mix_g, norm_ffn_g, final_norm_g, rel_bias, diff_lq1, diff_lk1, diff_lq2, diff_lk2, diff_subln_g, cmp_pos_k, cmp_pos_v, cmp_w1_k, cmp_w2_k, cmp_w1_v, cmp_w2_v, peer_w_q, peer_sub_keys, peer_u, peer_v):
    b, s, d = x.shape
    x2d = x.reshape(b * s, d)
    lam_rows = jnp.zeros((8, LANES), _f32).at[0:4, :DIFF_HD].set(
        jnp.stack([diff_lq1[0], diff_lk1[0], diff_lq2[0], diff_lk2[0]]))
    o_diff, o_cmp, o_sel, o_win, ng = _attention(
        x2d, b, s, w_in[0], norm_mix_g[0], rel_bias, lam_rows, diff_subln_g[0],
        cmp_pos_k[0], cmp_pos_v[0], cmp_w1_k[0], cmp_w2_k[0], cmp_w1_v[0], cmp_w2_v[0])
    x1, xn, qp = _mix_out(x2d, o_diff, o_cmp, o_sel, o_win, ng, _gate_expand(),
                          w_out[0].astype(_bf16), norm_ffn_g[0], peer_w_q[0].astype(_bf16))
    y = _peer(x1, xn, qp, peer_sub_keys[0], peer_u[0], peer_v[0])
    return _final_norm(x1, y, final_norm_g).reshape(b, s, d)
```

```python
import functools
import math

import numpy as np
import jax
import jax.numpy as jnp
from jax import lax
from jax.experimental import pallas as pl
from jax.experimental.pallas import tpu as pltpu

D_MODEL = 1024
DIFF_HEADS = 4
DIFF_HD = 64
DIFF_VD = 2 * DIFF_HD
DIFF_WIDTH = DIFF_HEADS * DIFF_VD
NSA_HEADS = 8
NSA_KV_HEADS = 2
NSA_GROUP = NSA_HEADS // NSA_KV_HEADS
NSA_HD = 64
NSA_WIDTH = NSA_HEADS * NSA_HD
CMP_LEN = 32
CMP_STRIDE = 16
CMP_HIDDEN = 256
SEL_LEN = 64
SEL_TOPN = 8
WINDOW = 512
FORCE_BONUS = 1000.0
NEG_INF = -1e30
N_BUCKETS = 32
MAX_DISTANCE = 128
PEER_HEADS = 8
PEER_NKEYS = 128
PEER_EXPERTS = PEER_NKEYS ** 2
PEER_DKEY = 256
PEER_TOPK = 16
EPS = 1e-6
LAMBDA_INIT = 0.8 - 0.6 * math.exp(-0.3 * 0)
LOG2E = math.log2(math.e)

KV_COLS = NSA_KV_HEADS * NSA_HD
SPLIT_SIZES = (DIFF_HEADS * 2 * DIFF_HD, DIFF_HEADS * 2 * DIFF_HD, DIFF_WIDTH, NSA_WIDTH,
               KV_COLS, KV_COLS, KV_COLS, KV_COLS, KV_COLS, KV_COLS, 3 * NSA_HEADS)
SPLIT_OFF = tuple(int(v) for v in np.concatenate([[0], np.cumsum(SPLIT_SIZES)]))

LANES = 128
VMEM_LIMIT = 56 * 1024 * 1024
HALF_EXPERTS = PEER_EXPERTS // 2
PAIR_ROWS = 16
PAIR_WORD_ROWS = PAIR_ROWS // 2

_f32 = jnp.float32
_bf16 = jnp.bfloat16


def _cparams(sem):
    return pltpu.CompilerParams(dimension_semantics=sem, vmem_limit_bytes=VMEM_LIMIT)


def _dot(a, b):
    return jnp.dot(a, b, preferred_element_type=_f32)


def _dot_nt(a, b):
    return lax.dot_general(a, b, (((1,), (1,)), ((), ())), preferred_element_type=_f32)


def _split_hi_lo(x):
    hi = x.astype(_bf16)
    lo = (x - hi.astype(_f32)).astype(_bf16)
    return hi, lo


def _gelu_tanh(x):
    c = math.sqrt(2.0 / math.pi)
    return 0.5 * x * (1.0 + jnp.tanh(c * (x + 0.044715 * (x * x * x))))


IN_TM = 512
_IN_GROUPS = (("dq", 512, _bf16), ("dv", 512, _bf16), ("nq", 512, _bf16), ("kcvc", 256, _f32),
              ("vs2", 256, _bf16), ("vw2", 256, _bf16), ("ng", 128, _f32))
_IN_KEYS_T = (("dkT", 512), ("ksT", 256), ("kwT", 256))


def _in_proj_kernel(x_ref, g_ref, w_ref, wt_ref, *out_refs):
    x = x_ref[...]
    y = x * lax.rsqrt(jnp.mean(x * x, axis=-1, keepdims=True) + EPS)
    h = (y * g_ref[...]).astype(_bf16)
    off = 0
    for (_, width, dt), o_ref in zip(_IN_GROUPS, out_refs):
        o_ref[...] = _dot(h, w_ref[:, off:off + width]).astype(dt)
        off += width
    off = 0
    for (_, width), o_ref in zip(_IN_KEYS_T, out_refs[len(_IN_GROUPS):]):
        o_ref[...] = _dot_nt(wt_ref[off:off + width, :], h).astype(o_ref.dtype)
        off += width


def _arrange_w_in(w_in):
    o = SPLIT_OFF
    sl = lambda i: w_in[:, o[i]:o[i + 1]]
    dup = lambda w: jnp.concatenate([w[:, :64], w[:, :64], w[:, 64:], w[:, 64:]], axis=1)
    ng = jnp.pad(sl(10), ((0, 0), (0, LANES - 3 * NSA_HEADS)))
    cols = [sl(0) * (DIFF_HD ** -0.5 * LOG2E), sl(2), sl(3) * (NSA_HD ** -0.5 * LOG2E),
            sl(4), sl(5), dup(sl(7)), dup(sl(9)), ng]
    keys = [sl(1), dup(sl(6)), dup(sl(8))]
    return (jnp.concatenate(cols, axis=1).astype(_bf16),
            jnp.concatenate(keys, axis=1).T.astype(_bf16))


def _in_proj(x2d, g, w_arr, wt_arr):
    t = x2d.shape[0]
    out_shape = ([jax.ShapeDtypeStruct((t, wd), dt) for _, wd, dt in _IN_GROUPS]
                 + [jax.ShapeDtypeStruct((wd, t), _bf16) for _, wd in _IN_KEYS_T])
    out_specs = ([pl.BlockSpec((IN_TM, wd), lambda i: (i, 0)) for _, wd, _ in _IN_GROUPS]
                 + [pl.BlockSpec((wd, IN_TM), lambda i: (0, i)) for _, wd in _IN_KEYS_T])
    return pl.pallas_call(
        _in_proj_kernel,
        grid=(t // IN_TM,),
        in_specs=[pl.BlockSpec((IN_TM, D_MODEL), lambda i: (i, 0)),
                  pl.BlockSpec((1, D_MODEL), lambda i: (0, 0)),
                  pl.BlockSpec(w_arr.shape, lambda i: (0, 0)),
                  pl.BlockSpec(wt_arr.shape, lambda i: (0, 0))],
        out_specs=out_specs,
        out_shape=out_shape,
        compiler_params=_cparams(("parallel",)),
        name="in_proj",
    )(x2d, g.reshape(1, D_MODEL), w_arr, wt_arr)


def _bucket_table(n):
    rel = np.arange(n)
    max_exact = N_BUCKETS // 2
    nf = np.maximum(rel, 1).astype(np.float32)
    large = max_exact + (np.log(nf / np.float32(max_exact)) / np.float32(math.log(MAX_DISTANCE / max_exact))
                         * np.float32(N_BUCKETS - max_exact)).astype(np.int32)
    large = np.minimum(large, N_BUCKETS - 1)
    return np.where(rel < max_exact, rel, large).astype(np.int32)


def _bias_tiles(rel_bias_heads, tq, tk, n_tiles, window=None):
    tab = (rel_bias_heads.astype(_f32) - rel_bias_heads[N_BUCKETS - 1][None, :].astype(_f32)) * LOG2E
    period = tq + tk
    m = np.arange(period)
    off = np.where(m < tk, -m, period - m)
    rel_w = np.arange(n_tiles)[:, None] * tq + off[None, :]
    bucket = _bucket_table(n_tiles * tq + period)[np.clip(rel_w, 0, None)]
    w = jnp.transpose(tab[bucket], (2, 0, 1))
    tiles = jnp.tile(w, (1, 1, tq))[..., :tq * (period - 1)]
    tiles = tiles.reshape(w.shape[0], n_tiles, tq, period - 1)[..., :tk]
    rel = (np.arange(n_tiles)[:, None, None] * tq + np.arange(tq)[None, :, None]
           - np.arange(tk)[None, None, :])
    ok = rel >= 0
    if window is not None:
        ok &= rel < window
    return jnp.where(jnp.asarray(ok)[None], tiles, NEG_INF)


def _near_tiles(bt):
    first = jnp.concatenate([bt[..., 0, :, :], jnp.full_like(bt[..., 0, :, :], NEG_INF)], axis=-1)
    later = jnp.concatenate([bt[..., 1, :, :], bt[..., 0, :, :]], axis=-1)
    return jnp.stack([first, later], axis=-3)


def _attend_first(s, v_aug):
    m = jnp.max(s, axis=-1, keepdims=True)
    return m, _dot(jnp.exp2(s - m).astype(_bf16), v_aug)


def _attend_step(carry, s, v_aug):
    m, acc = carry
    m_new = jnp.maximum(m, jnp.max(s, axis=-1, keepdims=True))
    acc = jnp.exp2(m - m_new) * acc + _dot(jnp.exp2(s - m_new).astype(_bf16), v_aug)
    return m_new, acc


def _lane_half_mask(shape):
    return lax.broadcasted_iota(jnp.int32, shape, len(shape) - 1) < (LANES // 2)


DIFF_TQ = 256


def _diff_attn_kernel(q_ref, k_ref, v_ref, bt_ref, lam_ref, g_ref, o_ref):
    tq = DIFF_TQ
    qb = pl.program_id(2)
    q = q_ref[...]
    lo = _lane_half_mask(q.shape)
    zero = jnp.zeros_like(q)
    q2 = jnp.concatenate([jnp.where(lo, q, zero), jnp.where(lo, zero, q)], axis=0)

    def kv(start, n):
        start = pl.multiple_of(start, tq)
        v = v_ref[pl.ds(start, n), :]
        return k_ref[:, pl.ds(start, n)], jnp.concatenate([v, jnp.ones_like(v)], axis=1)

    def both(b):
        return jnp.concatenate([b, b], axis=0)

    k, v = kv(jnp.maximum(qb - 1, 0) * tq, 2 * tq)
    carry = _attend_first(_dot(q2, k) + both(bt_ref[jnp.minimum(qb, 1)]), v)

    n_far = jnp.maximum(qb - 1, 0)

    def odd(c):
        k, v = kv((n_far - 1) * tq, tq)
        return _attend_step(c, _dot(q2, k), v)

    carry = lax.cond(n_far % 2 == 1, odd, lambda c: c, carry)

    def pair(i, c):
        k, v = kv(i * 2 * tq, 2 * tq)
        return _attend_step(c, _dot(q2, k), v)

    m, acc = lax.fori_loop(0, n_far // 2, pair, carry)
    o2 = acc[:, :DIFF_VD] / acc[:, DIFF_VD:]
    lv = lam_ref[...]
    lam = (jnp.exp(jnp.sum(lv[0:1] * lv[1:2], axis=-1, keepdims=True))
           - jnp.exp(jnp.sum(lv[2:3] * lv[3:4], axis=-1, keepdims=True)) + LAMBDA_INIT)
    o = o2[:tq] - lam * o2[tq:]
    y = o * lax.rsqrt(jnp.mean(o * o, axis=-1, keepdims=True) + EPS)
    o_ref[...] = (y * g_ref[...] * (1.0 - LAMBDA_INIT)).astype(o_ref.dtype)


def _diff_attn(dq, dkT, dv, bt, lam_rows, subln_g):
    b, s, _ = dq.shape
    tq = DIFF_TQ
    return pl.pallas_call(
        _diff_attn_kernel,
        grid=(b, DIFF_HEADS, s // tq),
        in_specs=[pl.BlockSpec((None, tq, LANES), lambda bi, h, qb: (bi, qb, h)),
                  pl.BlockSpec((LANES, s), lambda bi, h, qb: (h, bi)),
                  pl.BlockSpec((None, s, LANES), lambda bi, h, qb: (bi, 0, h)),
                  pl.BlockSpec((None, 2, tq, 2 * tq), lambda bi, h, qb: (h, 0, 0, 0)),
                  pl.BlockSpec((8, LANES), lambda bi, h, qb: (0, 0)),
                  pl.BlockSpec((1, LANES), lambda bi, h, qb: (0, 0))],
        out_specs=pl.BlockSpec((None, tq, LANES), lambda bi, h, qb: (bi, qb, h)),
        out_shape=jax.ShapeDtypeStruct((b, s, DIFF_WIDTH), _bf16),
        compiler_params=_cparams(("parallel", "parallel", "arbitrary")),
        name="diff_attn",
    )(dq, dkT, dv, bt, lam_rows, subln_g.reshape(1, LANES))


def _compress_kernel(a_ref, pos_ref, w1_ref, w2_ref, o_ref):
    n = o_ref.shape[-2]
    z = []
    for half in range(2):
        acc = jnp.zeros((n, 2 * CMP_HIDDEN), _f32)
        for l in range(CMP_STRIDE):
            rows = a_ref[pl.ds(l, n, stride=CMP_STRIDE), :]
            acc = acc + _dot((rows + pos_ref[half, l]).astype(_bf16), w1_ref[half, l])
        z.append(acc)
    hid = z[0] + pltpu.roll(z[1], n - 1, 0)
    for g in range(NSA_KV_HEADS):
        hg = hid[:, g * CMP_HIDDEN:(g + 1) * CMP_HIDDEN]
        o_ref[g] = _dot(_gelu_tanh(hg).astype(_bf16), w2_ref[...]).astype(o_ref.dtype)


def _compress(kcvc, pos, w1, w2dup):
    b, s, _ = kcvc.shape
    n = s // CMP_STRIDE
    return pl.pallas_call(
        _compress_kernel,
        grid=(b, 2),
        in_specs=[pl.BlockSpec((None, s, LANES), lambda bi, kv: (bi, 0, kv)),
                  pl.BlockSpec((None,) + pos.shape[1:], lambda bi, kv: (kv, 0, 0, 0, 0)),
                  pl.BlockSpec((None,) + w1.shape[1:], lambda bi, kv: (kv, 0, 0, 0, 0)),
                  pl.BlockSpec((None, CMP_HIDDEN, LANES), lambda bi, kv: (kv, 0, 0))],
        out_specs=pl.BlockSpec((None, None, NSA_KV_HEADS, n, LANES), lambda bi, kv: (bi, kv, 0, 0, 0)),
        out_shape=jax.ShapeDtypeStruct((b, 2, NSA_KV_HEADS, n, LANES), _bf16),
        compiler_params=_cparams(("parallel", "parallel")),
        name="compress",
    )(kcvc, pos, w1, w2dup)


NSA_TQ = 256


GROUP_ORDER = (0, 2, 1, 3)


def _stack_group_queries(q):
    rows = []
    for j in GROUP_ORDER:
        blk = q[:, (j // 2) * LANES:(j // 2 + 1) * LANES]
        lo = _lane_half_mask(blk.shape)
        keep = lo if j % 2 == 0 else jnp.logical_not(lo)
        rows.append(jnp.where(keep, blk, jnp.zeros_like(blk)))
    return jnp.concatenate(rows, axis=0)


def _unstack_group_outputs(o4, tq):
    lo = _lane_half_mask((tq, LANES))
    half = NSA_GROUP // 2
    pairs = [jnp.where(lo, o4[i * tq:(i + 1) * tq], o4[(half + i) * tq:(half + i + 1) * tq])
             for i in range(half)]
    return jnp.concatenate(pairs, axis=1)


def _group_values(v):
    lo = _lane_half_mask(v.shape)
    one = jnp.ones_like(v)
    return jnp.where(lo, v, one), jnp.where(lo, one, v)


def _group_pv(p, v_pair):
    n = p.shape[0] // 2
    p = p.astype(_bf16)
    return jnp.concatenate([_dot(p[:n], v_pair[0]), _dot(p[n:], v_pair[1])], axis=0)


def _group_attend_first(s, v_pair):
    m = jnp.max(s, axis=-1, keepdims=True)
    return m, _group_pv(jnp.exp2(s - m), v_pair)


def _group_attend_step(carry, s, v_pair):
    m, acc = carry
    m_new = jnp.maximum(m, jnp.max(s, axis=-1, keepdims=True))
    return m_new, jnp.exp2(m - m_new) * acc + _group_pv(jnp.exp2(s - m_new), v_pair)


def _group_normalise(acc):
    return acc / pltpu.roll(acc, LANES // 2, 1)


def _cmp_attn_kernel(q_ref, kc_ref, vc_ref, cs_ref, o_ref, sel_ref, *, n_sel):
    tq = NSA_TQ
    qb = pl.program_id(2)
    q4 = _stack_group_queries(q_ref[...])
    s = _dot_nt(q4, kc_ref[...])
    row = lax.broadcasted_iota(jnp.int32, s.shape, 0)
    pos4 = qb * tq + (row & (tq - 1))
    c = lax.broadcasted_iota(jnp.int32, s.shape, 1)
    ok = (c * CMP_STRIDE + (CMP_LEN - 1)) <= pos4
    s = jnp.where(ok, s, NEG_INF)
    m = jnp.max(s, axis=-1, keepdims=True)
    e = jnp.where(ok, jnp.exp2(s - m), 0.0)
    l = jnp.sum(e, axis=-1, keepdims=True)
    p = e / jnp.where(l > 0.0, l, 1.0)
    o_ref[...] = _unstack_group_outputs(_dot(p.astype(_bf16), vc_ref[...]), tq).astype(o_ref.dtype)

    psum = p[0:tq] + p[tq:2 * tq] + p[2 * tq:3 * tq] + p[3 * tq:4 * tq]
    hi, lo = _split_hi_lo(psum)
    imp = _dot(hi, cs_ref[...]) + _dot(lo, cs_ref[...])
    blk = lax.broadcasted_iota(jnp.int32, imp.shape, 1)
    pos = qb * tq + lax.broadcasted_iota(jnp.int32, imp.shape, 0)
    cur = pos // SEL_LEN
    forced = (blk == 0) | (blk == cur) | (blk == cur - 1)
    score = jnp.where(blk <= cur, imp + jnp.where(forced, FORCE_BONUS, 0.0), NEG_INF)
    st = score.T[:n_sel]
    blk_t = lax.broadcasted_iota(jnp.int32, st.shape, 0)
    rank = jnp.zeros(st.shape, jnp.int32)
    for i in range(n_sel):
        si = st[i:i + 1]
        beats = (si > st) | ((si == st) & (i < blk_t))
        rank = rank + beats.astype(jnp.int32)
    sel_t = ((rank < SEL_TOPN) & (st > 0.5 * NEG_INF)).astype(_f32)
    sel = jnp.concatenate([sel_t, jnp.zeros((LANES - n_sel, tq), _f32)], axis=0).T
    sel_ref[...] = sel.astype(sel_ref.dtype)


def _cmp_attn(nq, kcmp, vcmp, cmp_sel, n_sel):
    b, s, _ = nq.shape
    tq = NSA_TQ
    n = kcmp.shape[-2]
    gw = NSA_GROUP * NSA_HD
    return pl.pallas_call(
        functools.partial(_cmp_attn_kernel, n_sel=n_sel),
        grid=(b, NSA_KV_HEADS, s // tq),
        in_specs=[pl.BlockSpec((None, tq, gw), lambda bi, g, qb: (bi, qb, g)),
                  pl.BlockSpec((None, None, n, LANES), lambda bi, g, qb: (bi, g, 0, 0)),
                  pl.BlockSpec((None, None, n, LANES), lambda bi, g, qb: (bi, g, 0, 0)),
                  pl.BlockSpec((n, LANES), lambda bi, g, qb: (0, 0))],
        out_specs=[pl.BlockSpec((None, tq, gw), lambda bi, g, qb: (bi, qb, g)),
                   pl.BlockSpec((None, None, tq, LANES), lambda bi, g, qb: (bi, g, qb, 0))],
        out_shape=[jax.ShapeDtypeStruct((b, s, NSA_WIDTH), _bf16),
                   jax.ShapeDtypeStruct((b, NSA_KV_HEADS, s, LANES), _bf16)],
        compiler_params=_cparams(("parallel", "parallel", "arbitrary")),
        name="cmp_attn",
    )(nq, kcmp, vcmp, cmp_sel)


SEL_CHUNK = 4 * NSA_TQ


def _sel_attn_kernel(q_ref, k_ref, v_ref, sel_ref, ex_ref, bt_ref, o_ref):
    tq = NSA_TQ
    qb = pl.program_id(2)
    q4 = _stack_group_queries(q_ref[...])
    sel = sel_ref[...]

    def scores(pen_rows, start, n):
        start = pl.multiple_of(start, tq)
        kx = jnp.concatenate([k_ref[:, pl.ds(start, n)], ex_ref[:, pl.ds(start, n)]], axis=0)
        qx = jnp.concatenate([q4, jnp.concatenate([pen_rows] * NSA_GROUP, axis=0)], axis=1)
        return _dot(qx, kx), _group_values(v_ref[pl.ds(start, n), :])

    def penalty(sel_rows):
        return ((sel_rows.astype(_f32) - 1.0) * (-NEG_INF)).astype(_bf16)

    s, v = scores(penalty(sel), jnp.maximum(qb - 1, 0) * tq, 2 * tq)
    carry = _group_attend_first(s + bt_ref[jnp.minimum(qb, 1)], v)

    far_len = jnp.maximum(qb - 1, 0) * tq
    blk = lax.broadcasted_iota(jnp.int32, sel.shape, 1)
    pen_far = penalty(jnp.where(blk * SEL_LEN < far_len, sel, jnp.zeros_like(sel)))

    def far(i, c):
        s, v = scores(pen_far, i * SEL_CHUNK, SEL_CHUNK)
        return _group_attend_step(c, s, v)

    m, acc = lax.fori_loop(0, (far_len + SEL_CHUNK - 1) // SEL_CHUNK, far, carry)
    o_ref[...] = _unstack_group_outputs(_group_normalise(acc), tq).astype(o_ref.dtype)


def _sel_attn(nq, ks2, vs2, sel, expand, bt):
    b, s, _ = nq.shape
    tq = NSA_TQ
    gw = NSA_GROUP * NSA_HD
    return pl.pallas_call(
        _sel_attn_kernel,
        grid=(b, NSA_KV_HEADS, s // tq),
        in_specs=[pl.BlockSpec((None, tq, gw), lambda bi, g, qb: (bi, qb, g)),
                  pl.BlockSpec((LANES, s), lambda bi, g, qb: (g, bi)),
                  pl.BlockSpec((None, s, LANES), lambda bi, g, qb: (bi, 0, g)),
                  pl.BlockSpec((None, None, tq, LANES), lambda bi, g, qb: (bi, g, qb, 0)),
                  pl.BlockSpec((LANES, s), lambda bi, g, qb: (0, 0)),
                  pl.BlockSpec((None, 2, NSA_GROUP * tq, 2 * tq), lambda bi, g, qb: (g, 0, 0, 0))],
        out_specs=pl.BlockSpec((None, tq, gw), lambda bi, g, qb: (bi, qb, g)),
        out_shape=jax.ShapeDtypeStruct((b, s, NSA_WIDTH), _bf16),
        compiler_params=_cparams(("parallel", "parallel", "arbitrary")),
        name="sel_attn",
    )(nq, ks2, vs2, sel, expand, bt)


WIN_TILES = WINDOW // NSA_TQ + 1


def _win_attn_kernel(q_ref, k_ref, v_ref, bt_ref, o_ref):
    tq = NSA_TQ
    qb = pl.program_id(2)
    q4 = _stack_group_queries(q_ref[...])
    s_blocks, va, vb = [], [], []
    for d in range(WIN_TILES):
        start = pl.multiple_of(jnp.maximum(qb - d, 0) * tq, tq)
        s = _dot(q4, k_ref[:, pl.ds(start, tq)])
        s_blocks.append(s if d == 0 else s + jnp.where(qb >= d, 0.0, NEG_INF))
        a, b = _group_values(v_ref[pl.ds(start, tq), :])
        va.append(a)
        vb.append(b)
    s = jnp.concatenate(s_blocks, axis=1) + bt_ref[...]
    m, acc = _group_attend_first(s, (jnp.concatenate(va, axis=0), jnp.concatenate(vb, axis=0)))
    o_ref[...] = _unstack_group_outputs(_group_normalise(acc), tq).astype(o_ref.dtype)


def _win_attn(nq, kw2, vw2, bt):
    b, s, _ = nq.shape
    tq = NSA_TQ
    gw = NSA_GROUP * NSA_HD
    return pl.pallas_call(
        _win_attn_kernel,
        grid=(b, NSA_KV_HEADS, s // tq),
        in_specs=[pl.BlockSpec((None, tq, gw), lambda bi, g, qb: (bi, qb, g)),
                  pl.BlockSpec((LANES, s), lambda bi, g, qb: (g, bi)),
                  pl.BlockSpec((None, s, LANES), lambda bi, g, qb: (bi, 0, g)),
                  pl.BlockSpec((None, NSA_GROUP * tq, WIN_TILES * tq), lambda bi, g, qb: (g, 0, 0))],
        out_specs=pl.BlockSpec((None, tq, gw), lambda bi, g, qb: (bi, qb, g)),
        out_shape=jax.ShapeDtypeStruct((b, s, NSA_WIDTH), _bf16),
        compiler_params=_cparams(("parallel", "parallel", "arbitrary")),
        name="win_attn",
    )(nq, kw2, vw2, bt)


MIX_TM = 512


def _mix_out_kernel(x_ref, od_ref, oc_ref, os_ref, ow_ref, ng_ref, ge_ref, wo_ref, g_ref, wq_ref,
                    x1_ref, xn_ref, qp_ref):
    sig = jax.nn.sigmoid(ng_ref[...])
    hi, lo = _split_hi_lo(sig)
    o_nsa = jnp.zeros(oc_ref.shape, _f32)
    for br, o_ref in enumerate((oc_ref, os_ref, ow_ref)):
        gate = _dot(hi, ge_ref[br]) + _dot(lo, ge_ref[br])
        o_nsa = o_nsa + gate * o_ref[...].astype(_f32)
    y = _dot(od_ref[...], wo_ref[:DIFF_WIDTH, :]) + _dot(o_nsa.astype(_bf16), wo_ref[DIFF_WIDTH:, :])
    x1 = x_ref[...] + y
    x1_ref[...] = x1
    xn = x1 * lax.rsqrt(jnp.mean(x1 * x1, axis=-1, keepdims=True) + EPS) * g_ref[...]
    xn = xn.astype(_bf16)
    xn_ref[...] = xn
    qp_ref[...] = _dot(xn, wq_ref[...]).astype(qp_ref.dtype)


def _mix_out(x2d, o_diff, o_cmp, o_sel, o_win, ng, gate_expand, w_out, g_ffn, w_q):
    t = x2d.shape[0]
    tm = MIX_TM
    nq = w_q.shape[1]
    row = lambda w: pl.BlockSpec((tm, w), lambda i: (i, 0))
    full = lambda a: pl.BlockSpec(a.shape, lambda i: (0,) * a.ndim)
    g2 = g_ffn.reshape(1, D_MODEL)
    return pl.pallas_call(
        _mix_out_kernel,
        grid=(t // tm,),
        in_specs=[row(D_MODEL), row(DIFF_WIDTH), row(NSA_WIDTH), row(NSA_WIDTH), row(NSA_WIDTH),
                  row(LANES), full(gate_expand), full(w_out), full(g2), full(w_q)],
        out_specs=[row(D_MODEL), row(D_MODEL), row(nq)],
        out_shape=[jax.ShapeDtypeStruct((t, D_MODEL), _f32),
                   jax.ShapeDtypeStruct((t, D_MODEL), _bf16),
                   jax.ShapeDtypeStruct((t, nq), _bf16)],
        compiler_params=_cparams(("parallel",)),
        name="mix_out",
    )(x2d, o_diff, o_cmp, o_sel, o_win, ng, gate_expand, w_out, g2, w_q)


ROUTE_TM = 256
SLOTS = 2 * PEER_HEADS * PEER_TOPK


def _extract_step(st):
    sc, rows, n = st[0], st[1], st[2]
    m = jnp.max(sc, axis=0, keepdims=True)
    idx = jnp.min(jnp.where(sc == m, rows, n), axis=0, keepdims=True)
    st[3].append(m)
    st[4].append(idx)
    st[0] = jnp.where(rows == idx, -jnp.inf, sc)


def _route_head_steps(h, q_ref, sk_ref, flat_c, keep_c, rows_k, rowt_ref, gatet_ref):
    half_d = PEER_DKEY // 2
    col = pl.multiple_of(h * PEER_DKEY, PEER_DKEY)
    box = {}

    def start():
        q0 = q_ref[:, pl.ds(col, half_d)]
        q1 = q_ref[:, pl.ds(col + half_d, half_d)]
        box["a"] = [_dot_nt(sk_ref[2 * h], q0), rows_k, PEER_NKEYS, [], []]
        box["b"] = [_dot_nt(sk_ref[2 * h + 1], q1), rows_k, PEER_NKEYS, [], []]

    def first_stage():
        _extract_step(box["a"])
        _extract_step(box["b"])

    def pairs():
        sv0, si0 = jnp.concatenate(box["a"][3], axis=0), jnp.concatenate(box["a"][4], axis=0)
        sv1, si1 = jnp.concatenate(box["b"][3], axis=0), jnp.concatenate(box["b"][4], axis=0)
        slabs, ids = [], []
        for a, b0 in _PAIR_SLABS:
            if a is None:
                slabs.append(sv0[8:16] + sv1[0:1])
                ids.append(si0[8:16] * PEER_NKEYS + si1[0:1])
            else:
                slabs.append(sv0[a:a + 1] + sv1[b0:b0 + 8])
                ids.append(si0[a:a + 1] * PEER_NKEYS + si1[b0:b0 + 8])
        cand = jnp.where(keep_c, jnp.concatenate(slabs, axis=0), -jnp.inf)
        box["cidx"] = jnp.concatenate(ids, axis=0)
        box["c"] = [cand, flat_c, PEER_TOPK * PEER_TOPK, [], []]

    def second_stage():
        _extract_step(box["c"])

    def finish():
        cv, ci = jnp.concatenate(box["c"][3], axis=0), jnp.concatenate(box["c"][4], axis=0)
        e = jnp.concatenate(
            [jnp.sum(jnp.where(flat_c == ci[k:k + 1], box["cidx"], 0), axis=0, keepdims=True)
             for k in range(PEER_TOPK)], axis=0)
        ex = jnp.exp(cv - cv[0:1])
        gate = ex / jnp.sum(ex, axis=0, keepdims=True)
        upper = e >= HALF_EXPERTS
        base = pl.multiple_of(h * PEER_TOPK, PEER_TOPK)
        rowt_ref[pl.ds(base, PEER_TOPK), :] = (e & (HALF_EXPERTS - 1)) * PAIR_WORD_ROWS
        base2 = pl.multiple_of(h * 2 * PEER_TOPK, 2 * PEER_TOPK)
        gatet_ref[pl.ds(base2, PEER_TOPK), :] = jnp.where(upper, 0.0, gate)
        gatet_ref[pl.ds(base2 + PEER_TOPK, PEER_TOPK), :] = jnp.where(upper, gate, 0.0)

    return [start] + [first_stage] * PEER_TOPK + [pairs] + [second_stage] * PEER_TOPK + [finish]


def _route_head(h, q_ref, sk_ref, flat_c, keep_c, rows_k, rowt_ref, gatet_ref):
    for step in _route_head_steps(h, q_ref, sk_ref, flat_c, keep_c, rows_k, rowt_ref, gatet_ref):
        step()


def _route_consts(flat_ref, tm):
    rows_k = lax.broadcasted_iota(jnp.int32, (PEER_NKEYS, tm), 0)
    flat_c = jnp.broadcast_to(flat_ref[...], (flat_ref.shape[0], tm))
    return rows_k, flat_c, flat_c >= 0


def _peer_route_kernel(q_ref, sk_ref, flat_ref, row_ref, gate_ref, rowt_ref, gatet_ref):
    rows_k, flat_c, keep_c = _route_consts(flat_ref, ROUTE_TM)

    def head(h, _):
        _route_head(h, q_ref, sk_ref, flat_c, keep_c, rows_k, rowt_ref, gatet_ref)
        return 0

    lax.fori_loop(0, PEER_HEADS, head, 0)
    row_ref[...] = rowt_ref[...].T
    gate_ref[...] = gatet_ref[...].T


def _pair_slabs():
    slabs, flat = [(0, 0), (0, 8)] + [(a, 0) for a in range(1, 8)] + [(None, 0)], []
    for a, b0 in slabs:
        for i in range(8):
            aa, bb = (8 + i, 0) if a is None else (a, b0 + i)
            flat.append(aa * PEER_TOPK + bb if (aa + 1) * (bb + 1) <= PEER_TOPK else -1)
    return tuple(slabs), np.asarray(flat, np.int32).reshape(-1, 1)


_PAIR_SLABS, _PAIR_FLAT = _pair_slabs()


def _peer_route(qp, sk, n_tok):
    t = n_tok
    tm = ROUTE_TM
    npick = PEER_HEADS * PEER_TOPK
    return pl.pallas_call(
        _peer_route_kernel,
        grid=(t // tm,),
        in_specs=[pl.BlockSpec((tm, qp.shape[1]), lambda i: (i, 0)),
                  pl.BlockSpec(sk.shape, lambda i: (0, 0, 0)),
                  pl.BlockSpec(_PAIR_FLAT.shape, lambda i: (0, 0))],
        out_specs=[pl.BlockSpec((tm, npick), lambda i: (i, 0)),
                   pl.BlockSpec((tm, SLOTS), lambda i: (i, 0))],
        out_shape=[jax.ShapeDtypeStruct((t, npick), jnp.int32),
                   jax.ShapeDtypeStruct((t, SLOTS), _f32)],
        scratch_shapes=[pltpu.VMEM((npick, tm), jnp.int32), pltpu.VMEM((SLOTS, tm), _f32)],
        compiler_params=_cparams(("parallel",)),
        name="peer_route",
    )(qp, sk, jnp.asarray(_PAIR_FLAT))


PEER_TT = 64
N_OFFS = 8
PEER_UNROLL = 8
PEER_DOWN_UNROLL = 16
NPICK = PEER_HEADS * PEER_TOPK
CHUNK_ROWS = PEER_TOPK * PAIR_ROWS
STAGE_COLS = NPICK * PAIR_ROWS


def _pair_table(w):
    e = w.shape[0]
    w4 = w.astype(_bf16).reshape(2, e // 2, PAIR_WORD_ROWS, LANES)
    words = lax.bitcast_convert_type(jnp.transpose(w4, (1, 2, 3, 0)), jnp.int32)
    return words.reshape(e // 2 * PAIR_WORD_ROWS, LANES)


def _gather_head(tab_ref, tok_rows, c, offs):
    tiles = []
    n_off = len(offs)
    for k in range(PEER_TOPK):
        if k % n_off == 0:
            part_rows = tok_rows.at[pl.ds(c * PEER_TOPK + k, n_off)]
        r = pl.multiple_of(part_rows[offs[k % n_off]], PAIR_WORD_ROWS)
        tiles.append(pltpu.bitcast(tab_ref[pl.ds(r, PAIR_WORD_ROWS), :], _bf16))
    return jnp.concatenate(tiles, axis=0)


def _peer_up_kernel(row_ref, off_ref, x_ref, tab_ref, diag_ref, fold_ref, gsum_ref, h_ref, z_ref):
    offs = [off_ref[k] for k in range(N_OFFS)]

    def token(t, _):
        x8 = x_ref[t]
        tok_rows = row_ref.at[pl.ds(t * NPICK, NPICK)]
        for c in range(PEER_HEADS):
            rows = _gather_head(tab_ref, tok_rows, c, offs)
            z_ref[t, :, c * CHUNK_ROWS:(c + 1) * CHUNK_ROWS] = _dot_nt(x8, rows)
        return 0

    lax.fori_loop(0, PEER_TT, token, 0, unroll=PEER_UNROLL)
    zm = (z_ref[...] * diag_ref[...][None]).reshape(PEER_TT * 8, STAGE_COLS)
    part = _dot(zm.astype(_bf16), fold_ref[...])
    hi, lo = _split_hi_lo(part)
    h_ref[...] = _dot(gsum_ref[...], hi) + _dot(gsum_ref[...], lo)


def _peer_up(rows_flat, x8, tab, diag, fold, gsum, n_tok, x_off):
    t = n_tok
    tt = PEER_TT
    xo = x_off // tt
    full = lambda a: pl.BlockSpec(a.shape, lambda i: (0,) * a.ndim)
    return pl.pallas_call(
        _peer_up_kernel,
        grid=(t // tt,),
        in_specs=[pl.BlockSpec((tt * NPICK,), lambda i: (i,), memory_space=pltpu.SMEM),
                  pl.BlockSpec((N_OFFS,), lambda i: (0,), memory_space=pltpu.SMEM),
                  pl.BlockSpec((tt, 8, LANES), lambda i: (i + xo, 0, 0)),
                  pl.BlockSpec(tab.shape, lambda i: (0, 0), pipeline_mode=pl.Buffered(1)),
                  full(diag), full(fold), full(gsum)],
        out_specs=pl.BlockSpec((tt, SLOTS), lambda i: (i, 0)),
        out_shape=jax.ShapeDtypeStruct((t, SLOTS), _f32),
        scratch_shapes=[pltpu.VMEM((tt, 8, STAGE_COLS), _f32)],
        compiler_params=_cparams(("arbitrary",)),
        name="peer_up",
    )(rows_flat, jnp.arange(N_OFFS, dtype=jnp.int32), x8, tab, diag, fold, gsum)


FUSE_TT = 128
FUSE_INNER = FUSE_TT // PEER_HEADS


def _peer_up_route_kernel(row_ref, off_ref, x_ref, tab_ref, diag_ref, fold_ref, gsum_ref, q_ref, sk_ref, flat_ref,
                          h_ref, row2_ref, gate2_ref, z_ref, rowt_ref, gatet_ref):
    offs = [off_ref[k] for k in range(N_OFFS)]
    rows_k, flat_c, keep_c = _route_consts(flat_ref, FUSE_TT)

    def trip(j, _):
        steps = _route_head_steps(j, q_ref, sk_ref, flat_c, keep_c, rows_k, rowt_ref, gatet_ref)
        per = -(-len(steps) // FUSE_INNER)
        for i in range(FUSE_INNER):
            t = j * FUSE_INNER + i
            x8 = x_ref[t]
            tok_rows = row_ref.at[pl.ds(t * NPICK, NPICK)]
            for c in range(PEER_HEADS):
                rows = _gather_head(tab_ref, tok_rows, c, offs)
                z_ref[t, :, c * CHUNK_ROWS:(c + 1) * CHUNK_ROWS] = _dot_nt(x8, rows)
            for step in steps[i * per:(i + 1) * per]:
                step()
        return 0

    lax.fori_loop(0, PEER_HEADS, trip, 0)
    row2_ref[...] = rowt_ref[...].T
    gate2_ref[...] = gatet_ref[...].T
    zm = (z_ref[...] * diag_ref[...][None]).reshape(FUSE_TT * 8, STAGE_COLS)
    part = _dot(zm.astype(_bf16), fold_ref[...])
    hi, lo = _split_hi_lo(part)
    h_ref[...] = _dot(gsum_ref[...], hi) + _dot(gsum_ref[...], lo)


def _peer_up_route(rows_flat, x8, tab, diag, fold, gsum, qp, sk, n_tok, x_off, q_off):
    tt = FUSE_TT
    xo, qo = x_off // tt, q_off // tt
    full = lambda a: pl.BlockSpec(a.shape, lambda i: (0,) * a.ndim)
    flat = jnp.asarray(_PAIR_FLAT)
    return pl.pallas_call(
        _peer_up_route_kernel,
        grid=(n_tok // tt,),
        in_specs=[pl.BlockSpec((tt * NPICK,), lambda i: (i,), memory_space=pltpu.SMEM),
                  pl.BlockSpec((N_OFFS,), lambda i: (0,), memory_space=pltpu.SMEM),
                  pl.BlockSpec((tt, 8, LANES), lambda i: (i + xo, 0, 0)),
                  pl.BlockSpec(tab.shape, lambda i: (0, 0), pipeline_mode=pl.Buffered(1)),
                  full(diag), full(fold), full(gsum),
                  pl.BlockSpec((tt, qp.shape[1]), lambda i: (i + qo, 0)), full(sk), full(flat)],
        out_specs=[pl.BlockSpec((tt, SLOTS), lambda i: (i, 0)),
                   pl.BlockSpec((tt, NPICK), lambda i: (i, 0)),
                   pl.BlockSpec((tt, SLOTS), lambda i: (i, 0))],
        out_shape=[jax.ShapeDtypeStruct((n_tok, SLOTS), _f32),
                   jax.ShapeDtypeStruct((n_tok, NPICK), jnp.int32),
                   jax.ShapeDtypeStruct((n_tok, SLOTS), _f32)],
        scratch_shapes=[pltpu.VMEM((tt, 8, STAGE_COLS), _f32),
                        pltpu.VMEM((NPICK, tt), jnp.int32), pltpu.VMEM((SLOTS, tt), _f32)],
        compiler_params=_cparams(("arbitrary",)),
        name="peer_up_route",
    )(rows_flat, jnp.arange(N_OFFS, dtype=jnp.int32), x8, tab, diag, fold, gsum, qp, sk, flat)


def _peer_down_kernel(row_ref, off_ref, h_ref, gate_ref, tab_ref, ex_ref, diag_ref, y_ref, wexp_ref):
    offs = [off_ref[k] for k in range(N_OFFS)]
    w = _gelu_tanh(h_ref[...]) * gate_ref[...]
    hi, lo = _split_hi_lo(w)
    wexp_ref[...] = _dot(hi, ex_ref[...]) + _dot(lo, ex_ref[...])

    def token(t, _):
        a = jnp.broadcast_to(wexp_ref[pl.ds(t, 1), :], diag_ref.shape) * diag_ref[...]
        ahi = a.astype(_bf16).astype(_f32)
        a2 = jnp.concatenate([ahi, a - ahi], axis=0).astype(_bf16)
        acc = jnp.zeros((2 * 8, LANES), _f32)
        tok_rows = row_ref.at[pl.ds(t * NPICK, NPICK)]
        for c in range(PEER_HEADS):
            rows = _gather_head(tab_ref, tok_rows, c, offs)
            acc = acc + _dot(a2[:, c * CHUNK_ROWS:(c + 1) * CHUNK_ROWS], rows)
        y_ref[t] = acc[:8] + acc[8:]
        return 0

    lax.fori_loop(0, PEER_TT, token, 0, unroll=PEER_DOWN_UNROLL)


def _peer_down(rows_flat, h2, gate2, tab, expand, diag):
    t = h2.shape[0]
    tt = PEER_TT
    full = lambda a: pl.BlockSpec(a.shape, lambda i: (0,) * a.ndim)
    return pl.pallas_call(
        _peer_down_kernel,
        grid=(t // tt,),
        in_specs=[pl.BlockSpec((tt * NPICK,), lambda i: (i,), memory_space=pltpu.SMEM),
                  pl.BlockSpec((N_OFFS,), lambda i: (0,), memory_space=pltpu.SMEM),
                  pl.BlockSpec((tt, SLOTS), lambda i: (i, 0)),
                  pl.BlockSpec((tt, SLOTS), lambda i: (i, 0)),
                  pl.BlockSpec(tab.shape, lambda i: (0, 0), pipeline_mode=pl.Buffered(1)),
                  full(expand), full(diag)],
        out_specs=pl.BlockSpec((tt, 8, LANES), lambda i: (i, 0, 0)),
        out_shape=jax.ShapeDtypeStruct((t, 8, LANES), _f32),
        scratch_shapes=[pltpu.VMEM((tt, STAGE_COLS), _f32)],
        compiler_params=_cparams(("arbitrary",)),
        name="peer_down",
    )(rows_flat, jnp.arange(N_OFFS, dtype=jnp.int32), h2, gate2, tab, expand, diag)


def _peer_constants():
    col = np.arange(STAGE_COLS)
    p, r = col // PAIR_ROWS, col % PAIR_ROWS
    slot = (p // PEER_TOPK) * 2 * PEER_TOPK + (r % 2) * PEER_TOPK + p % PEER_TOPK
    diag = (r[None, :] // 2 == np.arange(8)[:, None]).astype(np.float32)
    fold = (slot[:, None] == np.arange(SLOTS)[None, :]).astype(np.float32)
    gsum = lambda tt: jnp.asarray(np.arange(tt * 8)[None, :] // 8 == np.arange(tt)[:, None], _bf16)
    return (jnp.asarray(diag, _f32), jnp.asarray(fold, _bf16), jnp.asarray(fold.T, _bf16),
            gsum(PEER_TT), gsum(FUSE_TT))


FIN_TM = 512


def _final_norm_kernel(x_ref, y_ref, g_ref, o_ref):
    z = x_ref[...] + y_ref[...]
    o_ref[...] = z * lax.rsqrt(jnp.mean(z * z, axis=-1, keepdims=True) + EPS) * g_ref[...]


def _final_norm(x1, y, g):
    t = x1.shape[0]
    row = pl.BlockSpec((FIN_TM, D_MODEL), lambda i: (i, 0))
    return pl.pallas_call(
        _final_norm_kernel,
        grid=(t // FIN_TM,),
        in_specs=[row, row, pl.BlockSpec((1, D_MODEL), lambda i: (0, 0))],
        out_specs=row,
        out_shape=jax.ShapeDtypeStruct((t, D_MODEL), _f32),
        compiler_params=_cparams(("parallel",)),
        name="final_norm",
    )(x1, y, g.reshape(1, D_MODEL))


def _cmp_to_sel_counts(n_rows, n_sel):
    r = SEL_LEN // CMP_STRIDE
    c = CMP_LEN // CMP_STRIDE
    off = np.arange(n_rows)[:, None] - r * np.arange(n_sel)[None, :] + (c - 1)
    counts = np.zeros((n_rows, LANES), np.float32)
    for n in range(c):
        counts[:, :n_sel] += ((off - n >= 0) & (off - n < r)).astype(np.float32)
    counts[n_rows - 1] = 0.0
    return counts


def _attention(x2d, b, s, w_in, norm_mix_g, rel_bias, lam_rows, subln_g,
               cmp_pos_k, cmp_pos_v, cmp_w1_k, cmp_w2_k, cmp_w1_v, cmp_w2_v):
    dq, dv, nq, kcvc, vs2, vw2, ng, dkT, ksT, kwT = _in_proj(x2d, norm_mix_g, *_arrange_w_in(w_in))
    to3 = lambda a: a.reshape(b, s, a.shape[-1])

    bt_diff = _near_tiles(_bias_tiles(rel_bias[:, :DIFF_HEADS], DIFF_TQ, DIFF_TQ, 2))
    o_diff = _diff_attn(to3(dq), dkT, to3(dv), bt_diff, lam_rows, subln_g)

    n = s // CMP_STRIDE
    pos = jnp.stack([cmp_pos_k, cmp_pos_v]).reshape(2, 2, CMP_STRIDE, 1, NSA_HD)
    pos = jnp.concatenate([pos, pos], axis=-1)
    w1 = jnp.stack([cmp_w1_k, cmp_w1_v]).reshape(2, 2, CMP_STRIDE, NSA_HD, CMP_HIDDEN)
    zw = jnp.zeros_like(w1)
    w1 = jnp.concatenate([jnp.concatenate([w1, zw], axis=-1), jnp.concatenate([zw, w1], axis=-1)],
                         axis=-2).astype(_bf16)
    w2 = jnp.stack([cmp_w2_k, cmp_w2_v])
    w2dup = jnp.concatenate([w2, w2], axis=-1).astype(_bf16)
    cmp_kv = _compress(to3(kcvc), pos, w1, w2dup)

    n_sel = s // SEL_LEN
    counts = jnp.asarray(_cmp_to_sel_counts(n, n_sel), _bf16)
    nq3 = to3(nq)
    o_cmp, sel = _cmp_attn(nq3, cmp_kv[:, 0], cmp_kv[:, 1], counts, n_sel)

    nsa_bias = rel_bias[:, DIFF_HEADS:]
    tq = NSA_TQ
    def group_tiles(n_tiles, window=None):
        bt = _bias_tiles(nsa_bias, tq, tq, n_tiles, window=window)
        bt = bt.reshape(NSA_KV_HEADS, NSA_GROUP, n_tiles, tq, tq)[:, np.asarray(GROUP_ORDER)]
        return jnp.transpose(bt, (0, 2, 1, 3, 4)).reshape(NSA_KV_HEADS, n_tiles, NSA_GROUP * tq, tq)

    expand = jnp.asarray(np.arange(s)[None, :] // SEL_LEN == np.arange(LANES)[:, None], _bf16)
    o_sel = _sel_attn(nq3, ksT, to3(vs2), sel, expand, _near_tiles(group_tiles(2)))

    bt_win = jnp.transpose(group_tiles(WIN_TILES, window=WINDOW), (0, 2, 1, 3))
    bt_win = bt_win.reshape(NSA_KV_HEADS, NSA_GROUP * tq, WIN_TILES * tq)
    o_win = _win_attn(nq3, kwT, to3(vw2), bt_win)

    flat = lambda a3: a3.reshape(b * s, a3.shape[-1])
    return flat(o_diff), flat(o_cmp), flat(o_sel), flat(o_win), ng


def _gate_expand():
    ge = np.zeros((3, LANES, NSA_WIDTH), np.float32)
    for head in range(NSA_HEADS):
        for br in range(3):
            ge[br, 3 * head + br, head * NSA_HD:(head + 1) * NSA_HD] = 1.0
    return jnp.asarray(ge, _bf16)


PEER_PARTS = 8


def _peer(x1, xn, qp, peer_sub_keys, peer_u, peer_v):
    t = x1.shape[0]
    sk = peer_sub_keys.reshape(2 * PEER_HEADS, PEER_NKEYS, PEER_DKEY // 2).astype(_bf16)
    diag, fold, expand, gsum, gsum_fuse = _peer_constants()
    x8 = xn.reshape(t, 8, LANES)
    tab_u = _pair_table(peer_u)
    n = t // PEER_PARTS
    rows, gate = _peer_route(qp, sk, n)
    rows_all, gate_all, h_all = [rows], [gate], []
    for k in range(PEER_PARTS - 1):
        h, rows, gate = _peer_up_route(rows.reshape(n * NPICK), x8, tab_u, diag, fold, gsum_fuse, qp, sk,
                                       n, k * n, (k + 1) * n)
        h_all.append(h)
        rows_all.append(rows)
        gate_all.append(gate)
    h_all.append(_peer_up(rows.reshape(n * NPICK), x8, tab_u, diag, fold, gsum, n, (PEER_PARTS - 1) * n))
    rows_flat = jnp.concatenate(rows_all, axis=0).reshape(t * NPICK)
    y = _peer_down(rows_flat, jnp.concatenate(h_all, axis=0), jnp.concatenate(gate_all, axis=0),
                   _pair_table(peer_v), expand, diag)
    return y.reshape(t, D_MODEL)


def kernel(x, w_in, w_out, norm_mix_g, norm_ffn_g, final_norm_g, rel_bias, diff_lq1, diff_lk1, diff_lq2, diff_lk2, diff_subln_g, cmp_pos_k, cmp_pos_v, cmp_w1_k, cmp_w2_k, cmp_w1_v, cmp_w2_v, peer_w_q, peer_sub_keys, peer_u, peer_v):
    b, s, d = x.shape
    x2d = x.reshape(b * s, d)
    lam_rows = jnp.zeros((8, LANES), _f32).at[0:4, :DIFF_HD].set(
        jnp.stack([diff_lq1[0], diff_lk1[0], diff_lq2[0], diff_lk2[0]]))
    o_diff, o_cmp, o_sel, o_win, ng = _attention(
        x2d, b, s, w_in[0], norm_mix_g[0], rel_bias, lam_rows, diff_subln_g[0],
        cmp_pos_k[0], cmp_pos_v[0], cmp_w1_k[0], cmp_w2_k[0], cmp_w1_v[0], cmp_w2_v[0])
    x1, xn, qp = _mix_out(x2d, o_diff, o_cmp, o_sel, o_win, ng, _gate_expand(),
                          w_out[0].astype(_bf16), norm_ffn_g[0], peer_w_q[0].astype(_bf16))
    y = _peer(x1, xn, qp, peer_sub_keys[0], peer_u[0], peer_v[0])
    return _final_norm(x1, y, final_norm_g).reshape(b, s, d)
```
